```python
import jax, jax.numpy as jnp
from jax import lax
import numpy as np

D_MODEL = 1024
BATCH = 2
SEQ = 16384
DEPTH = 1
DEC_BATCH = 4
DEC_SEQ = 4096
PAST_LEN = 128

D_MIX = D_MODEL
GLA_WIDTH = D_MIX // 2
CONV_WIDTH = D_MIX - GLA_WIDTH
GLA_HEADS = 4
GLA_DV = GLA_WIDTH // GLA_HEADS
GLA_DK = (GLA_WIDTH // 2) // GLA_HEADS
GLA_KW = GLA_HEADS * GLA_DK
GLA_RANK = 16
GLA_GATE_NORM = 16.0
GLA_CHUNK = 64
CONV_K = 3
CONV_GROUPS = 8
CONV_GDIM = CONV_WIDTH // CONV_GROUPS
IN_SIZES = (GLA_KW, GLA_KW, GLA_WIDTH, GLA_WIDTH, GLA_RANK, GLA_RANK, CONV_WIDTH, CONV_WIDTH, CONV_WIDTH)
IN_COLS = sum(IN_SIZES)
IN_SPLITS = tuple(int(s) for s in np.cumsum(IN_SIZES)[:-1])
PEER_HEADS = 8
PEER_NKEYS = 128
PEER_EXPERTS = PEER_NKEYS * PEER_NKEYS
PEER_DQ = 256
PEER_DQH = PEER_DQ // 2
PEER_TOPK = 16
PEER_BLOCK = 128
N_ADA = 6
EPS = 1e-6

kernel_name = "hybrid_gla_shortconv_peer_encoder"


def rmsnorm(x, g):
    xf = x.astype(jnp.float32)
    r = lax.rsqrt(jnp.mean(xf * xf, axis=-1, keepdims=True) + EPS)
    return (xf * r).astype(x.dtype) * g


def modulate(h, shift, scale):
    return h * (1 + scale[:, None, :]) + shift[:, None, :]


def gla_chunked(q, k, v, log_a):
    bsz, nh, seq, dk = q.shape
    dv = v.shape[-1]
    n = seq // GLA_CHUNK
    q = q.reshape(bsz, nh, n, GLA_CHUNK, dk)
    k = k.reshape(bsz, nh, n, GLA_CHUNK, dk)
    vc = v.reshape(bsz, nh, n, GLA_CHUNK, dv)
    b = jnp.cumsum(log_a.astype(jnp.float32).reshape(bsz, nh, n, GLA_CHUNK, dk), axis=3)
    b_last = b[..., -1:, :]
    q_dec = q * jnp.exp(b)
    k_dec = k * jnp.exp(-b)
    k_end = k * jnp.exp(b_last - b)
    mask = jnp.tril(jnp.ones((GLA_CHUNK, GLA_CHUNK), dtype=bool))
    scores = jnp.where(mask, jnp.einsum('bhncd,bhnsd->bhncs', q_dec, k_dec), 0.0)
    o_intra = jnp.einsum('bhncs,bhnse->bhnce', scores, vc)
    s_chunk = jnp.einsum('bhnsd,bhnse->bhnde', k_end, vc)
    decay = jnp.exp(b_last[..., 0, :])

    def step(state, inp):
        s_c, d_c = inp
        return d_c[..., None] * state + s_c, state

    s0 = jnp.zeros((bsz, nh, dk, dv), dtype=s_chunk.dtype)
    _, s_before = lax.scan(step, s0, (jnp.moveaxis(s_chunk, 2, 0), jnp.moveaxis(decay, 2, 0)))
    s_before = jnp.moveaxis(s_before, 0, 2)
    o = o_intra + jnp.einsum('bhncd,bhnde->bhnce', q_dec, s_before)
    return o.reshape(bsz, nh, seq, dv).astype(v.dtype)


def to_heads(t, nh):
    bsz, seq, w = t.shape
    return t.reshape(bsz, seq, nh, w // nh).transpose(0, 2, 1, 3)


def hybrid_mixer(h, w_in, w_dec_f, b_dec_f, w_dec_b, b_dec_b, gla_norm_g, conv_w, conv_norm_g, w_out):
    bsz, seq, _ = h.shape
    proj = h @ w_in
    q, k, v, g, alf, alb, cb, cc, ch = jnp.split(proj, IN_SPLITS, axis=-1)
    la_f = jax.nn.log_sigmoid((alf @ w_dec_f + b_dec_f).astype(jnp.float32)) / GLA_GATE_NORM
    la_b = jax.nn.log_sigmoid((alb @ w_dec_b + b_dec_b).astype(jnp.float32)) / GLA_GATE_NORM
    qh = to_heads(q * (GLA_DK ** -0.5), GLA_HEADS)
    kh = to_heads(k, GLA_HEADS)
    vh = to_heads(v, GLA_HEADS)
    o_f = gla_chunked(qh, kh, vh, to_heads(la_f, GLA_HEADS))
    rev = lambda t: jnp.flip(t, axis=2)
    o_b = rev(gla_chunked(rev(qh), rev(kh), rev(vh), rev(to_heads(la_b, GLA_HEADS))))
    o = rmsnorm(o_f + o_b, gla_norm_g)
    y_gla = o.transpose(0, 2, 1, 3).reshape(bsz, seq, GLA_WIDTH) * jax.nn.silu(g)
    z = jnp.pad(cc * ch, ((0, 0), (1, 1), (0, 0)))
    conv = conv_w[0] * z[:, :-2] + conv_w[1] * z[:, 1:-1] + conv_w[2] * z[:, 2:]
    yc = (cb * conv).reshape(bsz, seq, CONV_GROUPS, CONV_GDIM)
    y_conv = rmsnorm(yc, conv_norm_g.reshape(CONV_GROUPS, CONV_GDIM)).reshape(bsz, seq, CONV_WIDTH)
    return jnp.concatenate([y_gla, y_conv], axis=-1) @ w_out


def peer(h, wq, subkeys, u, v):
    bsz, seq, d = h.shape
    blocks = h.reshape(-1, PEER_BLOCK, d)

    def block(xb):
        t = xb.shape[0]
        qf = (xb @ wq).reshape(t, PEER_HEADS, 2, PEER_DQH)
        s = jnp.einsum('thpd,hpkd->thpk', qf, subkeys).astype(jnp.float32)
        sv, si = lax.top_k(s, PEER_TOPK)
        cand = sv[:, :, 0, :, None] + sv[:, :, 1, None, :]
        cidx = si[:, :, 0, :, None] * PEER_NKEYS + si[:, :, 1, None, :]
        cv, ci = lax.top_k(cand.reshape(t, PEER_HEADS, PEER_TOPK * PEER_TOPK), PEER_TOPK)
        eidx = jnp.take_along_axis(cidx.reshape(t, PEER_HEADS, PEER_TOPK * PEER_TOPK), ci, axis=-1)
        gate = jax.nn.softmax(cv, axis=-1)
        ue = jnp.take(u, eidx, axis=0)
        ve = jnp.take(v, eidx, axis=0)
        act = jax.nn.gelu(jnp.einsum('td,thkd->thk', xb, ue))
        return jnp.einsum('thk,thkd->td', (gate * act.astype(jnp.float32)).astype(xb.dtype), ve)

    return lax.map(block, blocks).reshape(bsz, seq, d)


def trunk(x, c, norm1_g, norm2_g, w_ada, b_ada, w_in, w_dec_f, b_dec_f, w_dec_b, b_dec_b,
          gla_norm_g, conv_w, conv_norm_g, w_out, peer_wq, peer_subkeys, peer_u, peer_v, normf_g):
    for l in range(DEPTH):
        ada = jax.nn.silu(c) @ w_ada[l] + b_ada[l]
        sh1, sc1, gt1, sh2, sc2, gt2 = jnp.split(ada, N_ADA, axis=-1)
        h = modulate(rmsnorm(x, norm1_g[l]), sh1, sc1)
        y = hybrid_mixer(h, w_in[l], w_dec_f[l], b_dec_f[l], w_dec_b[l], b_dec_b[l],
                         gla_norm_g[l], conv_w[l], conv_norm_g[l], w_out[l])
        x = x + gt1[:, None, :] * y
        h = modulate(rmsnorm(x, norm2_g[l]), sh2, sc2)
        x = x + gt2[:, None, :] * peer(h, peer_wq[l], peer_subkeys[l], peer_u[l], peer_v[l])
    return rmsnorm(x, normf_g)


def setup_inputs(seed: int = 0) -> dict:
    key = jax.random.key(seed)
    ks = jax.random.split(key, 24)
    nrm = lambda k, shape, s: jax.random.normal(k, shape, dtype=jnp.float32) * s
    L = DEPTH
    D = D_MODEL
    return {
        "x_prompt": nrm(ks[0], (BATCH, SEQ, D), 1.0),
        "x_sample": nrm(ks[1], (DEC_BATCH, DEC_SEQ, D), 1.0),
        "c_prompt": nrm(ks[2], (BATCH, D), 1.0),
        "c_sample": nrm(ks[3], (DEC_BATCH, D), 1.0),
        "norm1_g": 1.0 + nrm(ks[4], (L, D), 0.02),
        "norm2_g": 1.0 + nrm(ks[5], (L, D), 0.02),
        "w_ada": nrm(ks[6], (L, D, N_ADA * D), 0.5 * D ** -0.5),
        "b_ada": nrm(ks[7], (L, N_ADA * D), 0.02),
        "w_in": nrm(ks[8], (L, D, IN_COLS), D ** -0.5),
        "w_dec_f": nrm(ks[9], (L, GLA_RANK, GLA_KW), GLA_RANK ** -0.5),
        "b_dec_f": nrm(ks[10], (L, GLA_KW), 0.1),
        "w_dec_b": nrm(ks[11], (L, GLA_RANK, GLA_KW), GLA_RANK ** -0.5),
        "b_dec_b": nrm(ks[12], (L, GLA_KW), 0.1),
        "gla_norm_g": 1.0 + nrm(ks[13], (L, GLA_DV), 0.02),
        "conv_w": nrm(ks[14], (L, CONV_K, CONV_WIDTH), CONV_K ** -0.5),
        "conv_norm_g": 1.0 + nrm(ks[15], (L, CONV_WIDTH), 0.02),
        "w_out": nrm(ks[16], (L, D_MIX, D), D_MIX ** -0.5),
        "peer_wq": nrm(ks[17], (L, D, PEER_HEADS * PEER_DQ), D ** -0.5),
        "peer_subkeys": nrm(ks[18], (L, PEER_HEADS, 2, PEER_NKEYS, PEER_DQH), PEER_DQH ** -0.5),
        "peer_u": nrm(ks[19], (L, PEER_EXPERTS, D), D ** -0.5),
        "peer_v": nrm(ks[20], (L, PEER_EXPERTS, D), PEER_HEADS ** -0.5),
        "normf_g": 1.0 + nrm(ks[21], (D,), 0.02),
    }


def reference(x_prompt, x_sample, c_prompt, c_sample, norm1_g, norm2_g, w_ada, b_ada, w_in,
              w_dec_f, b_dec_f, w_dec_b, b_dec_b, gla_norm_g, conv_w, conv_norm_g, w_out,
              peer_wq, peer_subkeys, peer_u, peer_v, normf_g):
    y_prompt = trunk(x_prompt, c_prompt, norm1_g, norm2_g, w_ada, b_ada, w_in, w_dec_f, b_dec_f,
                     w_dec_b, b_dec_b, gla_norm_g, conv_w, conv_norm_g, w_out,
                     peer_wq, peer_subkeys, peer_u, peer_v, normf_g)
    y_sample = trunk(x_sample, c_sample, norm1_g, norm2_g, w_ada, b_ada, w_in, w_dec_f, b_dec_f,
                     w_dec_b, b_dec_b, gla_norm_g, conv_w, conv_norm_g, w_out,
                     peer_wq, peer_subkeys, peer_u, peer_v, normf_g)
    return (y_prompt, y_sample)
```

```python
import functools

import jax
import jax.numpy as jnp
from jax import lax
from jax.experimental import pallas as pl
from jax.experimental.pallas import tpu as pltpu

F32 = jnp.float32
BF16 = jnp.bfloat16

D_MODEL = 1024
GLA_HEADS = 4
GLA_DV = 128
GLA_DK = 64
GLA_KW = GLA_HEADS * GLA_DK
GLA_WIDTH = GLA_HEADS * GLA_DV
GLA_RANK = 16
GLA_GATE_NORM = 16.0
GLA_CHUNK = 64
CONV_WIDTH = 512
CONV_GDIM = 64
PEER_HEADS = 8
PEER_NKEYS = 128
PEER_DQH = 128
PEER_TOPK = 16
N_ADA = 6
EPS = 1e-6

LANES = 128
SUBLANES = 8
VMEM_LIMIT = 48 * 1024 * 1024

C_QK, C_V, C_G, C_CB, C_CC, C_CH, C_AL, C_END = 0, 512, 1024, 1536, 2048, 2560, 3072, 3200

TOKEN_TILE = 256
EXPERT_BLOCK = 1024
GATE_PITCH = PEER_NKEYS + SUBLANES


def _dot(a, b):
    return jnp.dot(a, b, preferred_element_type=F32)


def _dot_nt(a, b):
    return lax.dot_general(a, b, (((1,), (1,)), ((), ())), preferred_element_type=F32)


def _dot_tn(a, b):
    return lax.dot_general(a, b, (((0,), (0,)), ((), ())), preferred_element_type=F32)


def _split2(x):
    hi = x.astype(BF16)
    lo = (x - hi.astype(F32)).astype(BF16)
    return hi, lo


def _split3(x):
    hi = x.astype(BF16)
    r = x - hi.astype(F32)
    mid = r.astype(BF16)
    lo = (r - mid.astype(F32)).astype(BF16)
    return hi, mid, lo


def _dot_f32(a, b):
    a_hi, a_lo = _split2(a)
    b_hi, b_lo = _split2(b)
    return _dot(a_hi, b_hi) + _dot(a_hi, b_lo) + _dot(a_lo, b_hi)


def _rms(x):
    return lax.rsqrt(jnp.mean(x * x, axis=-1, keepdims=True) + EPS)


def _sigmoid(x):
    return 1.0 / (1.0 + jnp.exp(-x))


def _log_sigmoid(x):
    return jnp.minimum(x, 0.0) - jnp.log1p(jnp.exp(-jnp.abs(x)))


def _gelu_tanh(x):
    return 0.5 * x * (1.0 + jnp.tanh(0.7978845608028654 * (x + 0.044715 * (x * x * x))))


def _ada_kernel(c_ref, w_ref, b_ref, o_ref):
    c = c_ref[...]
    o_ref[...] = _dot_f32(c * _sigmoid(c), w_ref[...]) + b_ref[...]


def _ada(c_pad, w_ada, b_ada):
    n_col = N_ADA * D_MODEL
    blk = 1536
    return pl.pallas_call(
        _ada_kernel,
        out_shape=jax.ShapeDtypeStruct((SUBLANES, n_col), F32),
        grid=(n_col // blk,),
        in_specs=[
            pl.BlockSpec((SUBLANES, D_MODEL), lambda j: (0, 0)),
            pl.BlockSpec((D_MODEL, blk), lambda j: (0, j)),
            pl.BlockSpec((1, blk), lambda j: (0, j)),
        ],
        out_specs=pl.BlockSpec((SUBLANES, blk), lambda j: (0, j)),
        compiler_params=pltpu.CompilerParams(dimension_semantics=("arbitrary",), vmem_limit_bytes=VMEM_LIMIT),
        name="ada",
    )(c_pad, w_ada, b_ada)


def _gla_chunk(q_c, k_c, v_c, la_c, tri, st_ref, forward):
    la_hi, la_mid, la_lo = _split3(la_c)
    b = _dot(tri, la_hi) + _dot(tri, la_mid) + _dot(tri, la_lo)
    b_end = b[GLA_CHUNK - 1:GLA_CHUNK, :] if forward else b[0:1, :]
    q_dec = q_c * (jnp.exp(b) * (GLA_DK ** -0.5))
    k_dec = k_c * jnp.exp(-b)
    k_end = k_c * jnp.exp(b_end - b)
    decay = jnp.exp(b_end)
    row = lax.broadcasted_iota(jnp.int32, (GLA_CHUNK, GLA_CHUNK), 0)
    col = lax.broadcasted_iota(jnp.int32, (GLA_CHUNK, GLA_CHUNK), 1)
    causal = (row >= col) if forward else (row <= col)
    lane = lax.broadcasted_iota(jnp.int32, (1, LANES), 1)
    outs = []
    for head in range(GLA_HEADS):
        pair, half = head // 2, head % 2
        sl = slice(pair * LANES, (pair + 1) * LANES)
        in_head = (lane // GLA_DK) == half
        qd = jnp.where(in_head, q_dec[:, sl], 0.0).astype(BF16)
        kd = k_dec[:, sl].astype(BF16)
        ke = k_end[:, sl].astype(BF16)
        v_h = v_c[:, head * GLA_DV:(head + 1) * GLA_DV].astype(BF16)
        scores = jnp.where(causal, _dot_nt(qd, kd), 0.0)
        st = st_ref[head]
        o = _dot(scores.astype(BF16), v_h) + _dot_nt(qd, st.astype(BF16))
        st_ref[head] = st * decay[:, sl] + _dot_tn(v_h, ke)
        outs.append(o)
    return jnp.concatenate(outs, axis=-1)


def _mix1_kernel(x_ref, xp_ref, xn_ref, mod_ref, n1g_ref, win_ref, wdec_ref, bdec_ref, convw_ref, cng_ref,
                 mgrp_ref, qk_ref, v_ref, lab_ref, of_ref, gy_ref, laf_s, st_s, *, tt, nt):
    i = pl.program_id(1)
    mod = mod_ref[0]
    sh1, sc1 = mod[0:1], mod[1:2]
    n1g = n1g_ref[...]

    def norm_mod(x):
        return ((x * _rms(x)) * n1g) * (1.0 + sc1) + sh1

    h = norm_mod(x_ref[0]).astype(BF16)
    qk_ref[0] = _dot(h, win_ref[:, C_QK:C_V])
    v_ref[0] = _dot(h, win_ref[:, C_V:C_G])
    gy_ref[0, :, 0:GLA_WIDTH] = _dot(h, win_ref[:, C_G:C_CB])

    hh = norm_mod(jnp.concatenate([xp_ref[0], xn_ref[0]], axis=0)).astype(BF16)
    z_halo = _dot(hh, win_ref[:, C_CC:C_CH]) * _dot(hh, win_ref[:, C_CH:C_AL])
    z_prev = jnp.where(i > 0, z_halo[SUBLANES - 1:SUBLANES, :], 0.0)
    z_next = jnp.where(i < nt - 1, z_halo[SUBLANES:SUBLANES + 1, :], 0.0)
    z = _dot(h, win_ref[:, C_CC:C_CH]) * _dot(h, win_ref[:, C_CH:C_AL])
    row = lax.broadcasted_iota(jnp.int32, (tt, 1), 0)
    z_m1 = jnp.where(row == 0, z_prev, pltpu.roll(z, 1, axis=0))
    z_p1 = jnp.where(row == tt - 1, z_next, pltpu.roll(z, tt - 1, axis=0))
    cw = convw_ref[...]
    conv = cw[0:1] * z_m1 + cw[1:2] * z + cw[2:3] * z_p1
    yc = _dot(h, win_ref[:, C_CB:C_CC]) * conv
    sq_hi, sq_lo = _split2(yc * yc)
    ss = _dot(sq_hi, mgrp_ref[...]) + _dot(sq_lo, mgrp_ref[...])
    gy_ref[0, :, GLA_WIDTH:] = (yc * lax.rsqrt(ss * (1.0 / CONV_GDIM) + EPS)) * cng_ref[...]

    zd = _dot_f32(_dot(h, win_ref[:, C_AL:C_END]), wdec_ref[...]) + bdec_ref[...]
    la = _log_sigmoid(zd) * (1.0 / GLA_GATE_NORM)
    laf_s[...] = la[:, 0:GLA_KW]
    lab_ref[0] = la[:, GLA_KW:]

    @pl.when(i == 0)
    def _():
        st_s[...] = jnp.zeros_like(st_s)

    r_i = lax.broadcasted_iota(jnp.int32, (GLA_CHUNK, GLA_CHUNK), 0)
    c_i = lax.broadcasted_iota(jnp.int32, (GLA_CHUNK, GLA_CHUNK), 1)
    tri = jnp.where(r_i >= c_i, 1.0, 0.0).astype(BF16)

    def chunk(c, carry):
        r0 = pl.multiple_of(c * GLA_CHUNK, GLA_CHUNK)
        rows = pl.ds(r0, GLA_CHUNK)
        o = _gla_chunk(qk_ref[0, rows, 0:GLA_KW], qk_ref[0, rows, GLA_KW:], v_ref[0, rows, :],
                       laf_s[rows, :], tri, st_s, True)
        of_ref[0, rows, :] = o
        return carry

    lax.fori_loop(0, tt // GLA_CHUNK, chunk, 0)


def _mix1(x, mod, n1g, win, wdec, bdec, convw, cng, mgrp, tt):
    bsz, seq, _ = x.shape
    nt = seq // tt
    hb = tt // SUBLANES
    last_hb = seq // SUBLANES - 1
    const = lambda shape: pl.BlockSpec(shape, lambda b, i: tuple(0 for _ in shape))
    tile = lambda w: pl.BlockSpec((1, tt, w), lambda b, i: (b, i, 0))
    out_w = (2 * GLA_KW, GLA_WIDTH, GLA_KW, GLA_WIDTH, GLA_WIDTH + CONV_WIDTH)
    return pl.pallas_call(
        functools.partial(_mix1_kernel, tt=tt, nt=nt),
        out_shape=tuple(jax.ShapeDtypeStruct((bsz, seq, w), F32) for w in out_w),
        grid=(bsz, nt),
        in_specs=[
            tile(D_MODEL),
            pl.BlockSpec((1, SUBLANES, D_MODEL), lambda b, i: (b, jnp.maximum(i * hb - 1, 0), 0)),
            pl.BlockSpec((1, SUBLANES, D_MODEL), lambda b, i: (b, jnp.minimum((i + 1) * hb, last_hb), 0)),
            pl.BlockSpec((1, SUBLANES, D_MODEL), lambda b, i: (b, 0, 0)),
            const((1, D_MODEL)),
            const((D_MODEL, C_END)),
            const((LANES, 2 * GLA_KW)),
            const((1, 2 * GLA_KW)),
            const((3, CONV_WIDTH)),
            const((1, CONV_WIDTH)),
            const((CONV_WIDTH, CONV_WIDTH)),
        ],
        out_specs=tuple(tile(w) for w in out_w),
        scratch_shapes=[pltpu.VMEM((tt, GLA_KW), F32), pltpu.VMEM((GLA_HEADS, GLA_DV, LANES), F32)],
        compiler_params=pltpu.CompilerParams(dimension_semantics=("arbitrary", "arbitrary"),
                                             vmem_limit_bytes=VMEM_LIMIT),
        name="mix1",
    )(x, x, x, mod, n1g, win, wdec, bdec, convw, cng, mgrp)


def _mix2_kernel(x_ref, mod_ref, qk_ref, v_ref, lab_ref, of_ref, gy_ref, gng_ref, wout_ref, n2g_ref,
                 x1_ref, h2_ref, ob_s, st_s, *, tt):
    i = pl.program_id(1)

    @pl.when(i == 0)
    def _():
        st_s[...] = jnp.zeros_like(st_s)

    r_i = lax.broadcasted_iota(jnp.int32, (GLA_CHUNK, GLA_CHUNK), 0)
    c_i = lax.broadcasted_iota(jnp.int32, (GLA_CHUNK, GLA_CHUNK), 1)
    tri = jnp.where(r_i <= c_i, 1.0, 0.0).astype(BF16)
    n_chunk = tt // GLA_CHUNK

    def chunk(ci, carry):
        r0 = pl.multiple_of((n_chunk - 1 - ci) * GLA_CHUNK, GLA_CHUNK)
        rows = pl.ds(r0, GLA_CHUNK)
        ob_s[rows, :] = _gla_chunk(qk_ref[0, rows, 0:GLA_KW], qk_ref[0, rows, GLA_KW:], v_ref[0, rows, :],
                                   lab_ref[0, rows, :], tri, st_s, False)
        return carry

    lax.fori_loop(0, n_chunk, chunk, 0)

    mod = mod_ref[0]
    gt1, sh2, sc2 = mod[2:3], mod[3:4], mod[4:5]
    o = of_ref[0] + ob_s[...]
    gng = gng_ref[...]
    y = None
    for head in range(GLA_HEADS):
        sl = slice(head * GLA_DV, (head + 1) * GLA_DV)
        oh = o[:, sl]
        g = gy_ref[0, :, sl]
        yh = ((oh * _rms(oh)) * gng) * (g * _sigmoid(g))
        part = _dot(yh.astype(BF16), wout_ref[sl, :])
        y = part if y is None else y + part
    y = y + _dot(gy_ref[0, :, GLA_WIDTH:].astype(BF16), wout_ref[GLA_WIDTH:, :])
    x1 = x_ref[0] + gt1 * y
    x1_ref[0] = x1
    h2_ref[0] = (((x1 * _rms(x1)) * n2g_ref[...]) * (1.0 + sc2) + sh2).astype(BF16)


def _mix2(x, mod, qk, v, lab, of, gy, gng, wout, n2g, tt):
    bsz, seq, _ = x.shape
    nt = seq // tt
    const = lambda shape: pl.BlockSpec(shape, lambda b, i: tuple(0 for _ in shape))
    tile = lambda w: pl.BlockSpec((1, tt, w), lambda b, i: (b, nt - 1 - i, 0))
    return pl.pallas_call(
        functools.partial(_mix2_kernel, tt=tt),
        out_shape=(jax.ShapeDtypeStruct((bsz, seq, D_MODEL), F32), jax.ShapeDtypeStruct((bsz, seq, D_MODEL), BF16)),
        grid=(bsz, nt),
        in_specs=[
            tile(D_MODEL),
            pl.BlockSpec((1, SUBLANES, D_MODEL), lambda b, i: (b, 0, 0)),
            tile(2 * GLA_KW), tile(GLA_WIDTH), tile(GLA_KW), tile(GLA_WIDTH), tile(GLA_WIDTH + CONV_WIDTH),
            const((1, GLA_DV)),
            const((D_MODEL, D_MODEL)),
            const((1, D_MODEL)),
        ],
        out_specs=(tile(D_MODEL), tile(D_MODEL)),
        scratch_shapes=[pltpu.VMEM((tt, GLA_WIDTH), F32), pltpu.VMEM((GLA_HEADS, GLA_DV, LANES), F32)],
        compiler_params=pltpu.CompilerParams(dimension_semantics=("arbitrary", "arbitrary"),
                                             vmem_limit_bytes=VMEM_LIMIT),
        name="mix2",
    )(x, mod, qk, v, lab, of, gy, gng, wout, n2g)


NEG_INF = float("-inf")
N_CAND = 10 * SUBLANES


def _top16_rows(s, sv_ref, si_ref):
    n = s.shape[0]
    rid = lax.broadcasted_iota(jnp.int32, s.shape, 0).astype(F32)
    for k in range(PEER_TOPK):
        m = jnp.max(s, axis=0, keepdims=True)
        idx = jnp.min(jnp.where(s == m, rid, float(n)), axis=0, keepdims=True)
        s = jnp.where(rid == idx, NEG_INF, s)
        sv_ref[k:k + 1, :] = m
        si_ref[k:k + 1, :] = idx


def _route_kernel(h2_ref, wq_ref, sk_ref, a_ref, b_ref, g_ref, sv_s, si_s, pa_s, pb_s, pg_s, *, tt):
    h2 = h2_ref[0]
    sub = lax.broadcasted_iota(jnp.int32, (SUBLANES, 1), 0).astype(F32)
    for head in range(PEER_HEADS):
        for p in range(2):
            hp = head * 2 + p
            q = _dot(h2, wq_ref[:, hp * PEER_DQH:(hp + 1) * PEER_DQH])
            q_hi, q_lo = _split2(q)
            k_hi, k_lo = _split2(sk_ref[hp])
            s = _dot_nt(k_hi, q_hi) + _dot_nt(k_hi, q_lo) + _dot_nt(k_lo, q_hi)
            _top16_rows(s, sv_s.at[p], si_s.at[p])
        sv1, sv2 = sv_s[0], sv_s[1]
        si1, si2 = si_s[0], si_s[1]
        bc = lambda r: jnp.broadcast_to(r, (SUBLANES, tt))
        cand, code, ca, cb = [], [], [], []
        for k1, k2_0 in [(0, 0), (0, 8)] + [(k, 0) for k in range(1, 8)]:
            cand.append(bc(sv1[k1:k1 + 1]) + sv2[k2_0:k2_0 + SUBLANES])
            code.append(sub + float(k1 * PEER_TOPK + k2_0))
            ca.append(bc(si1[k1:k1 + 1]))
            cb.append(si2[k2_0:k2_0 + SUBLANES])
        cand.append(sv1[SUBLANES:] + bc(sv2[0:1]))
        code.append((sub + float(SUBLANES)) * float(PEER_TOPK))
        ca.append(si1[SUBLANES:])
        cb.append(bc(si2[0:1]))
        cand = jnp.concatenate(cand, axis=0)
        code = jnp.broadcast_to(jnp.concatenate(code, axis=0), (N_CAND, tt))
        ca = jnp.concatenate(ca, axis=0)
        cb = jnp.concatenate(cb, axis=0)
        vals = []
        for k in range(PEER_TOPK):
            m = jnp.max(cand, axis=0, keepdims=True)
            sel = jnp.min(jnp.where(cand == m, code, 1e9), axis=0, keepdims=True)
            hit = code == sel
            r = head * PEER_TOPK + k
            pa_s[r:r + 1, :] = jnp.sum(jnp.where(hit, ca, 0.0), axis=0, keepdims=True)
            pb_s[r:r + 1, :] = jnp.sum(jnp.where(hit, cb, 0.0), axis=0, keepdims=True)
            cand = jnp.where(hit, NEG_INF, cand)
            vals.append(m)
        es = [jnp.exp(m - vals[0]) for m in vals]
        tot = es[0]
        for e in es[1:]:
            tot = tot + e
        inv = 1.0 / tot
        for k in range(PEER_TOPK):
            r = head * PEER_TOPK + k
            pg_s[r:r + 1, :] = es[k] * inv
    a_ref[0] = jnp.transpose(pa_s[...]).astype(jnp.int32)
    b_ref[0] = jnp.transpose(pb_s[...]).astype(jnp.int32)
    g_ref[0] = jnp.transpose(pg_s[...])


def _route(h2, wq, subkeys, tt):
    bsz, seq, _ = h2.shape
    n_pair = PEER_HEADS * PEER_TOPK
    tile = lambda w: pl.BlockSpec((1, tt, w), lambda b, i: (b, i, 0))
    return pl.pallas_call(
        functools.partial(_route_kernel, tt=tt),
        out_shape=(jax.ShapeDtypeStruct((bsz, seq, n_pair), jnp.int32),
                   jax.ShapeDtypeStruct((bsz, seq, n_pair), jnp.int32),
                   jax.ShapeDtypeStruct((bsz, seq, n_pair), F32)),
        grid=(bsz, seq // tt),
        in_specs=[
            tile(D_MODEL),
            pl.BlockSpec((D_MODEL, 2 * PEER_HEADS * PEER_DQH), lambda b, i: (0, 0)),
            pl.BlockSpec((2 * PEER_HEADS, PEER_NKEYS, PEER_DQH), lambda b, i: (0, 0, 0)),
        ],
        out_specs=(tile(n_pair), tile(n_pair), tile(n_pair)),
        scratch_shapes=[pltpu.VMEM((2, PEER_TOPK, tt), F32), pltpu.VMEM((2, PEER_TOPK, tt), F32),
                        pltpu.VMEM((n_pair, tt), F32), pltpu.VMEM((n_pair, tt), F32), pltpu.VMEM((n_pair, tt), F32)],
        compiler_params=pltpu.CompilerParams(dimension_semantics=("arbitrary", "arbitrary"),
                                             vmem_limit_bytes=VMEM_LIMIT),
        name="route",
    )(h2, wq, subkeys)


def _peer_kernel(h2_ref, x1_ref, mod_ref, a_ref, b_ref, g_ref, ut_ref, v_ref, nfg_ref, y_ref, gate_s, acc_s,
                 *, tt, eb, ne):
    j = pl.program_id(2)

    @pl.when(j == 0)
    def _():
        acc_s[...] = jnp.zeros_like(acc_s)
        sub = lax.broadcasted_iota(jnp.int32, (PEER_NKEYS, PEER_NKEYS), 0)

        def build(t, carry):
            a_row = a_ref[0, pl.ds(t, 1), :]
            b_row = b_ref[0, pl.ds(t, 1), :]
            g_row = g_ref[0, pl.ds(t, 1), :]
            g_hi, g_lo = _split2(g_row)
            at = jnp.where(sub == a_row, 1.0, 0.0).astype(BF16)
            hit_b = sub == b_row
            bt_hi = jnp.where(hit_b, g_hi.astype(F32), 0.0).astype(BF16)
            bt_lo = jnp.where(hit_b, g_lo.astype(F32), 0.0).astype(BF16)
            lhs = jnp.concatenate([at, at], axis=1)
            rhs = jnp.concatenate([bt_hi, bt_lo], axis=1)
            r0 = pl.multiple_of(t * GATE_PITCH, SUBLANES)
            gate_s[pl.ds(r0, PEER_NKEYS), :] = _dot_nt(lhs, rhs)
            return carry

        lax.fori_loop(0, tt, build, 0)

    s = _dot(h2_ref[0], ut_ref[...])
    n_grp = eb // PEER_NKEYS
    parts = []
    for gi in range(n_grp):
        i1 = j * n_grp + gi
        gate = gate_s[pl.ds(i1, tt, stride=GATE_PITCH), :]
        parts.append((gate * _gelu_tanh(s[:, gi * PEER_NKEYS:(gi + 1) * PEER_NKEYS])).astype(BF16))
    acc_s[...] += _dot(jnp.concatenate(parts, axis=1), v_ref[...])

    @pl.when(j == ne - 1)
    def _():
        gt2 = mod_ref[0][5:6]
        x2 = x1_ref[0] + gt2 * acc_s[...]
        y_ref[0] = (x2 * _rms(x2)) * nfg_ref[...]


def _peer(h2, x1, mod, a_idx, b_idx, gates, ut, v, nfg, tt, eb):
    bsz, seq, _ = h2.shape
    n_exp = v.shape[0]
    ne = n_exp // eb
    n_pair = PEER_HEADS * PEER_TOPK
    tile = lambda w: pl.BlockSpec((1, tt, w), lambda b, i, j: (b, i, 0))
    return pl.pallas_call(
        functools.partial(_peer_kernel, tt=tt, eb=eb, ne=ne),
        out_shape=jax.ShapeDtypeStruct((bsz, seq, D_MODEL), F32),
        grid=(bsz, seq // tt, ne),
        in_specs=[
            tile(D_MODEL), tile(D_MODEL),
            pl.BlockSpec((1, SUBLANES, D_MODEL), lambda b, i, j: (b, 0, 0)),
            tile(n_pair), tile(n_pair), tile(n_pair),
            pl.BlockSpec((D_MODEL, eb), lambda b, i, j: (0, j)),
            pl.BlockSpec((eb, D_MODEL), lambda b, i, j: (j, 0)),
            pl.BlockSpec((1, D_MODEL), lambda b, i, j: (0, 0)),
        ],
        out_specs=tile(D_MODEL),
        scratch_shapes=[pltpu.VMEM((tt * GATE_PITCH, PEER_NKEYS), F32), pltpu.VMEM((tt, D_MODEL), F32)],
        compiler_params=pltpu.CompilerParams(dimension_semantics=("arbitrary", "arbitrary", "arbitrary"),
                                             vmem_limit_bytes=VMEM_LIMIT),
        name="peer",
    )(h2, x1, mod, a_idx, b_idx, gates, ut, v, nfg)


def _prep_weights(norm1_g, norm2_g, w_in, w_dec_f, b_dec_f, w_dec_b, b_dec_b, gla_norm_g, conv_w, conv_norm_g,
                  w_out, peer_wq, peer_subkeys, peer_u, peer_v, normf_g):
    w = w_in[0]
    q, k, v, g, alf, alb, cb, cc, ch = jnp.split(w, (256, 512, 1024, 1536, 1552, 1568, 2080, 2592), axis=-1)
    pad = jnp.zeros((D_MODEL, C_END - C_AL - 2 * GLA_RANK), F32)
    win = jnp.concatenate([q, k, v, g, cb, cc, ch, alf, alb, pad], axis=-1).astype(BF16)
    wdec = jnp.zeros((LANES, 2 * GLA_KW), F32)
    wdec = wdec.at[0:GLA_RANK, 0:GLA_KW].set(w_dec_f[0]).at[GLA_RANK:2 * GLA_RANK, GLA_KW:].set(w_dec_b[0])
    bdec = jnp.concatenate([b_dec_f[0], b_dec_b[0]])[None, :]
    grp = jnp.arange(CONV_WIDTH) // CONV_GDIM
    mgrp = (grp[:, None] == grp[None, :]).astype(BF16)
    return dict(
        n1g=norm1_g[0][None, :], n2g=norm2_g[0][None, :], win=win, wdec=wdec, bdec=bdec,
        convw=conv_w[0], cng=conv_norm_g[0][None, :], mgrp=mgrp, gng=gla_norm_g[0][None, :],
        wout=w_out[0].astype(BF16), wq=peer_wq[0].astype(BF16),
        subkeys=peer_subkeys[0].reshape(2 * PEER_HEADS, PEER_NKEYS, PEER_DQH),
        ut=jnp.transpose(peer_u[0]).astype(BF16), v=peer_v[0].astype(BF16), nfg=normf_g[None, :],
    )


def _trunk(x, mod, w, tt, eb):
    qk, v, lab, of, gy = _mix1(x, mod, w["n1g"], w["win"], w["wdec"], w["bdec"], w["convw"], w["cng"], w["mgrp"], tt)
    x1, h2 = _mix2(x, mod, qk, v, lab, of, gy, w["gng"], w["wout"], w["n2g"], tt)
    a_idx, b_idx, gates = _route(h2, w["wq"], w["subkeys"], tt)
    return _peer(h2, x1, mod, a_idx, b_idx, gates, w["ut"], w["v"], w["nfg"], tt, eb)


def kernel(x_prompt, x_sample, c_prompt, c_sample, norm1_g, norm2_g, w_ada, b_ada, w_in, w_dec_f, b_dec_f,
           w_dec_b, b_dec_b, gla_norm_g, conv_w, conv_norm_g, w_out, peer_wq, peer_subkeys, peer_u, peer_v,
           normf_g):
    w = _prep_weights(norm1_g, norm2_g, w_in, w_dec_f, b_dec_f, w_dec_b, b_dec_b, gla_norm_g, conv_w,
                      conv_norm_g, w_out, peer_wq, peer_subkeys, peer_u, peer_v, normf_g)
    nb_p, nb_s = c_prompt.shape[0], c_sample.shape[0]
    c_all = jnp.concatenate([c_prompt, c_sample, jnp.zeros((SUBLANES - nb_p - nb_s, D_MODEL), F32)], axis=0)
    ada = _ada(c_all, w_ada[0], b_ada[0][None, :])
    mod = jnp.pad(ada.reshape(SUBLANES, N_ADA, D_MODEL), ((0, 0), (0, SUBLANES - N_ADA), (0, 0)))
    y_prompt = _trunk(x_prompt, mod[:nb_p], w, TOKEN_TILE, EXPERT_BLOCK)
    y_sample = _trunk(x_sample, mod[nb_p:nb_p + nb_s], w, TOKEN_TILE, EXPERT_BLOCK)
    return (y_prompt, y_sample)
```

```python
import functools

import jax
import jax.numpy as jnp
from jax import lax
from jax.experimental import pallas as pl
from jax.experimental.pallas import tpu as pltpu

F32 = jnp.float32
BF16 = jnp.bfloat16

D_MODEL = 1024
GLA_HEADS = 4
GLA_DV = 128
GLA_DK = 64
GLA_KW = GLA_HEADS * GLA_DK
GLA_WIDTH = GLA_HEADS * GLA_DV
GLA_RANK = 16
GLA_GATE_NORM = 16.0
GLA_CHUNK = 64
CONV_WIDTH = 512
CONV_GDIM = 64
PEER_HEADS = 8
PEER_NKEYS = 128
PEER_DQH = 128
PEER_TOPK = 16
N_ADA = 6
EPS = 1e-6

LANES = 128
SUBLANES = 8
VMEM_LIMIT = 56 * 1024 * 1024

C_QK, C_V, C_G, C_CB, C_CC, C_CH, C_AL, C_END = 0, 512, 1024, 1536, 2048, 2560, 3072, 3200

TOKEN_TILE = 256
PEER_TOKEN_TILE = 512
EXPERT_BLOCK = 1024
GATE_HALF = PEER_NKEYS // 2
GATE_PITCH = GATE_HALF + SUBLANES
GATE_UNROLL = 8


def _dot(a, b):
    return jnp.dot(a, b, preferred_element_type=F32)


def _dot_nt(a, b):
    return lax.dot_general(a, b, (((1,), (1,)), ((), ())), preferred_element_type=F32)


def _dot_tn(a, b):
    return lax.dot_general(a, b, (((0,), (0,)), ((), ())), preferred_element_type=F32)


def _split2(x):
    hi = x.astype(BF16)
    lo = (x - hi.astype(F32)).astype(BF16)
    return hi, lo


def _split3(x):
    hi = x.astype(BF16)
    r = x - hi.astype(F32)
    mid = r.astype(BF16)
    lo = (r - mid.astype(F32)).astype(BF16)
    return hi, mid, lo


def _dot_f32(a, b):
    a_hi, a_lo = _split2(a)
    b_hi, b_lo = _split2(b)
    return _dot(a_hi, b_hi) + _dot(a_hi, b_lo) + _dot(a_lo, b_hi)


def _bf16_bits(x):
    return lax.bitcast_convert_type(x.astype(BF16).astype(F32), jnp.uint32)


def _rms(x):
    return lax.rsqrt(jnp.mean(x * x, axis=-1, keepdims=True) + EPS)


def _sigmoid(x):
    return 1.0 / (1.0 + jnp.exp(-x))


def _log_sigmoid(x):
    return jnp.minimum(x, 0.0) - jnp.log1p(jnp.exp(-jnp.abs(x)))


def _gelu_tanh(x):
    return 0.5 * x * (1.0 + jnp.tanh(0.7978845608028654 * (x + 0.044715 * (x * x * x))))


def _ada_kernel(c_ref, w_ref, b_ref, o_ref):
    c = c_ref[...]
    o_ref[...] = _dot_f32(c * _sigmoid(c), w_ref[...]) + b_ref[...]


def _ada(c_pad, w_ada, b_ada):
    n_col = N_ADA * D_MODEL
    blk = 1536
    return pl.pallas_call(
        _ada_kernel,
        out_shape=jax.ShapeDtypeStruct((SUBLANES, n_col), F32),
        grid=(n_col // blk,),
        in_specs=[
            pl.BlockSpec((SUBLANES, D_MODEL), lambda j: (0, 0)),
            pl.BlockSpec((D_MODEL, blk), lambda j: (0, j)),
            pl.BlockSpec((1, blk), lambda j: (0, j)),
        ],
        out_specs=pl.BlockSpec((SUBLANES, blk), lambda j: (0, j)),
        compiler_params=pltpu.CompilerParams(dimension_semantics=("arbitrary",), vmem_limit_bytes=VMEM_LIMIT),
        name="ada",
    )(c_pad, w_ada, b_ada)


def _gla_chunk(q_c, k_c, v_c, la_c, tri, st_ref, forward):
    la_hi, la_mid, la_lo = _split3(la_c)
    b = _dot(tri, la_hi) + _dot(tri, la_mid) + _dot(tri, la_lo)
    b_end = b[GLA_CHUNK - 1:GLA_CHUNK, :] if forward else b[0:1, :]
    q_dec = q_c * (jnp.exp(b) * (GLA_DK ** -0.5))
    k_dec = k_c * jnp.exp(-b)
    k_end = k_c * jnp.exp(b_end - b)
    decay = jnp.exp(b_end)
    row = lax.broadcasted_iota(jnp.int32, (GLA_CHUNK, GLA_CHUNK), 0)
    col = lax.broadcasted_iota(jnp.int32, (GLA_CHUNK, GLA_CHUNK), 1)
    causal = (row >= col) if forward else (row <= col)
    lane = lax.broadcasted_iota(jnp.int32, (1, LANES), 1)
    outs = []
    for head in range(GLA_HEADS):
        pair, half = head // 2, head % 2
        sl = slice(pair * LANES, (pair + 1) * LANES)
        in_head = (lane // GLA_DK) == half
        qd = jnp.where(in_head, q_dec[:, sl], 0.0).astype(BF16)
        kd = k_dec[:, sl].astype(BF16)
        ke = k_end[:, sl].astype(BF16)
        v_h = v_c[:, head * GLA_DV:(head + 1) * GLA_DV].astype(BF16)
        scores = jnp.where(causal, _dot_nt(qd, kd), 0.0)
        st = st_ref[head]
        o = _dot(scores.astype(BF16), v_h) + _dot_nt(qd, st.astype(BF16))
        st_ref[head] = st * decay[:, sl] + _dot_tn(v_h, ke)
        outs.append(o)
    return jnp.concatenate(outs, axis=-1)


def _mix1_kernel(x_ref, xp_ref, xn_ref, mod_ref, n1g_ref, win_ref, wdec_ref, bdec_ref, convw_ref, cng_ref,
                 mgrp_ref, qk_ref, v_ref, lab_ref, of_ref, gy_ref, laf_s, st_s, *, tt, nt):
    i = pl.program_id(1)
    mod = mod_ref[0]
    sh1, sc1 = mod[0:1], mod[1:2]
    n1g = n1g_ref[...]

    def norm_mod(x):
        return ((x * _rms(x)) * n1g) * (1.0 + sc1) + sh1

    h = norm_mod(x_ref[0]).astype(BF16)
    qk_ref[0] = _dot(h, win_ref[:, C_QK:C_V])
    v_ref[0] = _dot(h, win_ref[:, C_V:C_G])
    gy_ref[0, :, 0:GLA_WIDTH] = _dot(h, win_ref[:, C_G:C_CB])

    hh = norm_mod(jnp.concatenate([xp_ref[0], xn_ref[0]], axis=0)).astype(BF16)
    z_halo = _dot(hh, win_ref[:, C_CC:C_CH]) * _dot(hh, win_ref[:, C_CH:C_AL])
    z_prev = jnp.where(i > 0, z_halo[SUBLANES - 1:SUBLANES, :], 0.0)
    z_next = jnp.where(i < nt - 1, z_halo[SUBLANES:SUBLANES + 1, :], 0.0)
    z = _dot(h, win_ref[:, C_CC:C_CH]) * _dot(h, win_ref[:, C_CH:C_AL])
    row = lax.broadcasted_iota(jnp.int32, (tt, 1), 0)
    z_m1 = jnp.where(row == 0, z_prev, pltpu.roll(z, 1, axis=0))
    z_p1 = jnp.where(row == tt - 1, z_next, pltpu.roll(z, tt - 1, axis=0))
    cw = convw_ref[...]
    conv = cw[0:1] * z_m1 + cw[1:2] * z + cw[2:3] * z_p1
    yc = _dot(h, win_ref[:, C_CB:C_CC]) * conv
    sq_hi, sq_lo = _split2(yc * yc)
    ss = _dot(sq_hi, mgrp_ref[...]) + _dot(sq_lo, mgrp_ref[...])
    gy_ref[0, :, GLA_WIDTH:] = (yc * lax.rsqrt(ss * (1.0 / CONV_GDIM) + EPS)) * cng_ref[...]

    zd = _dot_f32(_dot(h, win_ref[:, C_AL:C_END]), wdec_ref[...]) + bdec_ref[...]
    la = _log_sigmoid(zd) * (1.0 / GLA_GATE_NORM)
    laf_s[...] = la[:, 0:GLA_KW]
    lab_ref[0] = la[:, GLA_KW:]

    @pl.when(i == 0)
    def _():
        st_s[...] = jnp.zeros_like(st_s)

    r_i = lax.broadcasted_iota(jnp.int32, (GLA_CHUNK, GLA_CHUNK), 0)
    c_i = lax.broadcasted_iota(jnp.int32, (GLA_CHUNK, GLA_CHUNK), 1)
    tri = jnp.where(r_i >= c_i, 1.0, 0.0).astype(BF16)

    def chunk(c, carry):
        r0 = pl.multiple_of(c * GLA_CHUNK, GLA_CHUNK)
        rows = pl.ds(r0, GLA_CHUNK)
        o = _gla_chunk(qk_ref[0, rows, 0:GLA_KW], qk_ref[0, rows, GLA_KW:], v_ref[0, rows, :],
                       laf_s[rows, :], tri, st_s, True)
        of_ref[0, rows, :] = o
        return carry

    lax.fori_loop(0, tt // GLA_CHUNK, chunk, 0)


def _mix1(x, mod, n1g, win, wdec, bdec, convw, cng, mgrp, tt):
    bsz, seq, _ = x.shape
    nt = seq // tt
    hb = tt // SUBLANES
    last_hb = seq // SUBLANES - 1
    const = lambda shape: pl.BlockSpec(shape, lambda b, i: tuple(0 for _ in shape))
    tile = lambda w: pl.BlockSpec((1, tt, w), lambda b, i: (b, i, 0))
    out_w = (2 * GLA_KW, GLA_WIDTH, GLA_KW, GLA_WIDTH, GLA_WIDTH + CONV_WIDTH)
    return pl.pallas_call(
        functools.partial(_mix1_kernel, tt=tt, nt=nt),
        out_shape=tuple(jax.ShapeDtypeStruct((bsz, seq, w), F32) for w in out_w),
        grid=(bsz, nt),
        in_specs=[
            tile(D_MODEL),
            pl.BlockSpec((1, SUBLANES, D_MODEL), lambda b, i: (b, jnp.maximum(i * hb - 1, 0), 0)),
            pl.BlockSpec((1, SUBLANES, D_MODEL), lambda b, i: (b, jnp.minimum((i + 1) * hb, last_hb), 0)),
            pl.BlockSpec((1, SUBLANES, D_MODEL), lambda b, i: (b, 0, 0)),
            const((1, D_MODEL)),
            const((D_MODEL, C_END)),
            const((LANES, 2 * GLA_KW)),
            const((1, 2 * GLA_KW)),
            const((3, CONV_WIDTH)),
            const((1, CONV_WIDTH)),
            const((CONV_WIDTH, CONV_WIDTH)),
        ],
        out_specs=tuple(tile(w) for w in out_w),
        scratch_shapes=[pltpu.VMEM((tt, GLA_KW), F32), pltpu.VMEM((GLA_HEADS, GLA_DV, LANES), F32)],
        compiler_params=pltpu.CompilerParams(dimension_semantics=("arbitrary", "arbitrary"),
                                             vmem_limit_bytes=VMEM_LIMIT),
        name="mix1",
    )(x, x, x, mod, n1g, win, wdec, bdec, convw, cng, mgrp)


def _mix2_kernel(x_ref, mod_ref, qk_ref, v_ref, lab_ref, of_ref, gy_ref, gng_ref, wout_ref, n2g_ref,
                 x1_ref, h2_ref, ob_s, st_s, *, tt):
    i = pl.program_id(1)

    @pl.when(i == 0)
    def _():
        st_s[...] = jnp.zeros_like(st_s)

    r_i = lax.broadcasted_iota(jnp.int32, (GLA_CHUNK, GLA_CHUNK), 0)
    c_i = lax.broadcasted_iota(jnp.int32, (GLA_CHUNK, GLA_CHUNK), 1)
    tri = jnp.where(r_i <= c_i, 1.0, 0.0).astype(BF16)
    n_chunk = tt // GLA_CHUNK

    def chunk(ci, carry):
        r0 = pl.multiple_of((n_chunk - 1 - ci) * GLA_CHUNK, GLA_CHUNK)
        rows = pl.ds(r0, GLA_CHUNK)
        ob_s[rows, :] = _gla_chunk(qk_ref[0, rows, 0:GLA_KW], qk_ref[0, rows, GLA_KW:], v_ref[0, rows, :],
                                   lab_ref[0, rows, :], tri, st_s, False)
        return carry

    lax.fori_loop(0, n_chunk, chunk, 0)

    mod = mod_ref[0]
    gt1, sh2, sc2 = mod[2:3], mod[3:4], mod[4:5]
    o = of_ref[0] + ob_s[...]
    gng = gng_ref[...]
    y = None
    for head in range(GLA_HEADS):
        sl = slice(head * GLA_DV, (head + 1) * GLA_DV)
        oh = o[:, sl]
        g = gy_ref[0, :, sl]
        yh = ((oh * _rms(oh)) * gng) * (g * _sigmoid(g))
        part = _dot(yh.astype(BF16), wout_ref[sl, :])
        y = part if y is None else y + part
    y = y + _dot(gy_ref[0, :, GLA_WIDTH:].astype(BF16), wout_ref[GLA_WIDTH:, :])
    x1 = x_ref[0] + gt1 * y
    x1_ref[0] = x1
    h2_ref[0] = (((x1 * _rms(x1)) * n2g_ref[...]) * (1.0 + sc2) + sh2).astype(BF16)


def _mix2(x, mod, qk, v, lab, of, gy, gng, wout, n2g, tt):
    bsz, seq, _ = x.shape
    nt = seq // tt
    const = lambda shape: pl.BlockSpec(shape, lambda b, i: tuple(0 for _ in shape))
    tile = lambda w: pl.BlockSpec((1, tt, w), lambda b, i: (b, nt - 1 - i, 0))
    return pl.pallas_call(
        functools.partial(_mix2_kernel, tt=tt),
        out_shape=(jax.ShapeDtypeStruct((bsz, seq, D_MODEL), F32), jax.ShapeDtypeStruct((bsz, seq, D_MODEL), BF16)),
        grid=(bsz, nt),
        in_specs=[
            tile(D_MODEL),
            pl.BlockSpec((1, SUBLANES, D_MODEL), lambda b, i: (b, 0, 0)),
            tile(2 * GLA_KW), tile(GLA_WIDTH), tile(GLA_KW), tile(GLA_WIDTH), tile(GLA_WIDTH + CONV_WIDTH),
            const((1, GLA_DV)),
            const((D_MODEL, D_MODEL)),
            const((1, D_MODEL)),
        ],
        out_specs=(tile(D_MODEL), tile(D_MODEL)),
        scratch_shapes=[pltpu.VMEM((tt, GLA_WIDTH), F32), pltpu.VMEM((GLA_HEADS, GLA_DV, LANES), F32)],
        compiler_params=pltpu.CompilerParams(dimension_semantics=("arbitrary", "arbitrary"),
                                             vmem_limit_bytes=VMEM_LIMIT),
        name="mix2",
    )(x, mod, qk, v, lab, of, gy, gng, wout, n2g)


NEG_INF = float("-inf")
N_CAND = 10 * SUBLANES


def _top16_rows(s, sv_ref, si_ref):
    n = s.shape[0]
    rid = lax.broadcasted_iota(jnp.int32, s.shape, 0).astype(F32)
    for k in range(PEER_TOPK):
        m = jnp.max(s, axis=0, keepdims=True)
        idx = jnp.min(jnp.where(s == m, rid, float(n)), axis=0, keepdims=True)
        s = jnp.where(rid == idx, NEG_INF, s)
        sv_ref[k:k + 1, :] = m
        si_ref[k:k + 1, :] = idx


def _route_kernel(h2_ref, wq_ref, sk_ref, a_ref, b_ref, g_ref, sv_s, si_s, pa_s, pb_s, pg_s, *, tt):
    h2 = h2_ref[0]
    sub = lax.broadcasted_iota(jnp.int32, (SUBLANES, 1), 0).astype(F32)
    for head in range(PEER_HEADS):
        for p in range(2):
            hp = head * 2 + p
            q = _dot(h2, wq_ref[:, hp * PEER_DQH:(hp + 1) * PEER_DQH])
            q_hi, q_lo = _split2(q)
            k_hi, k_lo = _split2(sk_ref[hp])
            s = _dot_nt(k_hi, q_hi) + _dot_nt(k_hi, q_lo) + _dot_nt(k_lo, q_hi)
            _top16_rows(s, sv_s.at[p], si_s.at[p])
        sv1, sv2 = sv_s[0], sv_s[1]
        si1, si2 = si_s[0], si_s[1]
        bc = lambda r: jnp.broadcast_to(r, (SUBLANES, tt))
        cand, code, ca, cb = [], [], [], []
        for k1, k2_0 in [(0, 0), (0, 8)] + [(k, 0) for k in range(1, 8)]:
            cand.append(bc(sv1[k1:k1 + 1]) + sv2[k2_0:k2_0 + SUBLANES])
            code.append(sub + float(k1 * PEER_TOPK + k2_0))
            ca.append(bc(si1[k1:k1 + 1]))
            cb.append(si2[k2_0:k2_0 + SUBLANES])
        cand.append(sv1[SUBLANES:] + bc(sv2[0:1]))
        code.append((sub + float(SUBLANES)) * float(PEER_TOPK))
        ca.append(si1[SUBLANES:])
        cb.append(bc(si2[0:1]))
        cand = jnp.concatenate(cand, axis=0)
        code = jnp.broadcast_to(jnp.concatenate(code, axis=0), (N_CAND, tt))
        ca = jnp.concatenate(ca, axis=0)
        cb = jnp.concatenate(cb, axis=0)
        vals = []
        for k in range(PEER_TOPK):
            m = jnp.max(cand, axis=0, keepdims=True)
            sel = jnp.min(jnp.where(cand == m, code, 1e9), axis=0, keepdims=True)
            hit = code == sel
            r = head * PEER_TOPK + k
            pa_s[r:r + 1, :] = jnp.sum(jnp.where(hit, ca, 0.0), axis=0, keepdims=True)
            pb_s[r:r + 1, :] = jnp.sum(jnp.where(hit, cb, 0.0), axis=0, keepdims=True)
            cand = jnp.where(hit, NEG_INF, cand)
            vals.append(m)
        es = [jnp.exp(m - vals[0]) for m in vals]
        tot = es[0]
        for e in es[1:]:
            tot = tot + e
        inv = 1.0 / tot
        for k in range(PEER_TOPK):
            r = head * PEER_TOPK + k
            pg_s[r:r + 1, :] = es[k] * inv
    a_ref[0] = jnp.transpose(pa_s[...])
    b_ref[0] = jnp.transpose(pb_s[...])
    g_ref[0] = jnp.transpose(pg_s[...])


def _route(h2, wq, subkeys, tt):
    bsz, seq, _ = h2.shape
    n_pair = PEER_HEADS * PEER_TOPK
    tile = lambda w: pl.BlockSpec((1, tt, w), lambda b, i: (b, i, 0))
    return pl.pallas_call(
        functools.partial(_route_kernel, tt=tt),
        out_shape=(jax.ShapeDtypeStruct((bsz, seq, n_pair), F32),
                   jax.ShapeDtypeStruct((bsz, seq, n_pair), F32),
                   jax.ShapeDtypeStruct((bsz, seq, n_pair), F32)),
        grid=(bsz, seq // tt),
        in_specs=[
            tile(D_MODEL),
            pl.BlockSpec((D_MODEL, 2 * PEER_HEADS * PEER_DQH), lambda b, i: (0, 0)),
            pl.BlockSpec((2 * PEER_HEADS, PEER_NKEYS, PEER_DQH), lambda b, i: (0, 0, 0)),
        ],
        out_specs=(tile(n_pair), tile(n_pair), tile(n_pair)),
        scratch_shapes=[pltpu.VMEM((2, PEER_TOPK, tt), F32), pltpu.VMEM((2, PEER_TOPK, tt), F32),
                        pltpu.VMEM((n_pair, tt), F32), pltpu.VMEM((n_pair, tt), F32), pltpu.VMEM((n_pair, tt), F32)],
        compiler_params=pltpu.CompilerParams(dimension_semantics=("arbitrary", "arbitrary"),
                                             vmem_limit_bytes=VMEM_LIMIT),
        name="route",
    )(h2, wq, subkeys)


def _peer_kernel(h2_ref, x1_ref, mod_ref, a_ref, b_ref, g_ref, ut_ref, v_ref, nfg_ref, y_ref, gate_s, acc_s,
                 *, tt, eb, ne):
    j = pl.program_id(2)
    n_grp = eb // PEER_NKEYS
    steps_per_half = ne // 2

    @pl.when(j == 0)
    def _():
        acc_s[...] = jnp.zeros_like(acc_s)
        sub = lax.broadcasted_iota(jnp.int32, (PEER_NKEYS, PEER_NKEYS), 0).astype(F32).astype(BF16)
        zero = jnp.zeros((PEER_NKEYS, PEER_NKEYS), BF16)
        one = jnp.ones((PEER_NKEYS, PEER_NKEYS), BF16)

        def onehots(t):
            bc = lambda ref: jnp.broadcast_to(ref[0, pl.ds(t, 1), :], (PEER_NKEYS, PEER_NKEYS)).astype(BF16)
            at = jnp.where(sub == bc(a_ref), one, zero)
            cbt = jnp.where(sub == bc(b_ref), bc(g_ref), zero)
            return at, cbt

        def build(p, carry):
            for u in range(GATE_UNROLL):
                t0 = (p * GATE_UNROLL + u) * 2
                at0, cbt0 = onehots(t0)
                at1, cbt1 = onehots(t0 + 1)
                lhs = jnp.concatenate([at0, at1], axis=1)
                rhs = jnp.concatenate([jnp.concatenate([cbt0, zero], axis=1),
                                       jnp.concatenate([zero, cbt1], axis=1)], axis=0)
                tiles = _dot_nt(lhs, rhs)
                for k in range(2):
                    lo = tiles[0:GATE_HALF, k * PEER_NKEYS:(k + 1) * PEER_NKEYS]
                    hi = tiles[GATE_HALF:, k * PEER_NKEYS:(k + 1) * PEER_NKEYS]
                    r0 = pl.multiple_of((t0 + k) * GATE_PITCH, SUBLANES)
                    gate_s[pl.ds(r0, GATE_HALF), :] = (_bf16_bits(lo) >> 16) | _bf16_bits(hi)
            return carry

        lax.fori_loop(0, tt // (2 * GATE_UNROLL), build, 0)

    def expert_block(half):
        s = _dot(h2_ref[0], ut_ref[...])
        parts = []
        for gi in range(n_grp):
            r = (j - half * steps_per_half) * n_grp + gi
            packed = gate_s[pl.ds(r, tt, stride=GATE_PITCH), :]
            word = (packed & jnp.uint32(0xFFFF0000)) if half else (packed << 16)
            gate = lax.bitcast_convert_type(word, F32)
            parts.append((gate * _gelu_tanh(s[:, gi * PEER_NKEYS:(gi + 1) * PEER_NKEYS])).astype(BF16))
        acc_s[...] += _dot(jnp.concatenate(parts, axis=1), v_ref[...])

    @pl.when(j < steps_per_half)
    def _():
        expert_block(0)

    @pl.when(j >= steps_per_half)
    def _():
        expert_block(1)

    @pl.when(j == ne - 1)
    def _():
        gt2 = mod_ref[0][5:6]
        x2 = x1_ref[0] + gt2 * acc_s[...]
        y_ref[0] = (x2 * _rms(x2)) * nfg_ref[...]


def _peer(h2, x1, mod, a_idx, b_idx, gates, ut, v, nfg, tt, eb):
    bsz, seq, _ = h2.shape
    n_exp = v.shape[0]
    ne = n_exp // eb
    assert ne % 2 == 0 and (GATE_HALF * PEER_NKEYS) % eb == 0 and tt % (2 * GATE_UNROLL) == 0
    n_pair = PEER_HEADS * PEER_TOPK
    tile = lambda w: pl.BlockSpec((1, tt, w), lambda b, i, j: (b, i, 0))
    return pl.pallas_call(
        functools.partial(_peer_kernel, tt=tt, eb=eb, ne=ne),
        out_shape=jax.ShapeDtypeStruct((bsz, seq, D_MODEL), F32),
        grid=(bsz, seq // tt, ne),
        in_specs=[
            tile(D_MODEL), tile(D_MODEL),
            pl.BlockSpec((1, SUBLANES, D_MODEL), lambda b, i, j: (b, 0, 0)),
            tile(n_pair), tile(n_pair), tile(n_pair),
            pl.BlockSpec((D_MODEL, eb), lambda b, i, j: (0, j)),
            pl.BlockSpec((eb, D_MODEL), lambda b, i, j: (j, 0)),
            pl.BlockSpec((1, D_MODEL), lambda b, i, j: (0, 0)),
        ],
        out_specs=tile(D_MODEL),
        scratch_shapes=[pltpu.VMEM((tt * GATE_PITCH, PEER_NKEYS), jnp.uint32), pltpu.VMEM((tt, D_MODEL), F32)],
        compiler_params=pltpu.CompilerParams(dimension_semantics=("arbitrary", "arbitrary", "arbitrary"),
                                             vmem_limit_bytes=VMEM_LIMIT),
        name="peer",
    )(h2, x1, mod, a_idx, b_idx, gates, ut, v, nfg)


def _prep_weights(norm1_g, norm2_g, w_in, w_dec_f, b_dec_f, w_dec_b, b_dec_b, gla_norm_g, conv_w, conv_norm_g,
                  w_out, peer_wq, peer_subkeys, peer_u, peer_v, normf_g):
    w = w_in[0]
    q, k, v, g, alf, alb, cb, cc, ch = jnp.split(w, (256, 512, 1024, 1536, 1552, 1568, 2080, 2592), axis=-1)
    pad = jnp.zeros((D_MODEL, C_END - C_AL - 2 * GLA_RANK), F32)
    win = jnp.concatenate([q, k, v, g, cb, cc, ch, alf, alb, pad], axis=-1).astype(BF16)
    wdec = jnp.zeros((LANES, 2 * GLA_KW), F32)
    wdec = wdec.at[0:GLA_RANK, 0:GLA_KW].set(w_dec_f[0]).at[GLA_RANK:2 * GLA_RANK, GLA_KW:].set(w_dec_b[0])
    bdec = jnp.concatenate([b_dec_f[0], b_dec_b[0]])[None, :]
    grp = jnp.arange(CONV_WIDTH) // CONV_GDIM
    mgrp = (grp[:, None] == grp[None, :]).astype(BF16)
    return dict(
        n1g=norm1_g[0][None, :], n2g=norm2_g[0][None, :], win=win, wdec=wdec, bdec=bdec,
        convw=conv_w[0], cng=conv_norm_g[0][None, :], mgrp=mgrp, gng=gla_norm_g[0][None, :],
        wout=w_out[0].astype(BF16), wq=peer_wq[0].astype(BF16),
        subkeys=peer_subkeys[0].reshape(2 * PEER_HEADS, PEER_NKEYS, PEER_DQH),
        ut=jnp.transpose(peer_u[0]).astype(BF16), v=peer_v[0].astype(BF16), nfg=normf_g[None, :],
    )


def _trunk(x, mod, w, tt, ptt, eb):
    qk, v, lab, of, gy = _mix1(x, mod, w["n1g"], w["win"], w["wdec"], w["bdec"], w["convw"], w["cng"], w["mgrp"], tt)
    x1, h2 = _mix2(x, mod, qk, v, lab, of, gy, w["gng"], w["wout"], w["n2g"], tt)
    a_idx, b_idx, gates = _route(h2, w["wq"], w["subkeys"], tt)
    return _peer(h2, x1, mod, a_idx, b_idx, gates, w["ut"], w["v"], w["nfg"], ptt, eb)


def kernel(x_prompt, x_sample, c_prompt, c_sample, norm1_g, norm2_g, w_ada, b_ada, w_in, w_dec_f, b_dec_f,
           w_dec_b, b_dec_b, gla_norm_g, conv_w, conv_norm_g, w_out, peer_wq, peer_subkeys, peer_u, peer_v,
           normf_g):
    w = _prep_weights(norm1_g, norm2_g, w_in, w_dec_f, b_dec_f, w_dec_b, b_dec_b, gla_norm_g, conv_w,
                      conv_norm_g, w_out, peer_wq, peer_subkeys, peer_u, peer_v, normf_g)
    nb_p, nb_s = c_prompt.shape[0], c_sample.shape[0]
    c_all = jnp.concatenate([c_prompt, c_sample, jnp.zeros((SUBLANES - nb_p - nb_s, D_MODEL), F32)], axis=0)
    ada = _ada(c_all, w_ada[0], b_ada[0][None, :])
    mod = jnp.pad(ada.reshape(SUBLANES, N_ADA, D_MODEL), ((0, 0), (0, SUBLANES - N_ADA), (0, 0)))
    y_prompt = _trunk(x_prompt, mod[:nb_p], w, TOKEN_TILE, PEER_TOKEN_TILE, EXPERT_BLOCK)
    y_sample = _trunk(x_sample, mod[nb_p:nb_p + nb_s], w, TOKEN_TILE, PEER_TOKEN_TILE, EXPERT_BLOCK)
    return (y_prompt, y_sample)
```

```python
import functools

import jax
import jax.numpy as jnp
from jax import lax
from jax.experimental import pallas as pl
from jax.experimental.pallas import tpu as pltpu

F32 = jnp.float32
BF16 = jnp.bfloat16

D_MODEL = 1024
GLA_HEADS = 4
GLA_DV = 128
GLA_DK = 64
GLA_KW = GLA_HEADS * GLA_DK
GLA_WIDTH = GLA_HEADS * GLA_DV
GLA_RANK = 16
GLA_GATE_NORM = 16.0
GLA_CHUNK = 64
CONV_WIDTH = 512
CONV_GDIM = 64
PEER_HEADS = 8
PEER_NKEYS = 128
PEER_DQH = 128
PEER_TOPK = 16
N_ADA = 6
EPS = 1e-6

LANES = 128
SUBLANES = 8
VMEM_LIMIT = 56 * 1024 * 1024

C_QK, C_V, C_G, C_CB, C_CC, C_CH, C_AL, C_END = 0, 512, 1024, 1536, 2048, 2560, 3072, 3200

TOKEN_TILE = 256
ROUTE_TILE = 512
PEER_TOKEN_TILE = 512
EXPERT_BLOCK = 1024
GATE_HALF = PEER_NKEYS // 2
GATE_PITCH = GATE_HALF + SUBLANES
GATE_UNROLL = 8


def _dot(a, b):
    return jnp.dot(a, b, preferred_element_type=F32)


def _dot_nt(a, b):
    return lax.dot_general(a, b, (((1,), (1,)), ((), ())), preferred_element_type=F32)


def _dot_tn(a, b):
    return lax.dot_general(a, b, (((0,), (0,)), ((), ())), preferred_element_type=F32)


def _split2(x):
    hi = x.astype(BF16)
    lo = (x - hi.astype(F32)).astype(BF16)
    return hi, lo


def _split3(x):
    hi = x.astype(BF16)
    r = x - hi.astype(F32)
    mid = r.astype(BF16)
    lo = (r - mid.astype(F32)).astype(BF16)
    return hi, mid, lo


def _dot_f32(a, b):
    a_hi, a_lo = _split2(a)
    b_hi, b_lo = _split2(b)
    return _dot(a_hi, b_hi) + _dot(a_hi, b_lo) + _dot(a_lo, b_hi)


def _bf16_bits(x):
    return lax.bitcast_convert_type(x.astype(BF16).astype(F32), jnp.uint32)


def _rms(x):
    return lax.rsqrt(jnp.mean(x * x, axis=-1, keepdims=True) + EPS)


def _sigmoid(x):
    return 1.0 / (1.0 + jnp.exp(-x))


def _log_sigmoid(x):
    return jnp.minimum(x, 0.0) - jnp.log1p(jnp.exp(-jnp.abs(x)))


def _gelu_tanh(x):
    return 0.5 * x * (1.0 + jnp.tanh(0.7978845608028654 * (x + 0.044715 * (x * x * x))))


def _ada_kernel(c_ref, w_ref, b_ref, o_ref):
    c = c_ref[...]
    o_ref[...] = _dot_f32(c * _sigmoid(c), w_ref[...]) + b_ref[...]


def _ada(c_pad, w_ada, b_ada):
    n_col = N_ADA * D_MODEL
    blk = 1536
    return pl.pallas_call(
        _ada_kernel,
        out_shape=jax.ShapeDtypeStruct((SUBLANES, n_col), F32),
        grid=(n_col // blk,),
        in_specs=[
            pl.BlockSpec((SUBLANES, D_MODEL), lambda j: (0, 0)),
            pl.BlockSpec((D_MODEL, blk), lambda j: (0, j)),
            pl.BlockSpec((1, blk), lambda j: (0, j)),
        ],
        out_specs=pl.BlockSpec((SUBLANES, blk), lambda j: (0, j)),
        compiler_params=pltpu.CompilerParams(dimension_semantics=("arbitrary",), vmem_limit_bytes=VMEM_LIMIT),
        name="ada",
    )(c_pad, w_ada, b_ada)


def _gla_chunk(q_c, k_c, v_c, la_c, tri, st_ref, forward):
    la_hi, la_mid, la_lo = _split3(la_c)
    b = _dot(tri, la_hi) + _dot(tri, la_mid) + _dot(tri, la_lo)
    b_end = b[GLA_CHUNK - 1:GLA_CHUNK, :] if forward else b[0:1, :]
    q_dec = q_c * (jnp.exp(b) * (GLA_DK ** -0.5))
    k_dec = k_c * jnp.exp(-b)
    k_end = k_c * jnp.exp(b_end - b)
    decay = jnp.exp(b_end)
    row = lax.broadcasted_iota(jnp.int32, (GLA_CHUNK, GLA_CHUNK), 0)
    col = lax.broadcasted_iota(jnp.int32, (GLA_CHUNK, GLA_CHUNK), 1)
    causal = (row >= col) if forward else (row <= col)
    lane = lax.broadcasted_iota(jnp.int32, (1, LANES), 1)
    outs = []
    for head in range(GLA_HEADS):
        pair, half = head // 2, head % 2
        sl = slice(pair * LANES, (pair + 1) * LANES)
        in_head = (lane // GLA_DK) == half
        qd = jnp.where(in_head, q_dec[:, sl], 0.0).astype(BF16)
        kd = k_dec[:, sl].astype(BF16)
        ke = k_end[:, sl].astype(BF16)
        v_h = v_c[:, head * GLA_DV:(head + 1) * GLA_DV].astype(BF16)
        scores = jnp.where(causal, _dot_nt(qd, kd), 0.0)
        st = st_ref[head]
        o = _dot(scores.astype(BF16), v_h) + _dot_nt(qd, st.astype(BF16))
        st_ref[head] = st * decay[:, sl] + _dot_tn(v_h, ke)
        outs.append(o)
    return jnp.concatenate(outs, axis=-1)


def _mix1_kernel(x_ref, xp_ref, xn_ref, mod_ref, n1g_ref, win_ref, wdec_ref, bdec_ref, convw_ref, cng_ref,
                 mgrp_ref, qk_ref, v_ref, lab_ref, of_ref, gy_ref, laf_s, st_s, *, tt, nt):
    i = pl.program_id(1)
    mod = mod_ref[0]
    sh1, sc1 = mod[0:1], mod[1:2]
    n1g = n1g_ref[...]

    def norm_mod(x):
        return ((x * _rms(x)) * n1g) * (1.0 + sc1) + sh1

    h = norm_mod(x_ref[0]).astype(BF16)
    qk_ref[0] = _dot(h, win_ref[:, C_QK:C_V])
    v_ref[0] = _dot(h, win_ref[:, C_V:C_G])
    gy_ref[0, :, 0:GLA_WIDTH] = _dot(h, win_ref[:, C_G:C_CB])

    hh = norm_mod(jnp.concatenate([xp_ref[0], xn_ref[0]], axis=0)).astype(BF16)
    z_halo = _dot(hh, win_ref[:, C_CC:C_CH]) * _dot(hh, win_ref[:, C_CH:C_AL])
    z_prev = jnp.where(i > 0, z_halo[SUBLANES - 1:SUBLANES, :], 0.0)
    z_next = jnp.where(i < nt - 1, z_halo[SUBLANES:SUBLANES + 1, :], 0.0)
    z = _dot(h, win_ref[:, C_CC:C_CH]) * _dot(h, win_ref[:, C_CH:C_AL])
    row = lax.broadcasted_iota(jnp.int32, (tt, 1), 0)
    z_m1 = jnp.where(row == 0, z_prev, pltpu.roll(z, 1, axis=0))
    z_p1 = jnp.where(row == tt - 1, z_next, pltpu.roll(z, tt - 1, axis=0))
    cw = convw_ref[...]
    conv = cw[0:1] * z_m1 + cw[1:2] * z + cw[2:3] * z_p1
    yc = _dot(h, win_ref[:, C_CB:C_CC]) * conv
    sq_hi, sq_lo = _split2(yc * yc)
    ss = _dot(sq_hi, mgrp_ref[...]) + _dot(sq_lo, mgrp_ref[...])
    gy_ref[0, :, GLA_WIDTH:] = (yc * lax.rsqrt(ss * (1.0 / CONV_GDIM) + EPS)) * cng_ref[...]

    zd = _dot_f32(_dot(h, win_ref[:, C_AL:C_END]), wdec_ref[...]) + bdec_ref[...]
    la = _log_sigmoid(zd) * (1.0 / GLA_GATE_NORM)
    laf_s[...] = la[:, 0:GLA_KW]
    lab_ref[0] = la[:, GLA_KW:]

    @pl.when(i == 0)
    def _():
        st_s[...] = jnp.zeros_like(st_s)

    r_i = lax.broadcasted_iota(jnp.int32, (GLA_CHUNK, GLA_CHUNK), 0)
    c_i = lax.broadcasted_iota(jnp.int32, (GLA_CHUNK, GLA_CHUNK), 1)
    tri = jnp.where(r_i >= c_i, 1.0, 0.0).astype(BF16)

    def chunk(c, carry):
        r0 = pl.multiple_of(c * GLA_CHUNK, GLA_CHUNK)
        rows = pl.ds(r0, GLA_CHUNK)
        o = _gla_chunk(qk_ref[0, rows, 0:GLA_KW], qk_ref[0, rows, GLA_KW:], v_ref[0, rows, :],
                       laf_s[rows, :], tri, st_s, True)
        of_ref[0, rows, :] = o
        return carry

    lax.fori_loop(0, tt // GLA_CHUNK, chunk, 0)


def _mix1(x, mod, n1g, win, wdec, bdec, convw, cng, mgrp, tt):
    bsz, seq, _ = x.shape
    nt = seq // tt
    hb = tt // SUBLANES
    last_hb = seq // SUBLANES - 1
    const = lambda shape: pl.BlockSpec(shape, lambda b, i: tuple(0 for _ in shape))
    tile = lambda w: pl.BlockSpec((1, tt, w), lambda b, i: (b, i, 0))
    out_w = (2 * GLA_KW, GLA_WIDTH, GLA_KW, GLA_WIDTH, GLA_WIDTH + CONV_WIDTH)
    return pl.pallas_call(
        functools.partial(_mix1_kernel, tt=tt, nt=nt),
        out_shape=tuple(jax.ShapeDtypeStruct((bsz, seq, w), F32) for w in out_w),
        grid=(bsz, nt),
        in_specs=[
            tile(D_MODEL),
            pl.BlockSpec((1, SUBLANES, D_MODEL), lambda b, i: (b, jnp.maximum(i * hb - 1, 0), 0)),
            pl.BlockSpec((1, SUBLANES, D_MODEL), lambda b, i: (b, jnp.minimum((i + 1) * hb, last_hb), 0)),
            pl.BlockSpec((1, SUBLANES, D_MODEL), lambda b, i: (b, 0, 0)),
            const((1, D_MODEL)),
            const((D_MODEL, C_END)),
            const((LANES, 2 * GLA_KW)),
            const((1, 2 * GLA_KW)),
            const((3, CONV_WIDTH)),
            const((1, CONV_WIDTH)),
            const((CONV_WIDTH, CONV_WIDTH)),
        ],
        out_specs=tuple(tile(w) for w in out_w),
        scratch_shapes=[pltpu.VMEM((tt, GLA_KW), F32), pltpu.VMEM((GLA_HEADS, GLA_DV, LANES), F32)],
        compiler_params=pltpu.CompilerParams(dimension_semantics=("arbitrary", "arbitrary"),
                                             vmem_limit_bytes=VMEM_LIMIT),
        name="mix1",
    )(x, x, x, mod, n1g, win, wdec, bdec, convw, cng, mgrp)


def _mix2_kernel(x_ref, mod_ref, qk_ref, v_ref, lab_ref, of_ref, gy_ref, gng_ref, wout_ref, n2g_ref,
                 x1_ref, h2_ref, ob_s, st_s, *, tt):
    i = pl.program_id(1)

    @pl.when(i == 0)
    def _():
        st_s[...] = jnp.zeros_like(st_s)

    r_i = lax.broadcasted_iota(jnp.int32, (GLA_CHUNK, GLA_CHUNK), 0)
    c_i = lax.broadcasted_iota(jnp.int32, (GLA_CHUNK, GLA_CHUNK), 1)
    tri = jnp.where(r_i <= c_i, 1.0, 0.0).astype(BF16)
    n_chunk = tt // GLA_CHUNK

    def chunk(ci, carry):
        r0 = pl.multiple_of((n_chunk - 1 - ci) * GLA_CHUNK, GLA_CHUNK)
        rows = pl.ds(r0, GLA_CHUNK)
        ob_s[rows, :] = _gla_chunk(qk_ref[0, rows, 0:GLA_KW], qk_ref[0, rows, GLA_KW:], v_ref[0, rows, :],
                                   lab_ref[0, rows, :], tri, st_s, False)
        return carry

    lax.fori_loop(0, n_chunk, chunk, 0)

    mod = mod_ref[0]
    gt1, sh2, sc2 = mod[2:3], mod[3:4], mod[4:5]
    o = of_ref[0] + ob_s[...]
    gng = gng_ref[...]
    y = None
    for head in range(GLA_HEADS):
        sl = slice(head * GLA_DV, (head + 1) * GLA_DV)
        oh = o[:, sl]
        g = gy_ref[0, :, sl]
        yh = ((oh * _rms(oh)) * gng) * (g * _sigmoid(g))
        part = _dot(yh.astype(BF16), wout_ref[sl, :])
        y = part if y is None else y + part
    y = y + _dot(gy_ref[0, :, GLA_WIDTH:].astype(BF16), wout_ref[GLA_WIDTH:, :])
    x1 = x_ref[0] + gt1 * y
    x1_ref[0] = x1
    h2_ref[0] = (((x1 * _rms(x1)) * n2g_ref[...]) * (1.0 + sc2) + sh2).astype(BF16)


def _mix2(x, mod, qk, v, lab, of, gy, gng, wout, n2g, tt):
    bsz, seq, _ = x.shape
    nt = seq // tt
    const = lambda shape: pl.BlockSpec(shape, lambda b, i: tuple(0 for _ in shape))
    tile = lambda w: pl.BlockSpec((1, tt, w), lambda b, i: (b, nt - 1 - i, 0))
    return pl.pallas_call(
        functools.partial(_mix2_kernel, tt=tt),
        out_shape=(jax.ShapeDtypeStruct((bsz, seq, D_MODEL), F32), jax.ShapeDtypeStruct((bsz, seq, D_MODEL), BF16)),
        grid=(bsz, nt),
        in_specs=[
            tile(D_MODEL),
            pl.BlockSpec((1, SUBLANES, D_MODEL), lambda b, i: (b, 0, 0)),
            tile(2 * GLA_KW), tile(GLA_WIDTH), tile(GLA_KW), tile(GLA_WIDTH), tile(GLA_WIDTH + CONV_WIDTH),
            const((1, GLA_DV)),
            const((D_MODEL, D_MODEL)),
            const((1, D_MODEL)),
        ],
        out_specs=(tile(D_MODEL), tile(D_MODEL)),
        scratch_shapes=[pltpu.VMEM((tt, GLA_WIDTH), F32), pltpu.VMEM((GLA_HEADS, GLA_DV, LANES), F32)],
        compiler_params=pltpu.CompilerParams(dimension_semantics=("arbitrary", "arbitrary"),
                                             vmem_limit_bytes=VMEM_LIMIT),
        name="mix2",
    )(x, mod, qk, v, lab, of, gy, gng, wout, n2g)


NEG_INF = float("-inf")
KEY_ROWS = 2 * PEER_NKEYS * PEER_HEADS
VREG = (SUBLANES, LANES)


def _keyproj_kernel(wq_ref, sk_ref, o_ref):
    w_hi, w_lo = _split2(wq_ref[...])
    k_hi, k_lo = _split2(sk_ref[0])
    o_ref[0] = _dot_nt(k_hi, w_hi) + _dot_nt(k_hi, w_lo) + _dot_nt(k_lo, w_hi)


def _keyproj(wq, subkeys):
    n_hp = 2 * PEER_HEADS
    return pl.pallas_call(
        _keyproj_kernel,
        out_shape=jax.ShapeDtypeStruct((n_hp, PEER_NKEYS, D_MODEL), F32),
        grid=(n_hp,),
        in_specs=[pl.BlockSpec((D_MODEL, PEER_DQH), lambda j: (0, j)),
                  pl.BlockSpec((1, PEER_NKEYS, PEER_DQH), lambda j: (j, 0, 0))],
        out_specs=pl.BlockSpec((1, PEER_NKEYS, D_MODEL), lambda j: (j, 0, 0)),
        compiler_params=pltpu.CompilerParams(dimension_semantics=("arbitrary",), vmem_limit_bytes=VMEM_LIMIT),
        name="keyproj",
    )(wq, subkeys)


def _sort16_network():
    pairs = []

    def merge(lo, hi, r):
        step = r * 2
        if step < hi - lo:
            merge(lo, hi, step)
            merge(lo + r, hi, step)
            pairs.extend((i, i + r) for i in range(lo + r, hi - r, step))
        else:
            pairs.append((lo, lo + r))

    def sort(lo, hi):
        if hi - lo >= 1:
            mid = lo + (hi - lo) // 2
            sort(lo, mid)
            sort(mid + 1, hi)
            merge(lo, hi, 1)

    sort(0, PEER_TOPK - 1)
    return tuple(pairs)


SORT16 = _sort16_network()
CAND_LISTS = ([[(0, k2) for k2 in range(PEER_TOPK)]]
              + [[(k1, k2) for k2 in range(PEER_TOPK // (k1 + 1))] for k1 in range(1, 8)]
              + [[(k1, 0) for k1 in range(8, PEER_TOPK)]])
CANDS = [c for lst in CAND_LISTS for c in lst]


def _tree(op, xs):
    xs = list(xs)
    while len(xs) > 1:
        xs = [op(xs[i], xs[i + 1]) for i in range(0, len(xs) - 1, 2)] + ([xs[-1]] if len(xs) % 2 else [])
    return xs[0]


def _ce(v, x, i, j):
    c = v[j] > v[i]
    v[i], v[j] = jnp.maximum(v[i], v[j]), jnp.minimum(v[i], v[j])
    x[i], x[j] = jnp.where(c, x[j], x[i]), jnp.where(c, x[i], x[j])


def _merge_top16(va, xa, vb, xb, sort_result):
    v, x = list(va), list(xa)
    for j in range(PEER_TOPK - len(vb), PEER_TOPK):
        b = PEER_TOPK - 1 - j
        c = vb[b] > va[j]
        v[j] = jnp.maximum(va[j], vb[b])
        x[j] = jnp.where(c, xb[b], xa[j])
    if sort_result:
        for d in (8, 4, 2, 1):
            for i in range(PEER_TOPK):
                if not i & d:
                    _ce(v, x, i, i + d)
    return v, x


def _any(flag):
    return jnp.max(flag) > 0.0


def _top16_keys(key, work_v, work_x, sv_ref, si_ref):
    for g in range(PEER_NKEYS // PEER_TOPK):
        v = [key(PEER_TOPK * g + i) for i in range(PEER_TOPK)]
        x = [float(PEER_TOPK * g + i) for i in range(PEER_TOPK)]
        for i, j in SORT16:
            _ce(v, x, i, j)
        for i in range(PEER_TOPK):
            work_v[PEER_TOPK * g + i] = v[i]
            work_x[PEER_TOPK * g + i] = x[i]
    for span in (1, 2, 4):
        for g in range(0, PEER_NKEYS // PEER_TOPK, 2 * span):
            a, b = PEER_TOPK * g, PEER_TOPK * (g + span)
            v, x = _merge_top16([work_v[a + i] for i in range(PEER_TOPK)], [work_x[a + i] for i in range(PEER_TOPK)],
                                [work_v[b + i] for i in range(PEER_TOPK)], [work_x[b + i] for i in range(PEER_TOPK)],
                                True)
            for i in range(PEER_TOPK):
                if span == 4:
                    sv_ref[i], si_ref[i] = v[i], x[i]
                else:
                    work_v[a + i], work_x[a + i] = v[i], x[i]
    repeat = _tree(jnp.maximum, [jnp.where(v[r] == v[r + 1], 1.0, 0.0) for r in range(PEER_TOPK - 1)])
    n_ge = _tree(jnp.add, [jnp.where(key(k) >= v[PEER_TOPK - 1], 1.0, 0.0) for k in range(PEER_NKEYS)])

    @pl.when(_any(jnp.maximum(repeat, jnp.where(n_ge > float(PEER_TOPK), 1.0, 0.0))))
    def _():
        for k in range(PEER_NKEYS):
            work_v[k] = key(k)

        def extract(r, carry):
            vals = [work_v[k] for k in range(PEER_NKEYS)]
            m = _tree(jnp.maximum, vals)
            idx = _tree(jnp.minimum, [jnp.where(vals[k] == m, float(k), float(PEER_NKEYS)) for k in range(PEER_NKEYS)])
            for k in range(PEER_NKEYS):
                work_v[k] = jnp.where(idx == float(k), NEG_INF, vals[k])
            sv_ref[r] = m
            si_ref[r] = idx
            return carry

        lax.fori_loop(0, PEER_TOPK, extract, 0)


def _top16_sums(sv_s, si_s, work_v, work_x, rv_s, re_s):
    sv0 = [sv_s[0, r] for r in range(PEER_TOPK)]
    sv1 = [sv_s[1, r] for r in range(PEER_TOPK)]
    e_hi = [si_s[0, r] * float(PEER_NKEYS) for r in range(PEER_TOPK)]
    si1 = [si_s[1, r] for r in range(PEER_TOPK)]
    val = {c: sv0[c[0]] + sv1[c[1]] for c in CANDS}
    eid = {c: e_hi[c[0]] + si1[c[1]] for c in CANDS}
    v, x = [val[c] for c in CAND_LISTS[0]], [eid[c] for c in CAND_LISTS[0]]
    for n, lst in enumerate(CAND_LISTS[1:]):
        v, x = _merge_top16(v, x, [val[c] for c in lst], [eid[c] for c in lst], n < len(CAND_LISTS) - 2)
    for r in range(PEER_TOPK):
        rv_s[r], re_s[r] = v[r], x[r]
    low = _tree(jnp.minimum, v)
    n_ge = _tree(jnp.add, [jnp.where(val[c] >= low, 1.0, 0.0) for c in CANDS])

    @pl.when(_any(jnp.where(n_ge > float(PEER_TOPK), 1.0, 0.0)))
    def _():
        for n, c in enumerate(CANDS):
            work_v[n], work_x[n] = val[c], eid[c]

        def extract(r, carry):
            vals = [work_v[n] for n in range(len(CANDS))]
            m = _tree(jnp.maximum, vals)
            code = [float(c[0] * PEER_TOPK + c[1]) for c in CANDS]
            sel = _tree(jnp.minimum, [jnp.where(vals[n] == m, code[n], 1e9) for n in range(len(CANDS))])
            hit = [sel == code[n] for n in range(len(CANDS))]
            rv_s[r] = m
            re_s[r] = _tree(jnp.add, [jnp.where(hit[n], work_x[n], 0.0) for n in range(len(CANDS))])
            for n in range(len(CANDS)):
                work_v[n] = jnp.where(hit[n], NEG_INF, vals[n])
            return carry

        lax.fori_loop(0, PEER_TOPK, extract, 0)


def _route_kernel(h2_ref, wpt_ref, a_ref, b_ref, g_ref, sc_s, work_v, work_x, sv_s, si_s, rv_s, re_s,
                  pa_s, pb_s, pg_s, *, tt):
    sc_s[...] = _dot_nt(wpt_ref[...], h2_ref[0])

    def lane_tile(c, carry):
        lanes = pl.ds(pl.multiple_of(c * LANES, LANES), LANES)
        for p in range(2):
            def key(k, p=p):
                r0 = (p * PEER_NKEYS + k) * SUBLANES
                return sc_s[r0:r0 + SUBLANES, lanes]
            _top16_keys(key, work_v, work_x, sv_s.at[p], si_s.at[p])
        _top16_sums(sv_s, si_s, work_v, work_x, rv_s, re_s)
        v = [rv_s[r] for r in range(PEER_TOPK)]
        m = _tree(jnp.maximum, v)
        ex = [jnp.exp(vr - m) for vr in v]
        inv = 1.0 / _tree(jnp.add, ex)
        for r in range(PEER_TOPK):
            e = re_s[r]
            i1 = jnp.floor(e * (1.0 / PEER_NKEYS))
            rows = slice(r * SUBLANES, (r + 1) * SUBLANES)
            pa_s[rows, :] = i1
            pb_s[rows, :] = e - i1 * float(PEER_NKEYS)
            pg_s[rows, :] = ex[r] * inv
        toks = pl.ds(pl.multiple_of(c * LANES, LANES), LANES)
        a_ref[0, toks, :] = jnp.transpose(pa_s[...])
        b_ref[0, toks, :] = jnp.transpose(pb_s[...])
        g_ref[0, toks, :] = jnp.transpose(pg_s[...])
        return carry

    lax.fori_loop(0, tt // LANES, lane_tile, 0)


def _route(h2, wpt, tt):
    bsz, seq, _ = h2.shape
    n_pair = PEER_HEADS * PEER_TOPK
    tile = lambda w: pl.BlockSpec((1, tt, w), lambda b, i: (b, i, 0))
    vregs = lambda *lead: pltpu.VMEM(lead + VREG, F32)
    return pl.pallas_call(
        functools.partial(_route_kernel, tt=tt),
        out_shape=tuple(jax.ShapeDtypeStruct((bsz, seq, n_pair), F32) for _ in range(3)),
        grid=(bsz, seq // tt),
        in_specs=[tile(D_MODEL), pl.BlockSpec((KEY_ROWS, D_MODEL), lambda b, i: (0, 0))],
        out_specs=(tile(n_pair), tile(n_pair), tile(n_pair)),
        scratch_shapes=[pltpu.VMEM((KEY_ROWS, tt), F32), vregs(PEER_NKEYS), vregs(PEER_NKEYS),
                        vregs(2, PEER_TOPK), vregs(2, PEER_TOPK), vregs(PEER_TOPK), vregs(PEER_TOPK),
                        pltpu.VMEM((n_pair, LANES), F32), pltpu.VMEM((n_pair, LANES), F32),
                        pltpu.VMEM((n_pair, LANES), F32)],
        compiler_params=pltpu.CompilerParams(dimension_semantics=("arbitrary", "arbitrary"),
                                             vmem_limit_bytes=VMEM_LIMIT),
        name="route",
    )(h2, wpt)


def _peer_kernel(h2_ref, x1_ref, mod_ref, a_ref, b_ref, g_ref, ut_ref, v_ref, nfg_ref, y_ref, gate_s, acc_s,
                 *, tt, eb, ne):
    j = pl.program_id(2)
    n_grp = eb // PEER_NKEYS
    steps_per_half = ne // 2

    @pl.when(j == 0)
    def _():
        acc_s[...] = jnp.zeros_like(acc_s)
        sub = lax.broadcasted_iota(jnp.int32, (PEER_NKEYS, PEER_NKEYS), 0).astype(F32).astype(BF16)
        zero = jnp.zeros((PEER_NKEYS, PEER_NKEYS), BF16)
        one = jnp.ones((PEER_NKEYS, PEER_NKEYS), BF16)

        def onehots(t):
            bc = lambda ref: jnp.broadcast_to(ref[0, pl.ds(t, 1), :], (PEER_NKEYS, PEER_NKEYS)).astype(BF16)
            at = jnp.where(sub == bc(a_ref), one, zero)
            cbt = jnp.where(sub == bc(b_ref), bc(g_ref), zero)
            return at, cbt

        def build(p, carry):
            for u in range(GATE_UNROLL):
                t0 = (p * GATE_UNROLL + u) * 2
                at0, cbt0 = onehots(t0)
                at1, cbt1 = onehots(t0 + 1)
                lhs = jnp.concatenate([at0, at1], axis=1)
                rhs = jnp.concatenate([jnp.concatenate([cbt0, zero], axis=1),
                                       jnp.concatenate([zero, cbt1], axis=1)], axis=0)
                tiles = _dot_nt(lhs, rhs)
                for k in range(2):
                    lo = tiles[0:GATE_HALF, k * PEER_NKEYS:(k + 1) * PEER_NKEYS]
                    hi = tiles[GATE_HALF:, k * PEER_NKEYS:(k + 1) * PEER_NKEYS]
                    r0 = pl.multiple_of((t0 + k) * GATE_PITCH, SUBLANES)
                    gate_s[pl.ds(r0, GATE_HALF), :] = (_bf16_bits(lo) >> 16) | _bf16_bits(hi)
            return carry

        lax.fori_loop(0, tt // (2 * GATE_UNROLL), build, 0)

    def expert_block(half):
        s = _dot(h2_ref[0], ut_ref[...])
        parts = []
        for gi in range(n_grp):
            r = (j - half * steps_per_half) * n_grp + gi
            packed = gate_s[pl.ds(r, tt, stride=GATE_PITCH), :]
            word = (packed & jnp.uint32(0xFFFF0000)) if half else (packed << 16)
            gate = lax.bitcast_convert_type(word, F32)
            parts.append((gate * _gelu_tanh(s[:, gi * PEER_NKEYS:(gi + 1) * PEER_NKEYS])).astype(BF16))
        acc_s[...] += _dot(jnp.concatenate(parts, axis=1), v_ref[...])

    @pl.when(j < steps_per_half)
    def _():
        expert_block(0)

    @pl.when(j >= steps_per_half)
    def _():
        expert_block(1)

    @pl.when(j == ne - 1)
    def _():
        gt2 = mod_ref[0][5:6]
        x2 = x1_ref[0] + gt2 * acc_s[...]
        y_ref[0] = (x2 * _rms(x2)) * nfg_ref[...]


def _peer(h2, x1, mod, a_idx, b_idx, gates, ut, v, nfg, tt, eb):
    bsz, seq, _ = h2.shape
    n_exp = v.shape[0]
    ne = n_exp // eb
    assert ne % 2 == 0 and (GATE_HALF * PEER_NKEYS) % eb == 0 and tt % (2 * GATE_UNROLL) == 0
    n_pair = PEER_HEADS * PEER_TOPK
    tile = lambda w: pl.BlockSpec((1, tt, w), lambda b, i, j: (b, i, 0))
    return pl.pallas_call(
        functools.partial(_peer_kernel, tt=tt, eb=eb, ne=ne),
        out_shape=jax.ShapeDtypeStruct((bsz, seq, D_MODEL), F32),
        grid=(bsz, seq // tt, ne),
        in_specs=[
            tile(D_MODEL), tile(D_MODEL),
            pl.BlockSpec((1, SUBLANES, D_MODEL), lambda b, i, j: (b, 0, 0)),
            tile(n_pair), tile(n_pair), tile(n_pair),
            pl.BlockSpec((D_MODEL, eb), lambda b, i, j: (0, j)),
            pl.BlockSpec((eb, D_MODEL), lambda b, i, j: (j, 0)),
            pl.BlockSpec((1, D_MODEL), lambda b, i, j: (0, 0)),
        ],
        out_specs=tile(D_MODEL),
        scratch_shapes=[pltpu.VMEM((tt * GATE_PITCH, PEER_NKEYS), jnp.uint32), pltpu.VMEM((tt, D_MODEL), F32)],
        compiler_params=pltpu.CompilerParams(dimension_semantics=("arbitrary", "arbitrary", "arbitrary"),
                                             vmem_limit_bytes=VMEM_LIMIT),
        name="peer",
    )(h2, x1, mod, a_idx, b_idx, gates, ut, v, nfg)


def _prep_weights(norm1_g, norm2_g, w_in, w_dec_f, b_dec_f, w_dec_b, b_dec_b, gla_norm_g, conv_w, conv_norm_g,
                  w_out, peer_wq, peer_subkeys, peer_u, peer_v, normf_g):
    w = w_in[0]
    q, k, v, g, alf, alb, cb, cc, ch = jnp.split(w, (256, 512, 1024, 1536, 1552, 1568, 2080, 2592), axis=-1)
    pad = jnp.zeros((D_MODEL, C_END - C_AL - 2 * GLA_RANK), F32)
    win = jnp.concatenate([q, k, v, g, cb, cc, ch, alf, alb, pad], axis=-1).astype(BF16)
    wdec = jnp.zeros((LANES, 2 * GLA_KW), F32)
    wdec = wdec.at[0:GLA_RANK, 0:GLA_KW].set(w_dec_f[0]).at[GLA_RANK:2 * GLA_RANK, GLA_KW:].set(w_dec_b[0])
    bdec = jnp.concatenate([b_dec_f[0], b_dec_b[0]])[None, :]
    grp = jnp.arange(CONV_WIDTH) // CONV_GDIM
    mgrp = (grp[:, None] == grp[None, :]).astype(BF16)
    wp = _keyproj(peer_wq[0], peer_subkeys[0].reshape(2 * PEER_HEADS, PEER_NKEYS, PEER_DQH))
    wpt = wp.reshape(PEER_HEADS, 2, PEER_NKEYS, D_MODEL).transpose(1, 2, 0, 3).reshape(KEY_ROWS, D_MODEL)
    return dict(
        n1g=norm1_g[0][None, :], n2g=norm2_g[0][None, :], win=win, wdec=wdec, bdec=bdec,
        convw=conv_w[0], cng=conv_norm_g[0][None, :], mgrp=mgrp, gng=gla_norm_g[0][None, :],
        wout=w_out[0].astype(BF16), wpt=wpt.astype(BF16),
        ut=jnp.transpose(peer_u[0]).astype(BF16), v=peer_v[0].astype(BF16), nfg=normf_g[None, :],
    )


def _trunk(x, mod, w, tt, rtt, ptt, eb):
    qk, v, lab, of, gy = _mix1(x, mod, w["n1g"], w["win"], w["wdec"], w["bdec"], w["convw"], w["cng"], w["mgrp"], tt)
    x1, h2 = _mix2(x, mod, qk, v, lab, of, gy, w["gng"], w["wout"], w["n2g"], tt)
    a_idx, b_idx, gates = _route(h2, w["wpt"], rtt)
    return _peer(h2, x1, mod, a_idx, b_idx, gates, w["ut"], w["v"], w["nfg"], ptt, eb)


def kernel(x_prompt, x_sample, c_prompt, c_sample, norm1_g, norm2_g, w_ada, b_ada, w_in, w_dec_f, b_dec_f,
           w_dec_b, b_dec_b, gla_norm_g, conv_w, conv_norm_g, w_out, peer_wq, peer_subkeys, peer_u, peer_v,
           normf_g):
    w = _prep_weights(norm1_g, norm2_g, w_in, w_dec_f, b_dec_f, w_dec_b, b_dec_b, gla_norm_g, conv_w,
                      conv_norm_g, w_out, peer_wq, peer_subkeys, peer_u, peer_v, normf_g)
    nb_p, nb_s = c_prompt.shape[0], c_sample.shape[0]
    c_all = jnp.concatenate([c_prompt, c_sample, jnp.zeros((SUBLANES - nb_p - nb_s, D_MODEL), F32)], axis=0)
    ada = _ada(c_all, w_ada[0], b_ada[0][None, :])
    mod = jnp.pad(ada.reshape(SUBLANES, N_ADA, D_MODEL), ((0, 0), (0, SUBLANES - N_ADA), (0, 0)))
    y_prompt = _trunk(x_prompt, mod[:nb_p], w, TOKEN_TILE, ROUTE_TILE, PEER_TOKEN_TILE, EXPERT_BLOCK)
    y_sample = _trunk(x_sample, mod[nb_p:nb_p + nb_s], w, TOKEN_TILE, ROUTE_TILE, PEER_TOKEN_TILE, EXPERT_BLOCK)
    return (y_prompt, y_sample)
```

```python
import functools

import jax
import jax.numpy as jnp
from jax import lax
from jax.experimental import pallas as pl
from jax.experimental.pallas import tpu as pltpu

F32 = jnp.float32
BF16 = jnp.bfloat16

D_MODEL = 1024
GLA_HEADS = 4
GLA_DV = 128
GLA_DK = 64
GLA_KW = GLA_HEADS * GLA_DK
GLA_WIDTH = GLA_HEADS * GLA_DV
GLA_RANK = 16
GLA_GATE_NORM = 16.0
GLA_CHUNK = 64
CONV_WIDTH = 512
CONV_GDIM = 64
PEER_HEADS = 8
PEER_NKEYS = 128
PEER_DQH = 128
PEER_TOPK = 16
N_ADA = 6
EPS = 1e-6

LANES = 128
SUBLANES = 8
VMEM_LIMIT = 56 * 1024 * 1024

C_QK, C_V, C_G, C_CB, C_CC, C_CH, C_AL, C_END = 0, 512, 1024, 1536, 2048, 2560, 3072, 3200

TOKEN_TILE = 256
ROUTE_TILE = 512
PEER_TOKEN_TILE = 512
EXPERT_BLOCK = 1024
GATE_HALF = PEER_NKEYS // 2
GATE_PITCH = GATE_HALF + SUBLANES
GATE_UNROLL = 8


def _dot(a, b):
    return jnp.dot(a, b, preferred_element_type=F32)


def _dot_nt(a, b):
    return lax.dot_general(a, b, (((1,), (1,)), ((), ())), preferred_element_type=F32)


def _dot_tn(a, b):
    return lax.dot_general(a, b, (((0,), (0,)), ((), ())), preferred_element_type=F32)


def _split2(x):
    hi = x.astype(BF16)
    lo = (x - hi.astype(F32)).astype(BF16)
    return hi, lo


def _split3(x):
    hi = x.astype(BF16)
    r = x - hi.astype(F32)
    mid = r.astype(BF16)
    lo = (r - mid.astype(F32)).astype(BF16)
    return hi, mid, lo


def _dot_f32(a, b):
    a_hi, a_lo = _split2(a)
    b_hi, b_lo = _split2(b)
    return _dot(a_hi, b_hi) + _dot(a_hi, b_lo) + _dot(a_lo, b_hi)


def _bf16_bits(x):
    return lax.bitcast_convert_type(x.astype(BF16).astype(F32), jnp.uint32)


def _rms(x):
    return lax.rsqrt(jnp.mean(x * x, axis=-1, keepdims=True) + EPS)


def _sigmoid(x):
    return 1.0 / (1.0 + jnp.exp(-x))


def _log_sigmoid(x):
    return jnp.minimum(x, 0.0) - jnp.log1p(jnp.exp(-jnp.abs(x)))


def _gelu_tanh(x):
    return 0.5 * x * (1.0 + jnp.tanh(0.7978845608028654 * (x + 0.044715 * (x * x * x))))


def _ada_kernel(c_ref, w_ref, b_ref, o_ref):
    c = c_ref[...]
    o_ref[...] = _dot_f32(c * _sigmoid(c), w_ref[...]) + b_ref[...]


def _ada(c_pad, w_ada, b_ada):
    n_col = N_ADA * D_MODEL
    blk = 1536
    return pl.pallas_call(
        _ada_kernel,
        out_shape=jax.ShapeDtypeStruct((SUBLANES, n_col), F32),
        grid=(n_col // blk,),
        in_specs=[
            pl.BlockSpec((SUBLANES, D_MODEL), lambda j: (0, 0)),
            pl.BlockSpec((D_MODEL, blk), lambda j: (0, j)),
            pl.BlockSpec((1, blk), lambda j: (0, j)),
        ],
        out_specs=pl.BlockSpec((SUBLANES, blk), lambda j: (0, j)),
        compiler_params=pltpu.CompilerParams(dimension_semantics=("arbitrary",), vmem_limit_bytes=VMEM_LIMIT),
        name="ada",
    )(c_pad, w_ada, b_ada)


def _gla_tile(q, k, v, la, st_ref, forward):
    tt = q.shape[0]
    n_chunk = tt // GLA_CHUNK
    row = lax.broadcasted_iota(jnp.int32, (tt, tt), 0)
    col = lax.broadcasted_iota(jnp.int32, (tt, tt), 1)
    ordered = (row >= col) if forward else (row <= col)
    causal = ordered & (row // GLA_CHUNK == col // GLA_CHUNK)
    tri = jnp.where(causal, 1.0, 0.0).astype(BF16)
    la_hi, la_mid, la_lo = _split3(la)
    b = _dot(tri, la_hi) + _dot(tri, la_mid) + _dot(tri, la_lo)
    end_rows = [ci * GLA_CHUNK + (GLA_CHUNK - 1 if forward else 0) for ci in range(n_chunk)]
    b_end = jnp.concatenate([jnp.broadcast_to(b[r:r + 1, :], (GLA_CHUNK, GLA_KW)) for r in end_rows], axis=0)
    q_dec = q * (jnp.exp(b) * (GLA_DK ** -0.5))
    k_dec = k * jnp.exp(-b)
    k_end = k * jnp.exp(b_end - b)
    decay = [jnp.exp(b[r:r + 1, :]) for r in end_rows]
    lane = lax.broadcasted_iota(jnp.int32, (1, LANES), 1)
    scan = range(n_chunk) if forward else range(n_chunk - 1, -1, -1)
    outs = []
    for head in range(GLA_HEADS):
        pair, half = head // 2, head % 2
        sl = slice(pair * LANES, (pair + 1) * LANES)
        in_head = (lane // GLA_DK) == half
        qd = jnp.where(in_head, q_dec[:, sl], 0.0).astype(BF16)
        kd = k_dec[:, sl].astype(BF16)
        ke = k_end[:, sl].astype(BF16)
        v_h = v[:, head * GLA_DV:(head + 1) * GLA_DV].astype(BF16)
        scores = jnp.where(causal, _dot_nt(qd, kd), 0.0)
        o_intra = _dot(scores.astype(BF16), v_h)
        st = st_ref[head]
        parts = [None] * n_chunk
        for ci in scan:
            rows = slice(ci * GLA_CHUNK, (ci + 1) * GLA_CHUNK)
            parts[ci] = o_intra[rows] + _dot_nt(qd[rows], st.astype(BF16))
            st = st * decay[ci][:, sl] + _dot_tn(v_h[rows], ke[rows])
        st_ref[head] = st
        outs.append(jnp.concatenate(parts, axis=0))
    return jnp.concatenate(outs, axis=-1)


def _mix1_kernel(x_ref, xp_ref, xn_ref, mod_ref, n1g_ref, win_ref, wdec_ref, bdec_ref, convw_ref, cng_ref,
                 mgrp_ref, qk_ref, v_ref, lab_ref, of_ref, gy_ref, st_s, *, tt, nt):
    i = pl.program_id(1)
    mod = mod_ref[0]
    sh1, sc1 = mod[0:1], mod[1:2]
    n1g = n1g_ref[...]

    def norm_mod(x):
        return ((x * _rms(x)) * n1g) * (1.0 + sc1) + sh1

    h = norm_mod(x_ref[0]).astype(BF16)
    qk_ref[0] = _dot(h, win_ref[:, C_QK:C_V])
    v_ref[0] = _dot(h, win_ref[:, C_V:C_G])
    gy_ref[0, :, 0:GLA_WIDTH] = _dot(h, win_ref[:, C_G:C_CB])

    hh = norm_mod(jnp.concatenate([xp_ref[0], xn_ref[0]], axis=0)).astype(BF16)
    z_halo = _dot(hh, win_ref[:, C_CC:C_CH]) * _dot(hh, win_ref[:, C_CH:C_AL])
    z_prev = jnp.where(i > 0, z_halo[SUBLANES - 1:SUBLANES, :], 0.0)
    z_next = jnp.where(i < nt - 1, z_halo[SUBLANES:SUBLANES + 1, :], 0.0)
    z = _dot(h, win_ref[:, C_CC:C_CH]) * _dot(h, win_ref[:, C_CH:C_AL])
    row = lax.broadcasted_iota(jnp.int32, (tt, 1), 0)
    z_m1 = jnp.where(row == 0, z_prev, pltpu.roll(z, 1, axis=0))
    z_p1 = jnp.where(row == tt - 1, z_next, pltpu.roll(z, tt - 1, axis=0))
    cw = convw_ref[...]
    conv = cw[0:1] * z_m1 + cw[1:2] * z + cw[2:3] * z_p1
    yc = _dot(h, win_ref[:, C_CB:C_CC]) * conv
    sq_hi, sq_lo = _split2(yc * yc)
    ss = _dot(sq_hi, mgrp_ref[...]) + _dot(sq_lo, mgrp_ref[...])
    gy_ref[0, :, GLA_WIDTH:] = (yc * lax.rsqrt(ss * (1.0 / CONV_GDIM) + EPS)) * cng_ref[...]

    zd = _dot_f32(_dot(h, win_ref[:, C_AL:C_END]), wdec_ref[...]) + bdec_ref[...]
    la = _log_sigmoid(zd) * (1.0 / GLA_GATE_NORM)
    lab_ref[0] = la[:, GLA_KW:]

    @pl.when(i == 0)
    def _():
        st_s[...] = jnp.zeros_like(st_s)

    of_ref[0] = _gla_tile(qk_ref[0, :, 0:GLA_KW], qk_ref[0, :, GLA_KW:], v_ref[0], la[:, 0:GLA_KW], st_s, True)


def _mix1(x, mod, n1g, win, wdec, bdec, convw, cng, mgrp, tt):
    bsz, seq, _ = x.shape
    nt = seq // tt
    hb = tt // SUBLANES
    last_hb = seq // SUBLANES - 1
    const = lambda shape: pl.BlockSpec(shape, lambda b, i: tuple(0 for _ in shape))
    tile = lambda w: pl.BlockSpec((1, tt, w), lambda b, i: (b, i, 0))
    out_w = (2 * GLA_KW, GLA_WIDTH, GLA_KW, GLA_WIDTH, GLA_WIDTH + CONV_WIDTH)
    return pl.pallas_call(
        functools.partial(_mix1_kernel, tt=tt, nt=nt),
        out_shape=tuple(jax.ShapeDtypeStruct((bsz, seq, w), F32) for w in out_w),
        grid=(bsz, nt),
        in_specs=[
            tile(D_MODEL),
            pl.BlockSpec((1, SUBLANES, D_MODEL), lambda b, i: (b, jnp.maximum(i * hb - 1, 0), 0)),
            pl.BlockSpec((1, SUBLANES, D_MODEL), lambda b, i: (b, jnp.minimum((i + 1) * hb, last_hb), 0)),
            pl.BlockSpec((1, SUBLANES, D_MODEL), lambda b, i: (b, 0, 0)),
            const((1, D_MODEL)),
            const((D_MODEL, C_END)),
            const((LANES, 2 * GLA_KW)),
            const((1, 2 * GLA_KW)),
            const((3, CONV_WIDTH)),
            const((1, CONV_WIDTH)),
            const((CONV_WIDTH, CONV_WIDTH)),
        ],
        out_specs=tuple(tile(w) for w in out_w),
        scratch_shapes=[pltpu.VMEM((GLA_HEADS, GLA_DV, LANES), F32)],
        compiler_params=pltpu.CompilerParams(dimension_semantics=("arbitrary", "arbitrary"),
                                             vmem_limit_bytes=VMEM_LIMIT),
        name="mix1",
    )(x, x, x, mod, n1g, win, wdec, bdec, convw, cng, mgrp)


def _mix2_kernel(x_ref, mod_ref, qk_ref, v_ref, lab_ref, of_ref, gy_ref, gng_ref, wout_ref, n2g_ref,
                 x1_ref, h2_ref, st_s):
    i = pl.program_id(1)

    @pl.when(i == 0)
    def _():
        st_s[...] = jnp.zeros_like(st_s)

    mod = mod_ref[0]
    gt1, sh2, sc2 = mod[2:3], mod[3:4], mod[4:5]
    o = of_ref[0] + _gla_tile(qk_ref[0, :, 0:GLA_KW], qk_ref[0, :, GLA_KW:], v_ref[0], lab_ref[0], st_s, False)
    gng = gng_ref[...]
    y = None
    for head in range(GLA_HEADS):
        sl = slice(head * GLA_DV, (head + 1) * GLA_DV)
        oh = o[:, sl]
        g = gy_ref[0, :, sl]
        yh = ((oh * _rms(oh)) * gng) * (g * _sigmoid(g))
        part = _dot(yh.astype(BF16), wout_ref[sl, :])
        y = part if y is None else y + part
    y = y + _dot(gy_ref[0, :, GLA_WIDTH:].astype(BF16), wout_ref[GLA_WIDTH:, :])
    x1 = x_ref[0] + gt1 * y
    x1_ref[0] = x1
    h2_ref[0] = (((x1 * _rms(x1)) * n2g_ref[...]) * (1.0 + sc2) + sh2).astype(BF16)


def _mix2(x, mod, qk, v, lab, of, gy, gng, wout, n2g, tt):
    bsz, seq, _ = x.shape
    nt = seq // tt
    const = lambda shape: pl.BlockSpec(shape, lambda b, i: tuple(0 for _ in shape))
    tile = lambda w: pl.BlockSpec((1, tt, w), lambda b, i: (b, nt - 1 - i, 0))
    return pl.pallas_call(
        _mix2_kernel,
        out_shape=(jax.ShapeDtypeStruct((bsz, seq, D_MODEL), F32), jax.ShapeDtypeStruct((bsz, seq, D_MODEL), BF16)),
        grid=(bsz, nt),
        in_specs=[
            tile(D_MODEL),
            pl.BlockSpec((1, SUBLANES, D_MODEL), lambda b, i: (b, 0, 0)),
            tile(2 * GLA_KW), tile(GLA_WIDTH), tile(GLA_KW), tile(GLA_WIDTH), tile(GLA_WIDTH + CONV_WIDTH),
            const((1, GLA_DV)),
            const((D_MODEL, D_MODEL)),
            const((1, D_MODEL)),
        ],
        out_specs=(tile(D_MODEL), tile(D_MODEL)),
        scratch_shapes=[pltpu.VMEM((GLA_HEADS, GLA_DV, LANES), F32)],
        compiler_params=pltpu.CompilerParams(dimension_semantics=("arbitrary", "arbitrary"),
                                             vmem_limit_bytes=VMEM_LIMIT),
        name="mix2",
    )(x, mod, qk, v, lab, of, gy, gng, wout, n2g)


NEG_INF = float("-inf")
KEY_ROWS = 2 * PEER_NKEYS * PEER_HEADS
VREG = (SUBLANES, LANES)


def _keyproj_kernel(wq_ref, sk_ref, o_ref):
    w_hi, w_lo = _split2(wq_ref[...])
    k_hi, k_lo = _split2(sk_ref[0])
    o_ref[0] = _dot_nt(k_hi, w_hi) + _dot_nt(k_hi, w_lo) + _dot_nt(k_lo, w_hi)


def _keyproj(wq, subkeys):
    n_hp = 2 * PEER_HEADS
    return pl.pallas_call(
        _keyproj_kernel,
        out_shape=jax.ShapeDtypeStruct((n_hp, PEER_NKEYS, D_MODEL), F32),
        grid=(n_hp,),
        in_specs=[pl.BlockSpec((D_MODEL, PEER_DQH), lambda j: (0, j)),
                  pl.BlockSpec((1, PEER_NKEYS, PEER_DQH), lambda j: (j, 0, 0))],
        out_specs=pl.BlockSpec((1, PEER_NKEYS, D_MODEL), lambda j: (j, 0, 0)),
        compiler_params=pltpu.CompilerParams(dimension_semantics=("arbitrary",), vmem_limit_bytes=VMEM_LIMIT),
        name="keyproj",
    )(wq, subkeys)


def _sort16_network():
    pairs = []

    def merge(lo, hi, r):
        step = r * 2
        if step < hi - lo:
            merge(lo, hi, step)
            merge(lo + r, hi, step)
            pairs.extend((i, i + r) for i in range(lo + r, hi - r, step))
        else:
            pairs.append((lo, lo + r))

    def sort(lo, hi):
        if hi - lo >= 1:
            mid = lo + (hi - lo) // 2
            sort(lo, mid)
            sort(mid + 1, hi)
            merge(lo, hi, 1)

    sort(0, PEER_TOPK - 1)
    return tuple(pairs)


SORT16 = _sort16_network()
CAND_LISTS = ([[(0, k2) for k2 in range(PEER_TOPK)]]
              + [[(k1, k2) for k2 in range(PEER_TOPK // (k1 + 1))] for k1 in range(1, 8)]
              + [[(k1, 0) for k1 in range(8, PEER_TOPK)]])
CANDS = [c for lst in CAND_LISTS for c in lst]


def _tree(op, xs):
    xs = list(xs)
    while len(xs) > 1:
        xs = [op(xs[i], xs[i + 1]) for i in range(0, len(xs) - 1, 2)] + ([xs[-1]] if len(xs) % 2 else [])
    return xs[0]


def _ce(v, x, i, j):
    c = v[j] > v[i]
    v[i], v[j] = jnp.maximum(v[i], v[j]), jnp.minimum(v[i], v[j])
    x[i], x[j] = jnp.where(c, x[j], x[i]), jnp.where(c, x[i], x[j])


def _merge_top16(va, xa, vb, xb, sort_result):
    v, x = list(va), list(xa)
    for j in range(PEER_TOPK - len(vb), PEER_TOPK):
        b = PEER_TOPK - 1 - j
        c = vb[b] > va[j]
        v[j] = jnp.maximum(va[j], vb[b])
        x[j] = jnp.where(c, xb[b], xa[j])
    if sort_result:
        for d in (8, 4, 2, 1):
            for i in range(PEER_TOPK):
                if not i & d:
                    _ce(v, x, i, i + d)
    return v, x


def _any(flag):
    return jnp.max(flag) > 0.0


def _top16_keys(key, work_v, work_x, sv_ref, si_ref):
    for g in range(PEER_NKEYS // PEER_TOPK):
        v = [key(PEER_TOPK * g + i) for i in range(PEER_TOPK)]
        x = [float(PEER_TOPK * g + i) for i in range(PEER_TOPK)]
        for i, j in SORT16:
            _ce(v, x, i, j)
        for i in range(PEER_TOPK):
            work_v[PEER_TOPK * g + i] = v[i]
            work_x[PEER_TOPK * g + i] = x[i]
    for span in (1, 2, 4):
        for g in range(0, PEER_NKEYS // PEER_TOPK, 2 * span):
            a, b = PEER_TOPK * g, PEER_TOPK * (g + span)
            v, x = _merge_top16([work_v[a + i] for i in range(PEER_TOPK)], [work_x[a + i] for i in range(PEER_TOPK)],
                                [work_v[b + i] for i in range(PEER_TOPK)], [work_x[b + i] for i in range(PEER_TOPK)],
                                True)
            for i in range(PEER_TOPK):
                if span == 4:
                    sv_ref[i], si_ref[i] = v[i], x[i]
                else:
                    work_v[a + i], work_x[a + i] = v[i], x[i]
    repeat = _tree(jnp.maximum, [jnp.where(v[r] == v[r + 1], 1.0, 0.0) for r in range(PEER_TOPK - 1)])
    n_ge = _tree(jnp.add, [jnp.where(key(k) >= v[PEER_TOPK - 1], 1.0, 0.0) for k in range(PEER_NKEYS)])

    @pl.when(_any(jnp.maximum(repeat, jnp.where(n_ge > float(PEER_TOPK), 1.0, 0.0))))
    def _():
        for k in range(PEER_NKEYS):
            work_v[k] = key(k)

        def extract(r, carry):
            vals = [work_v[k] for k in range(PEER_NKEYS)]
            m = _tree(jnp.maximum, vals)
            idx = _tree(jnp.minimum, [jnp.where(vals[k] == m, float(k), float(PEER_NKEYS)) for k in range(PEER_NKEYS)])
            for k in range(PEER_NKEYS):
                work_v[k] = jnp.where(idx == float(k), NEG_INF, vals[k])
            sv_ref[r] = m
            si_ref[r] = idx
            return carry

        lax.fori_loop(0, PEER_TOPK, extract, 0)


def _top16_sums(sv_s, si_s, work_v, work_x, rv_s, re_s):
    sv0 = [sv_s[0, r] for r in range(PEER_TOPK)]
    sv1 = [sv_s[1, r] for r in range(PEER_TOPK)]
    e_hi = [si_s[0, r] * float(PEER_NKEYS) for r in range(PEER_TOPK)]
    si1 = [si_s[1, r] for r in range(PEER_TOPK)]
    val = {c: sv0[c[0]] + sv1[c[1]] for c in CANDS}
    eid = {c: e_hi[c[0]] + si1[c[1]] for c in CANDS}
    v, x = [val[c] for c in CAND_LISTS[0]], [eid[c] for c in CAND_LISTS[0]]
    for n, lst in enumerate(CAND_LISTS[1:]):
        v, x = _merge_top16(v, x, [val[c] for c in lst], [eid[c] for c in lst], n < len(CAND_LISTS) - 2)
    for r in range(PEER_TOPK):
        rv_s[r], re_s[r] = v[r], x[r]
    low = _tree(jnp.minimum, v)
    n_ge = _tree(jnp.add, [jnp.where(val[c] >= low, 1.0, 0.0) for c in CANDS])

    @pl.when(_any(jnp.where(n_ge > float(PEER_TOPK), 1.0, 0.0)))
    def _():
        for n, c in enumerate(CANDS):
            work_v[n], work_x[n] = val[c], eid[c]

        def extract(r, carry):
            vals = [work_v[n] for n in range(len(CANDS))]
            m = _tree(jnp.maximum, vals)
            code = [float(c[0] * PEER_TOPK + c[1]) for c in CANDS]
            sel = _tree(jnp.minimum, [jnp.where(vals[n] == m, code[n], 1e9) for n in range(len(CANDS))])
            hit = [sel == code[n] for n in range(len(CANDS))]
            rv_s[r] = m
            re_s[r] = _tree(jnp.add, [jnp.where(hit[n], work_x[n], 0.0) for n in range(len(CANDS))])
            for n in range(len(CANDS)):
                work_v[n] = jnp.where(hit[n], NEG_INF, vals[n])
            return carry

        lax.fori_loop(0, PEER_TOPK, extract, 0)


def _route_kernel(h2_ref, wpt_ref, a_ref, b_ref, g_ref, sc_s, work_v, work_x, sv_s, si_s, rv_s, re_s,
                  pa_s, pb_s, pg_s, *, tt):
    sc_s[...] = _dot_nt(wpt_ref[...], h2_ref[0])

    def lane_tile(c, carry):
        lanes = pl.ds(pl.multiple_of(c * LANES, LANES), LANES)
        for p in range(2):
            def key(k, p=p):
                r0 = (p * PEER_NKEYS + k) * SUBLANES
                return sc_s[r0:r0 + SUBLANES, lanes]
            _top16_keys(key, work_v, work_x, sv_s.at[p], si_s.at[p])
        _top16_sums(sv_s, si_s, work_v, work_x, rv_s, re_s)
        v = [rv_s[r] for r in range(PEER_TOPK)]
        m = _tree(jnp.maximum, v)
        ex = [jnp.exp(vr - m) for vr in v]
        inv = 1.0 / _tree(jnp.add, ex)
        for r in range(PEER_TOPK):
            e = re_s[r]
            i1 = jnp.floor(e * (1.0 / PEER_NKEYS))
            rows = slice(r * SUBLANES, (r + 1) * SUBLANES)
            pa_s[rows, :] = i1
            pb_s[rows, :] = e - i1 * float(PEER_NKEYS)
            pg_s[rows, :] = ex[r] * inv
        toks = pl.ds(pl.multiple_of(c * LANES, LANES), LANES)
        a_ref[0, toks, :] = jnp.transpose(pa_s[...])
        b_ref[0, toks, :] = jnp.transpose(pb_s[...])
        g_ref[0, toks, :] = jnp.transpose(pg_s[...])
        return carry

    lax.fori_loop(0, tt // LANES, lane_tile, 0)


def _route(h2, wpt, tt):
    bsz, seq, _ = h2.shape
    n_pair = PEER_HEADS * PEER_TOPK
    tile = lambda w: pl.BlockSpec((1, tt, w), lambda b, i: (b, i, 0))
    vregs = lambda *lead: pltpu.VMEM(lead + VREG, F32)
    return pl.pallas_call(
        functools.partial(_route_kernel, tt=tt),
        out_shape=tuple(jax.ShapeDtypeStruct((bsz, seq, n_pair), F32) for _ in range(3)),
        grid=(bsz, seq // tt),
        in_specs=[tile(D_MODEL), pl.BlockSpec((KEY_ROWS, D_MODEL), lambda b, i: (0, 0))],
        out_specs=(tile(n_pair), tile(n_pair), tile(n_pair)),
        scratch_shapes=[pltpu.VMEM((KEY_ROWS, tt), F32), vregs(PEER_NKEYS), vregs(PEER_NKEYS),
                        vregs(2, PEER_TOPK), vregs(2, PEER_TOPK), vregs(PEER_TOPK), vregs(PEER_TOPK),
                        pltpu.VMEM((n_pair, LANES), F32), pltpu.VMEM((n_pair, LANES), F32),
                        pltpu.VMEM((n_pair, LANES), F32)],
        compiler_params=pltpu.CompilerParams(dimension_semantics=("arbitrary", "arbitrary"),
                                             vmem_limit_bytes=VMEM_LIMIT),
        name="route",
    )(h2, wpt)


def _peer_kernel(h2_ref, x1_ref, mod_ref, a_ref, b_ref, g_ref, ut_ref, v_ref, nfg_ref, y_ref, gate_s, acc_s,
                 *, tt, eb, ne):
    j = pl.program_id(2)
    n_grp = eb // PEER_NKEYS
    steps_per_half = ne // 2

    @pl.when(j == 0)
    def _():
        acc_s[...] = jnp.zeros_like(acc_s)
        sub = lax.broadcasted_iota(jnp.int32, (PEER_NKEYS, PEER_NKEYS), 0).astype(F32).astype(BF16)
        zero = jnp.zeros((PEER_NKEYS, PEER_NKEYS), BF16)
        one = jnp.ones((PEER_NKEYS, PEER_NKEYS), BF16)

        def onehots(t):
            bc = lambda ref: jnp.broadcast_to(ref[0, pl.ds(t, 1), :], (PEER_NKEYS, PEER_NKEYS)).astype(BF16)
            at = jnp.where(sub == bc(a_ref), one, zero)
            cbt = jnp.where(sub == bc(b_ref), bc(g_ref), zero)
            return at, cbt

        def build(p, carry):
            for u in range(GATE_UNROLL):
                t0 = (p * GATE_UNROLL + u) * 2
                at0, cbt0 = onehots(t0)
                at1, cbt1 = onehots(t0 + 1)
                lhs = jnp.concatenate([at0, at1], axis=1)
                rhs = jnp.concatenate([jnp.concatenate([cbt0, zero], axis=1),
                                       jnp.concatenate([zero, cbt1], axis=1)], axis=0)
                tiles = _dot_nt(lhs, rhs)
                for k in range(2):
                    lo = tiles[0:GATE_HALF, k * PEER_NKEYS:(k + 1) * PEER_NKEYS]
                    hi = tiles[GATE_HALF:, k * PEER_NKEYS:(k + 1) * PEER_NKEYS]
                    r0 = pl.multiple_of((t0 + k) * GATE_PITCH, SUBLANES)
                    gate_s[pl.ds(r0, GATE_HALF), :] = (_bf16_bits(lo) >> 16) | _bf16_bits(hi)
            return carry

        lax.fori_loop(0, tt // (2 * GATE_UNROLL), build, 0)

    def expert_block(half):
        s = _dot(h2_ref[0], ut_ref[...])
        parts = []
        for gi in range(n_grp):
            r = (j - half * steps_per_half) * n_grp + gi
            packed = gate_s[pl.ds(r, tt, stride=GATE_PITCH), :]
            word = (packed & jnp.uint32(0xFFFF0000)) if half else (packed << 16)
            gate = lax.bitcast_convert_type(word, F32)
            parts.append((gate * _gelu_tanh(s[:, gi * PEER_NKEYS:(gi + 1) * PEER_NKEYS])).astype(BF16))
        acc_s[...] += _dot(jnp.concatenate(parts, axis=1), v_ref[...])

    @pl.when(j < steps_per_half)
    def _():
        expert_block(0)

    @pl.when(j >= steps_per_half)
    def _():
        expert_block(1)

    @pl.when(j == ne - 1)
    def _():
        gt2 = mod_ref[0][5:6]
        x2 = x1_ref[0] + gt2 * acc_s[...]
        y_ref[0] = (x2 * _rms(x2)) * nfg_ref[...]


def _peer(h2, x1, mod, a_idx, b_idx, gates, ut, v, nfg, tt, eb):
    bsz, seq, _ = h2.shape
    n_exp = v.shape[0]
    ne = n_exp // eb
    assert ne % 2 == 0 and (GATE_HALF * PEER_NKEYS) % eb == 0 and tt % (2 * GATE_UNROLL) == 0
    n_pair = PEER_HEADS * PEER_TOPK
    tile = lambda w: pl.BlockSpec((1, tt, w), lambda b, i, j: (b, i, 0))
    return pl.pallas_call(
        functools.partial(_peer_kernel, tt=tt, eb=eb, ne=ne),
        out_shape=jax.ShapeDtypeStruct((bsz, seq, D_MODEL), F32),
        grid=(bsz, seq // tt, ne),
        in_specs=[
            tile(D_MODEL), tile(D_MODEL),
            pl.BlockSpec((1, SUBLANES, D_MODEL), lambda b, i, j: (b, 0, 0)),
            tile(n_pair), tile(n_pair), tile(n_pair),
            pl.BlockSpec((D_MODEL, eb), lambda b, i, j: (0, j)),
            pl.BlockSpec((eb, D_MODEL), lambda b, i, j: (j, 0)),
            pl.BlockSpec((1, D_MODEL), lambda b, i, j: (0, 0)),
        ],
        out_specs=tile(D_MODEL),
        scratch_shapes=[pltpu.VMEM((tt * GATE_PITCH, PEER_NKEYS), jnp.uint32), pltpu.VMEM((tt, D_MODEL), F32)],
        compiler_params=pltpu.CompilerParams(dimension_semantics=("arbitrary", "arbitrary", "arbitrary"),
                                             vmem_limit_bytes=VMEM_LIMIT),
        name="peer",
    )(h2, x1, mod, a_idx, b_idx, gates, ut, v, nfg)


def _prep_weights(norm1_g, norm2_g, w_in, w_dec_f, b_dec_f, w_dec_b, b_dec_b, gla_norm_g, conv_w, conv_norm_g,
                  w_out, peer_wq, peer_subkeys, peer_u, peer_v, normf_g):
    w = w_in[0]
    q, k, v, g, alf, alb, cb, cc, ch = jnp.split(w, (256, 512, 1024, 1536, 1552, 1568, 2080, 2592), axis=-1)
    pad = jnp.zeros((D_MODEL, C_END - C_AL - 2 * GLA_RANK), F32)
    win = jnp.concatenate([q, k, v, g, cb, cc, ch, alf, alb, pad], axis=-1).astype(BF16)
    wdec = jnp.zeros((LANES, 2 * GLA_KW), F32)
    wdec = wdec.at[0:GLA_RANK, 0:GLA_KW].set(w_dec_f[0]).at[GLA_RANK:2 * GLA_RANK, GLA_KW:].set(w_dec_b[0])
    bdec = jnp.concatenate([b_dec_f[0], b_dec_b[0]])[None, :]
    grp = jnp.arange(CONV_WIDTH) // CONV_GDIM
    mgrp = (grp[:, None] == grp[None, :]).astype(BF16)
    wp = _keyproj(peer_wq[0], peer_subkeys[0].reshape(2 * PEER_HEADS, PEER_NKEYS, PEER_DQH))
    wpt = wp.reshape(PEER_HEADS, 2, PEER_NKEYS, D_MODEL).transpose(1, 2, 0, 3).reshape(KEY_ROWS, D_MODEL)
    return dict(
        n1g=norm1_g[0][None, :], n2g=norm2_g[0][None, :], win=win, wdec=wdec, bdec=bdec,
        convw=conv_w[0], cng=conv_norm_g[0][None, :], mgrp=mgrp, gng=gla_norm_g[0][None, :],
        wout=w_out[0].astype(BF16), wpt=wpt.astype(BF16),
        ut=jnp.transpose(peer_u[0]).astype(BF16), v=peer_v[0].astype(BF16), nfg=normf_g[None, :],
    )


def _trunk(x, mod, w, tt, rtt, ptt, eb):
    qk, v, lab, of, gy = _mix1(x, mod, w["n1g"], w["win"], w["wdec"], w["bdec"], w["convw"], w["cng"], w["mgrp"], tt)
    x1, h2 = _mix2(x, mod, qk, v, lab, of, gy, w["gng"], w["wout"], w["n2g"], tt)
    a_idx, b_idx, gates = _route(h2, w["wpt"], rtt)
    return _peer(h2, x1, mod, a_idx, b_idx, gates, w["ut"], w["v"], w["nfg"], ptt, eb)


def kernel(x_prompt, x_sample, c_prompt, c_sample, norm1_g, norm2_g, w_ada, b_ada, w_in, w_dec_f, b_dec_f,
           w_dec_b, b_dec_b, gla_norm_g, conv_w, conv_norm_g, w_out, peer_wq, peer_subkeys, peer_u, peer_v,
           normf_g):
    w = _prep_weights(norm1_g, norm2_g, w_in, w_dec_f, b_dec_f, w_dec_b, b_dec_b, gla_norm_g, conv_w,
                      conv_norm_g, w_out, peer_wq, peer_subkeys, peer_u, peer_v, normf_g)
    nb_p, nb_s = c_prompt.shape[0], c_sample.shape[0]
    c_all = jnp.concatenate([c_prompt, c_sample, jnp.zeros((SUBLANES - nb_p - nb_s, D_MODEL), F32)], axis=0)
    ada = _ada(c_all, w_ada[0], b_ada[0][None, :])
    mod = jnp.pad(ada.reshape(SUBLANES, N_ADA, D_MODEL), ((0, 0), (0, SUBLANES - N_ADA), (0, 0)))
    y_prompt = _trunk(x_prompt, mod[:nb_p], w, TOKEN_TILE, ROUTE_TILE, PEER_TOKEN_TILE, EXPERT_BLOCK)
    y_sample = _trunk(x_sample, mod[nb_p:nb_p + nb_s], w, TOKEN_TILE, ROUTE_TILE, PEER_TOKEN_TILE, EXPERT_BLOCK)
    return (y_prompt, y_sample)
```

```python
import functools

import jax
import jax.numpy as jnp
from jax import lax
from jax.experimental import pallas as pl
from jax.experimental.pallas import tpu as pltpu

F32 = jnp.float32
BF16 = jnp.bfloat16

D_MODEL = 1024
GLA_HEADS = 4
GLA_DV = 128
GLA_DK = 64
GLA_KW = GLA_HEADS * GLA_DK
GLA_WIDTH = GLA_HEADS * GLA_DV
GLA_RANK = 16
GLA_GATE_NORM = 16.0
GLA_CHUNK = 64
CONV_WIDTH = 512
CONV_GDIM = 64
PEER_HEADS = 8
PEER_NKEYS = 128
PEER_DQH = 128
PEER_TOPK = 16
N_ADA = 6
EPS = 1e-6

LANES = 128
SUBLANES = 8
VMEM_LIMIT = 56 * 1024 * 1024

C_QK, C_V, C_G, C_CB, C_CC, C_CH, C_AL, C_END = 0, 512, 1024, 1536, 2048, 2560, 3072, 3200

TOKEN_TILE = 256
ROUTE_TILE = 512
PEER_TOKEN_TILE = 512
EXPERT_BLOCK = 2048
EXPERT_SUB = 1024
GATE_HALF = PEER_NKEYS // 2
GATE_PITCH = GATE_HALF + SUBLANES
GATE_UNROLL = 16


def _dot(a, b):
    return jnp.dot(a, b, preferred_element_type=F32)


def _dot_nt(a, b):
    return lax.dot_general(a, b, (((1,), (1,)), ((), ())), preferred_element_type=F32)


def _dot_tn(a, b):
    return lax.dot_general(a, b, (((0,), (0,)), ((), ())), preferred_element_type=F32)


def _split2(x):
    hi = x.astype(BF16)
    lo = (x - hi.astype(F32)).astype(BF16)
    return hi, lo


def _split3(x):
    hi = x.astype(BF16)
    r = x - hi.astype(F32)
    mid = r.astype(BF16)
    lo = (r - mid.astype(F32)).astype(BF16)
    return hi, mid, lo


def _dot_f32(a, b):
    a_hi, a_lo = _split2(a)
    b_hi, b_lo = _split2(b)
    return _dot(a_hi, b_hi) + _dot(a_hi, b_lo) + _dot(a_lo, b_hi)


def _bf16_bits(x):
    return lax.bitcast_convert_type(x.astype(BF16).astype(F32), jnp.uint32)


def _rms(x):
    return lax.rsqrt(jnp.mean(x * x, axis=-1, keepdims=True) + EPS)


def _sigmoid(x):
    return 1.0 / (1.0 + jnp.exp(-x))


def _log_sigmoid(x):
    return jnp.minimum(x, 0.0) - jnp.log1p(jnp.exp(-jnp.abs(x)))


def _gelu_tanh(x):
    return 0.5 * x * (1.0 + jnp.tanh(0.7978845608028654 * (x + 0.044715 * (x * x * x))))


def _ada_kernel(c_ref, w_ref, b_ref, o_ref):
    c = c_ref[...]
    o_ref[...] = _dot_f32(c * _sigmoid(c), w_ref[...]) + b_ref[...]


def _ada(c_pad, w_ada, b_ada):
    n_col = N_ADA * D_MODEL
    blk = 1536
    return pl.pallas_call(
        _ada_kernel,
        out_shape=jax.ShapeDtypeStruct((SUBLANES, n_col), F32),
        grid=(n_col // blk,),
        in_specs=[
            pl.BlockSpec((SUBLANES, D_MODEL), lambda j: (0, 0)),
            pl.BlockSpec((D_MODEL, blk), lambda j: (0, j)),
            pl.BlockSpec((1, blk), lambda j: (0, j)),
        ],
        out_specs=pl.BlockSpec((SUBLANES, blk), lambda j: (0, j)),
        compiler_params=pltpu.CompilerParams(dimension_semantics=("arbitrary",), vmem_limit_bytes=VMEM_LIMIT),
        name="ada",
    )(c_pad, w_ada, b_ada)


def _gla_tile(q, k, v, la, st_ref, forward):
    tt = q.shape[0]
    n_chunk = tt // GLA_CHUNK
    row = lax.broadcasted_iota(jnp.int32, (tt, tt), 0)
    col = lax.broadcasted_iota(jnp.int32, (tt, tt), 1)
    ordered = (row >= col) if forward else (row <= col)
    causal = ordered & (row // GLA_CHUNK == col // GLA_CHUNK)
    tri = jnp.where(causal, 1.0, 0.0).astype(BF16)
    la_hi, la_mid, la_lo = _split3(la)
    b = _dot(tri, la_hi) + _dot(tri, la_mid) + _dot(tri, la_lo)
    end_rows = [ci * GLA_CHUNK + (GLA_CHUNK - 1 if forward else 0) for ci in range(n_chunk)]
    b_end = jnp.concatenate([jnp.broadcast_to(b[r:r + 1, :], (GLA_CHUNK, GLA_KW)) for r in end_rows], axis=0)
    q_dec = q * (jnp.exp(b) * (GLA_DK ** -0.5))
    k_dec = k * jnp.exp(-b)
    k_end = k * jnp.exp(b_end - b)
    decay = [jnp.exp(b[r:r + 1, :]) for r in end_rows]
    lane = lax.broadcasted_iota(jnp.int32, (1, LANES), 1)
    scan = range(n_chunk) if forward else range(n_chunk - 1, -1, -1)
    outs = []
    for head in range(GLA_HEADS):
        pair, half = head // 2, head % 2
        sl = slice(pair * LANES, (pair + 1) * LANES)
        in_head = (lane // GLA_DK) == half
        qd = jnp.where(in_head, q_dec[:, sl], 0.0).astype(BF16)
        kd = k_dec[:, sl].astype(BF16)
        ke = k_end[:, sl].astype(BF16)
        v_h = v[:, head * GLA_DV:(head + 1) * GLA_DV].astype(BF16)
        scores = jnp.where(causal, _dot_nt(qd, kd), 0.0)
        o_intra = _dot(scores.astype(BF16), v_h)
        st = st_ref[head]
        parts = [None] * n_chunk
        for ci in scan:
            rows = slice(ci * GLA_CHUNK, (ci + 1) * GLA_CHUNK)
            parts[ci] = o_intra[rows] + _dot_nt(qd[rows], st.astype(BF16))
            st = st * decay[ci][:, sl] + _dot_tn(v_h[rows], ke[rows])
        st_ref[head] = st
        outs.append(jnp.concatenate(parts, axis=0))
    return jnp.concatenate(outs, axis=-1)


def _mix1_kernel(x_ref, xp_ref, xn_ref, mod_ref, n1g_ref, win_ref, wdec_ref, bdec_ref, convw_ref, cng_ref,
                 mgrp_ref, qk_ref, v_ref, lab_ref, of_ref, gy_ref, st_s, *, tt, nt):
    i = pl.program_id(1)
    mod = mod_ref[0]
    sh1, sc1 = mod[0:1], mod[1:2]
    n1g = n1g_ref[...]

    def norm_mod(x):
        return ((x * _rms(x)) * n1g) * (1.0 + sc1) + sh1

    h = norm_mod(x_ref[0]).astype(BF16)
    qk_ref[0] = _dot(h, win_ref[:, C_QK:C_V])
    v_ref[0] = _dot(h, win_ref[:, C_V:C_G])
    gy_ref[0, :, 0:GLA_WIDTH] = _dot(h, win_ref[:, C_G:C_CB])

    hh = norm_mod(jnp.concatenate([xp_ref[0], xn_ref[0]], axis=0)).astype(BF16)
    h_ext = jnp.concatenate([h, hh], axis=0)
    z_ext = _dot(h_ext, win_ref[:, C_CC:C_CH]) * _dot(h_ext, win_ref[:, C_CH:C_AL])
    z = z_ext[0:tt]
    z_prev = jnp.where(i > 0, z_ext[tt + SUBLANES - 1:tt + SUBLANES, :], 0.0)
    z_next = jnp.where(i < nt - 1, z_ext[tt + SUBLANES:tt + SUBLANES + 1, :], 0.0)
    row = lax.broadcasted_iota(jnp.int32, (tt, 1), 0)
    z_m1 = jnp.where(row == 0, z_prev, pltpu.roll(z, 1, axis=0))
    z_p1 = jnp.where(row == tt - 1, z_next, pltpu.roll(z, tt - 1, axis=0))
    cw = convw_ref[...]
    conv = cw[0:1] * z_m1 + cw[1:2] * z + cw[2:3] * z_p1
    yc = _dot(h, win_ref[:, C_CB:C_CC]) * conv
    sq_hi, sq_lo = _split2(yc * yc)
    ss = _dot(sq_hi, mgrp_ref[...]) + _dot(sq_lo, mgrp_ref[...])
    gy_ref[0, :, GLA_WIDTH:] = (yc * lax.rsqrt(ss * (1.0 / CONV_GDIM) + EPS)) * cng_ref[...]

    zd = _dot_f32(_dot(h, win_ref[:, C_AL:C_END]), wdec_ref[...]) + bdec_ref[...]
    la = _log_sigmoid(zd) * (1.0 / GLA_GATE_NORM)
    lab_ref[0] = la[:, GLA_KW:]

    @pl.when(i == 0)
    def _():
        st_s[...] = jnp.zeros_like(st_s)

    of_ref[0] = _gla_tile(qk_ref[0, :, 0:GLA_KW], qk_ref[0, :, GLA_KW:], v_ref[0], la[:, 0:GLA_KW], st_s, True)


def _mix1(x, mod, n1g, win, wdec, bdec, convw, cng, mgrp, tt):
    bsz, seq, _ = x.shape
    nt = seq // tt
    hb = tt // SUBLANES
    last_hb = seq // SUBLANES - 1
    const = lambda shape: pl.BlockSpec(shape, lambda b, i: tuple(0 for _ in shape))
    tile = lambda w: pl.BlockSpec((1, tt, w), lambda b, i: (b, i, 0))
    out_w = (2 * GLA_KW, GLA_WIDTH, GLA_KW, GLA_WIDTH, GLA_WIDTH + CONV_WIDTH)
    return pl.pallas_call(
        functools.partial(_mix1_kernel, tt=tt, nt=nt),
        out_shape=tuple(jax.ShapeDtypeStruct((bsz, seq, w), F32) for w in out_w),
        grid=(bsz, nt),
        in_specs=[
            tile(D_MODEL),
            pl.BlockSpec((1, SUBLANES, D_MODEL), lambda b, i: (b, jnp.maximum(i * hb - 1, 0), 0)),
            pl.BlockSpec((1, SUBLANES, D_MODEL), lambda b, i: (b, jnp.minimum((i + 1) * hb, last_hb), 0)),
            pl.BlockSpec((1, SUBLANES, D_MODEL), lambda b, i: (b, 0, 0)),
            const((1, D_MODEL)),
            const((D_MODEL, C_END)),
            const((LANES, 2 * GLA_KW)),
            const((1, 2 * GLA_KW)),
            const((3, CONV_WIDTH)),
            const((1, CONV_WIDTH)),
            const((CONV_WIDTH, CONV_WIDTH)),
        ],
        out_specs=tuple(tile(w) for w in out_w),
        scratch_shapes=[pltpu.VMEM((GLA_HEADS, GLA_DV, LANES), F32)],
        compiler_params=pltpu.CompilerParams(dimension_semantics=("arbitrary", "arbitrary"),
                                             vmem_limit_bytes=VMEM_LIMIT),
        name="mix1",
    )(x, x, x, mod, n1g, win, wdec, bdec, convw, cng, mgrp)


def _mix2_kernel(x_ref, mod_ref, qk_ref, v_ref, lab_ref, of_ref, gy_ref, gng_ref, wout_ref, n2g_ref,
                 x1_ref, h2_ref, st_s):
    i = pl.program_id(1)

    @pl.when(i == 0)
    def _():
        st_s[...] = jnp.zeros_like(st_s)

    mod = mod_ref[0]
    gt1, sh2, sc2 = mod[2:3], mod[3:4], mod[4:5]
    o = of_ref[0] + _gla_tile(qk_ref[0, :, 0:GLA_KW], qk_ref[0, :, GLA_KW:], v_ref[0], lab_ref[0], st_s, False)
    gng = gng_ref[...]
    y = None
    for head in range(GLA_HEADS):
        sl = slice(head * GLA_DV, (head + 1) * GLA_DV)
        oh = o[:, sl]
        g = gy_ref[0, :, sl]
        yh = ((oh * _rms(oh)) * gng) * (g * _sigmoid(g))
        part = _dot(yh.astype(BF16), wout_ref[sl, :])
        y = part if y is None else y + part
    y = y + _dot(gy_ref[0, :, GLA_WIDTH:].astype(BF16), wout_ref[GLA_WIDTH:, :])
    x1 = x_ref[0] + gt1 * y
    x1_ref[0] = x1
    h2_ref[0] = (((x1 * _rms(x1)) * n2g_ref[...]) * (1.0 + sc2) + sh2).astype(BF16)


def _mix2(x, mod, qk, v, lab, of, gy, gng, wout, n2g, tt):
    bsz, seq, _ = x.shape
    nt = seq // tt
    const = lambda shape: pl.BlockSpec(shape, lambda b, i: tuple(0 for _ in shape))
    tile = lambda w: pl.BlockSpec((1, tt, w), lambda b, i: (b, nt - 1 - i, 0))
    return pl.pallas_call(
        _mix2_kernel,
        out_shape=(jax.ShapeDtypeStruct((bsz, seq, D_MODEL), F32), jax.ShapeDtypeStruct((bsz, seq, D_MODEL), BF16)),
        grid=(bsz, nt),
        in_specs=[
            tile(D_MODEL),
            pl.BlockSpec((1, SUBLANES, D_MODEL), lambda b, i: (b, 0, 0)),
            tile(2 * GLA_KW), tile(GLA_WIDTH), tile(GLA_KW), tile(GLA_WIDTH), tile(GLA_WIDTH + CONV_WIDTH),
            const((1, GLA_DV)),
            const((D_MODEL, D_MODEL)),
            const((1, D_MODEL)),
        ],
        out_specs=(tile(D_MODEL), tile(D_MODEL)),
        scratch_shapes=[pltpu.VMEM((GLA_HEADS, GLA_DV, LANES), F32)],
        compiler_params=pltpu.CompilerParams(dimension_semantics=("arbitrary", "arbitrary"),
                                             vmem_limit_bytes=VMEM_LIMIT),
        name="mix2",
    )(x, mod, qk, v, lab, of, gy, gng, wout, n2g)


NEG_INF = float("-inf")
KEY_ROWS = 2 * PEER_NKEYS * PEER_HEADS
VREG = (SUBLANES, LANES)


def _keyproj_kernel(wq_ref, sk_ref, o_ref):
    w_hi, w_lo = _split2(wq_ref[...])
    k_hi, k_lo = _split2(sk_ref[0])
    o_ref[0] = _dot_nt(k_hi, w_hi) + _dot_nt(k_hi, w_lo) + _dot_nt(k_lo, w_hi)


def _keyproj(wq, subkeys):
    n_hp = 2 * PEER_HEADS
    return pl.pallas_call(
        _keyproj_kernel,
        out_shape=jax.ShapeDtypeStruct((n_hp, PEER_NKEYS, D_MODEL), F32),
        grid=(n_hp,),
        in_specs=[pl.BlockSpec((D_MODEL, PEER_DQH), lambda j: (0, j)),
                  pl.BlockSpec((1, PEER_NKEYS, PEER_DQH), lambda j: (j, 0, 0))],
        out_specs=pl.BlockSpec((1, PEER_NKEYS, D_MODEL), lambda j: (j, 0, 0)),
        compiler_params=pltpu.CompilerParams(dimension_semantics=("arbitrary",), vmem_limit_bytes=VMEM_LIMIT),
        name="keyproj",
    )(wq, subkeys)


def _sort16_network():
    pairs = []

    def merge(lo, hi, r):
        step = r * 2
        if step < hi - lo:
            merge(lo, hi, step)
            merge(lo + r, hi, step)
            pairs.extend((i, i + r) for i in range(lo + r, hi - r, step))
        else:
            pairs.append((lo, lo + r))

    def sort(lo, hi):
        if hi - lo >= 1:
            mid = lo + (hi - lo) // 2
            sort(lo, mid)
            sort(mid + 1, hi)
            merge(lo, hi, 1)

    sort(0, PEER_TOPK - 1)
    return tuple(pairs)


SORT16 = _sort16_network()
CAND_LISTS = ([[(0, k2) for k2 in range(PEER_TOPK)]]
              + [[(k1, k2) for k2 in range(PEER_TOPK // (k1 + 1))] for k1 in range(1, 8)]
              + [[(k1, 0) for k1 in range(8, PEER_TOPK)]])
CANDS = [c for lst in CAND_LISTS for c in lst]


def _tree(op, xs):
    xs = list(xs)
    while len(xs) > 1:
        xs = [op(xs[i], xs[i + 1]) for i in range(0, len(xs) - 1, 2)] + ([xs[-1]] if len(xs) % 2 else [])
    return xs[0]


def _ce(v, x, i, j):
    c = v[j] > v[i]
    v[i], v[j] = jnp.maximum(v[i], v[j]), jnp.minimum(v[i], v[j])
    x[i], x[j] = jnp.where(c, x[j], x[i]), jnp.where(c, x[i], x[j])


def _merge_top16(va, xa, vb, xb, sort_result):
    v, x = list(va), list(xa)
    for j in range(PEER_TOPK - len(vb), PEER_TOPK):
        b = PEER_TOPK - 1 - j
        c = vb[b] > va[j]
        v[j] = jnp.maximum(va[j], vb[b])
        x[j] = jnp.where(c, xb[b], xa[j])
    if sort_result:
        for d in (8, 4, 2, 1):
            for i in range(PEER_TOPK):
                if not i & d:
                    _ce(v, x, i, i + d)
    return v, x


def _any(flag):
    return jnp.max(flag) > 0.0


def _top16_keys(key, work_v, work_x, sv_ref, si_ref):
    for g in range(PEER_NKEYS // PEER_TOPK):
        v = [key(PEER_TOPK * g + i) for i in range(PEER_TOPK)]
        x = [float(PEER_TOPK * g + i) for i in range(PEER_TOPK)]
        for i, j in SORT16:
            _ce(v, x, i, j)
        for i in range(PEER_TOPK):
            work_v[PEER_TOPK * g + i] = v[i]
            work_x[PEER_TOPK * g + i] = x[i]
    for span in (1, 2, 4):
        for g in range(0, PEER_NKEYS // PEER_TOPK, 2 * span):
            a, b = PEER_TOPK * g, PEER_TOPK * (g + span)
            v, x = _merge_top16([work_v[a + i] for i in range(PEER_TOPK)], [work_x[a + i] for i in range(PEER_TOPK)],
                                [work_v[b + i] for i in range(PEER_TOPK)], [work_x[b + i] for i in range(PEER_TOPK)],
                                True)
            for i in range(PEER_TOPK):
                if span == 4:
                    sv_ref[i], si_ref[i] = v[i], x[i]
                else:
                    work_v[a + i], work_x[a + i] = v[i], x[i]
    repeat = _tree(jnp.maximum, [jnp.where(v[r] == v[r + 1], 1.0, 0.0) for r in range(PEER_TOPK - 1)])
    n_ge = _tree(jnp.add, [jnp.where(key(k) >= v[PEER_TOPK - 1], 1.0, 0.0) for k in range(PEER_NKEYS)])

    @pl.when(_any(jnp.maximum(repeat, jnp.where(n_ge > float(PEER_TOPK), 1.0, 0.0))))
    def _():
        for k in range(PEER_NKEYS):
            work_v[k] = key(k)

        def extract(r, carry):
            vals = [work_v[k] for k in range(PEER_NKEYS)]
            m = _tree(jnp.maximum, vals)
            idx = _tree(jnp.minimum, [jnp.where(vals[k] == m, float(k), float(PEER_NKEYS)) for k in range(PEER_NKEYS)])
            for k in range(PEER_NKEYS):
                work_v[k] = jnp.where(idx == float(k), NEG_INF, vals[k])
            sv_ref[r] = m
            si_ref[r] = idx
            return carry

        lax.fori_loop(0, PEER_TOPK, extract, 0)


def _top16_sums(sv_s, si_s, work_v, work_x, rv_s, re_s):
    sv0 = [sv_s[0, r] for r in range(PEER_TOPK)]
    sv1 = [sv_s[1, r] for r in range(PEER_TOPK)]
    e_hi = [si_s[0, r] * float(PEER_NKEYS) for r in range(PEER_TOPK)]
    si1 = [si_s[1, r] for r in range(PEER_TOPK)]
    val = {c: sv0[c[0]] + sv1[c[1]] for c in CANDS}
    eid = {c: e_hi[c[0]] + si1[c[1]] for c in CANDS}
    v, x = [val[c] for c in CAND_LISTS[0]], [eid[c] for c in CAND_LISTS[0]]
    for n, lst in enumerate(CAND_LISTS[1:]):
        v, x = _merge_top16(v, x, [val[c] for c in lst], [eid[c] for c in lst], n < len(CAND_LISTS) - 2)
    for r in range(PEER_TOPK):
        rv_s[r], re_s[r] = v[r], x[r]
    low = _tree(jnp.minimum, v)
    n_ge = _tree(jnp.add, [jnp.where(val[c] >= low, 1.0, 0.0) for c in CANDS])

    @pl.when(_any(jnp.where(n_ge > float(PEER_TOPK), 1.0, 0.0)))
    def _():
        for n, c in enumerate(CANDS):
            work_v[n], work_x[n] = val[c], eid[c]

        def extract(r, carry):
            vals = [work_v[n] for n in range(len(CANDS))]
            m = _tree(jnp.maximum, vals)
            code = [float(c[0] * PEER_TOPK + c[1]) for c in CANDS]
            sel = _tree(jnp.minimum, [jnp.where(vals[n] == m, code[n], 1e9) for n in range(len(CANDS))])
            hit = [sel == code[n] for n in range(len(CANDS))]
            rv_s[r] = m
            re_s[r] = _tree(jnp.add, [jnp.where(hit[n], work_x[n], 0.0) for n in range(len(CANDS))])
            for n in range(len(CANDS)):
                work_v[n] = jnp.where(hit[n], NEG_INF, vals[n])
            return carry

        lax.fori_loop(0, PEER_TOPK, extract, 0)


def _route_kernel(h2_ref, wpt_ref, a_ref, b_ref, g_ref, sc_s, work_v, work_x, sv_s, si_s, rv_s, re_s,
                  pa_s, pb_s, pg_s, *, tt):
    sc_s[...] = _dot_nt(wpt_ref[...], h2_ref[0])

    def lane_tile(c, carry):
        lanes = pl.ds(pl.multiple_of(c * LANES, LANES), LANES)
        for p in range(2):
            def key(k, p=p):
                r0 = (p * PEER_NKEYS + k) * SUBLANES
                return sc_s[r0:r0 + SUBLANES, lanes]
            _top16_keys(key, work_v, work_x, sv_s.at[p], si_s.at[p])
        _top16_sums(sv_s, si_s, work_v, work_x, rv_s, re_s)
        v = [rv_s[r] for r in range(PEER_TOPK)]
        m = _tree(jnp.maximum, v)
        ex = [jnp.exp(vr - m) for vr in v]
        inv = 1.0 / _tree(jnp.add, ex)
        for r in range(PEER_TOPK):
            e = re_s[r]
            i1 = jnp.floor(e * (1.0 / PEER_NKEYS))
            rows = slice(r * SUBLANES, (r + 1) * SUBLANES)
            pa_s[rows, :] = i1
            pb_s[rows, :] = e - i1 * float(PEER_NKEYS)
            pg_s[rows, :] = ex[r] * inv
        toks = pl.ds(pl.multiple_of(c * LANES, LANES), LANES)
        a_ref[0, toks, :] = jnp.transpose(pa_s[...])
        b_ref[0, toks, :] = jnp.transpose(pb_s[...])
        g_ref[0, toks, :] = jnp.transpose(pg_s[...])
        return carry

    lax.fori_loop(0, tt // LANES, lane_tile, 0)


def _route(h2, wpt, tt):
    bsz, seq, _ = h2.shape
    n_pair = PEER_HEADS * PEER_TOPK
    tile = lambda w: pl.BlockSpec((1, tt, w), lambda b, i: (b, i, 0))
    vregs = lambda *lead: pltpu.VMEM(lead + VREG, F32)
    return pl.pallas_call(
        functools.partial(_route_kernel, tt=tt),
        out_shape=tuple(jax.ShapeDtypeStruct((bsz, seq, n_pair), F32) for _ in range(3)),
        grid=(bsz, seq // tt),
        in_specs=[tile(D_MODEL), pl.BlockSpec((KEY_ROWS, D_MODEL), lambda b, i: (0, 0))],
        out_specs=(tile(n_pair), tile(n_pair), tile(n_pair)),
        scratch_shapes=[pltpu.VMEM((KEY_ROWS, tt), F32), vregs(PEER_NKEYS), vregs(PEER_NKEYS),
                        vregs(2, PEER_TOPK), vregs(2, PEER_TOPK), vregs(PEER_TOPK), vregs(PEER_TOPK),
                        pltpu.VMEM((n_pair, LANES), F32), pltpu.VMEM((n_pair, LANES), F32),
                        pltpu.VMEM((n_pair, LANES), F32)],
        compiler_params=pltpu.CompilerParams(dimension_semantics=("arbitrary", "arbitrary"),
                                             vmem_limit_bytes=VMEM_LIMIT),
        name="route",
    )(h2, wpt)


def _peer_kernel(h2_ref, x1_ref, mod_ref, a_ref, b_ref, g_ref, ut_ref, v_ref, nfg_ref, y_ref, gate_s, acc_s,
                 *, tt, eb, ne):
    j = pl.program_id(2)
    n_grp = eb // PEER_NKEYS
    steps_per_half = ne // 2

    @pl.when(j == 0)
    def _():
        acc_s[...] = jnp.zeros_like(acc_s)
        sub = lax.broadcasted_iota(jnp.int32, (PEER_NKEYS, PEER_NKEYS), 0).astype(F32).astype(BF16)
        zero = jnp.zeros((PEER_NKEYS, PEER_NKEYS), BF16)
        one = jnp.ones((PEER_NKEYS, PEER_NKEYS), BF16)

        def onehots(t):
            bc = lambda ref: jnp.broadcast_to(ref[0, pl.ds(t, 1), :], (PEER_NKEYS, PEER_NKEYS)).astype(BF16)
            at = jnp.where(sub == bc(a_ref), one, zero)
            cbt = jnp.where(sub == bc(b_ref), bc(g_ref), zero)
            return at, cbt

        def build(p, carry):
            for u in range(GATE_UNROLL):
                t0 = (p * GATE_UNROLL + u) * 2
                at0, cbt0 = onehots(t0)
                at1, cbt1 = onehots(t0 + 1)
                lhs = jnp.concatenate([at0, at1], axis=1)
                rhs = jnp.concatenate([jnp.concatenate([cbt0, zero], axis=1),
                                       jnp.concatenate([zero, cbt1], axis=1)], axis=0)
                tiles = _dot_nt(lhs, rhs)
                for k in range(2):
                    lo = tiles[0:GATE_HALF, k * PEER_NKEYS:(k + 1) * PEER_NKEYS]
                    hi = tiles[GATE_HALF:, k * PEER_NKEYS:(k + 1) * PEER_NKEYS]
                    r0 = pl.multiple_of((t0 + k) * GATE_PITCH, SUBLANES)
                    gate_s[pl.ds(r0, GATE_HALF), :] = (_bf16_bits(lo) >> 16) | _bf16_bits(hi)
            return carry

        lax.fori_loop(0, tt // (2 * GATE_UNROLL), build, 0)

    def expert_block(half):
        acc = acc_s[...]
        for sub in range(eb // EXPERT_SUB):
            cols = slice(sub * EXPERT_SUB, (sub + 1) * EXPERT_SUB)
            s = _dot(h2_ref[0], ut_ref[:, cols])
            parts = []
            for gi in range(EXPERT_SUB // PEER_NKEYS):
                g_abs = sub * (EXPERT_SUB // PEER_NKEYS) + gi
                r = (j - half * steps_per_half) * n_grp + g_abs
                packed = gate_s[pl.ds(r, tt, stride=GATE_PITCH), :]
                word = (packed & jnp.uint32(0xFFFF0000)) if half else (packed << 16)
                gate = lax.bitcast_convert_type(word, F32)
                parts.append((gate * _gelu_tanh(s[:, gi * PEER_NKEYS:(gi + 1) * PEER_NKEYS])).astype(BF16))
            acc = acc + _dot(jnp.concatenate(parts, axis=1), v_ref[cols, :])
        acc_s[...] = acc

    @pl.when(j < steps_per_half)
    def _():
        expert_block(0)

    @pl.when(j >= steps_per_half)
    def _():
        expert_block(1)

    @pl.when(j == ne - 1)
    def _():
        gt2 = mod_ref[0][5:6]
        x2 = x1_ref[0] + gt2 * acc_s[...]
        y_ref[0] = (x2 * _rms(x2)) * nfg_ref[...]


def _peer(h2, x1, mod, a_idx, b_idx, gates, ut, v, nfg, tt, eb):
    bsz, seq, _ = h2.shape
    n_exp = v.shape[0]
    ne = n_exp // eb
    assert ne % 2 == 0 and (GATE_HALF * PEER_NKEYS) % eb == 0 and tt % (2 * GATE_UNROLL) == 0
    n_pair = PEER_HEADS * PEER_TOPK
    tile = lambda w: pl.BlockSpec((1, tt, w), lambda b, i, j: (b, i, 0))
    return pl.pallas_call(
        functools.partial(_peer_kernel, tt=tt, eb=eb, ne=ne),
        out_shape=jax.ShapeDtypeStruct((bsz, seq, D_MODEL), F32),
        grid=(bsz, seq // tt, ne),
        in_specs=[
            tile(D_MODEL), tile(D_MODEL),
            pl.BlockSpec((1, SUBLANES, D_MODEL), lambda b, i, j: (b, 0, 0)),
            tile(n_pair), tile(n_pair), tile(n_pair),
            pl.BlockSpec((D_MODEL, eb), lambda b, i, j: (0, j)),
            pl.BlockSpec((eb, D_MODEL), lambda b, i, j: (j, 0)),
            pl.BlockSpec((1, D_MODEL), lambda b, i, j: (0, 0)),
        ],
        out_specs=tile(D_MODEL),
        scratch_shapes=[pltpu.VMEM((tt * GATE_PITCH, PEER_NKEYS), jnp.uint32), pltpu.VMEM((tt, D_MODEL), F32)],
        compiler_params=pltpu.CompilerParams(dimension_semantics=("arbitrary", "arbitrary", "arbitrary"),
                                             vmem_limit_bytes=VMEM_LIMIT),
        name="peer",
    )(h2, x1, mod, a_idx, b_idx, gates, ut, v, nfg)


def _prep_weights(norm1_g, norm2_g, w_in, w_dec_f, b_dec_f, w_dec_b, b_dec_b, gla_norm_g, conv_w, conv_norm_g,
                  w_out, peer_wq, peer_subkeys, peer_u, peer_v, normf_g):
    w = w_in[0]
    q, k, v, g, alf, alb, cb, cc, ch = jnp.split(w, (256, 512, 1024, 1536, 1552, 1568, 2080, 2592), axis=-1)
    pad = jnp.zeros((D_MODEL, C_END - C_AL - 2 * GLA_RANK), F32)
    win = jnp.concatenate([q, k, v, g, cb, cc, ch, alf, alb, pad], axis=-1).astype(BF16)
    wdec = jnp.zeros((LANES, 2 * GLA_KW), F32)
    wdec = wdec.at[0:GLA_RANK, 0:GLA_KW].set(w_dec_f[0]).at[GLA_RANK:2 * GLA_RANK, GLA_KW:].set(w_dec_b[0])
    bdec = jnp.concatenate([b_dec_f[0], b_dec_b[0]])[None, :]
    grp = jnp.arange(CONV_WIDTH) // CONV_GDIM
    mgrp = (grp[:, None] == grp[None, :]).astype(BF16)
    wp = _keyproj(peer_wq[0], peer_subkeys[0].reshape(2 * PEER_HEADS, PEER_NKEYS, PEER_DQH))
    wpt = wp.reshape(PEER_HEADS, 2, PEER_NKEYS, D_MODEL).transpose(1, 2, 0, 3).reshape(KEY_ROWS, D_MODEL)
    return dict(
        n1g=norm1_g[0][None, :], n2g=norm2_g[0][None, :], win=win, wdec=wdec, bdec=bdec,
        convw=conv_w[0], cng=conv_norm_g[0][None, :], mgrp=mgrp, gng=gla_norm_g[0][None, :],
        wout=w_out[0].astype(BF16), wpt=wpt.astype(BF16),
        ut=jnp.transpose(peer_u[0]).astype(BF16), v=peer_v[0].astype(BF16), nfg=normf_g[None, :],
    )


def _trunk(x, mod, w, tt, rtt, ptt, eb):
    qk, v, lab, of, gy = _mix1(x, mod, w["n1g"], w["win"], w["wdec"], w["bdec"], w["convw"], w["cng"], w["mgrp"], tt)
    x1, h2 = _mix2(x, mod, qk, v, lab, of, gy, w["gng"], w["wout"], w["n2g"], tt)
    a_idx, b_idx, gates = _route(h2, w["wpt"], rtt)
    return _peer(h2, x1, mod, a_idx, b_idx, gates, w["ut"], w["v"], w["nfg"], ptt, eb)


def kernel(x_prompt, x_sample, c_prompt, c_sample, norm1_g, norm2_g, w_ada, b_ada, w_in, w_dec_f, b_dec_f,
           w_dec_b, b_dec_b, gla_norm_g, conv_w, conv_norm_g, w_out, peer_wq, peer_subkeys, peer_u, peer_v,
           normf_g):
    w = _prep_weights(norm1_g, norm2_g, w_in, w_dec_f, b_dec_f, w_dec_b, b_dec_b, gla_norm_g, conv_w,
                      conv_norm_g, w_out, peer_wq, peer_subkeys, peer_u, peer_v, normf_g)
    nb_p, nb_s = c_prompt.shape[0], c_sample.shape[0]
    c_all = jnp.concatenate([c_prompt, c_sample, jnp.zeros((SUBLANES - nb_p - nb_s, D_MODEL), F32)], axis=0)
    ada = _ada(c_all, w_ada[0], b_ada[0][None, :])
    mod = jnp.pad(ada.reshape(SUBLANES, N_ADA, D_MODEL), ((0, 0), (0, SUBLANES - N_ADA), (0, 0)))
    y_prompt = _trunk(x_prompt, mod[:nb_p], w, TOKEN_TILE, ROUTE_TILE, PEER_TOKEN_TILE, EXPERT_BLOCK)
    y_sample = _trunk(x_sample, mod[nb_p:nb_p + nb_s], w, TOKEN_TILE, ROUTE_TILE, PEER_TOKEN_TILE, EXPERT_BLOCK)
    return (y_prompt, y_sample)
```

```python
import functools

import jax
import jax.numpy as jnp
from jax import lax
from jax.experimental import pallas as pl
from jax.experimental.pallas import tpu as pltpu

F32 = jnp.float32
BF16 = jnp.bfloat16

D_MODEL = 1024
GLA_HEADS = 4
GLA_DV = 128
GLA_DK = 64
GLA_KW = GLA_HEADS * GLA_DK
GLA_WIDTH = GLA_HEADS * GLA_DV
GLA_RANK = 16
GLA_GATE_NORM = 16.0
GLA_CHUNK = 64
CONV_WIDTH = 512
CONV_GDIM = 64
PEER_HEADS = 8
PEER_NKEYS = 128
PEER_DQH = 128
PEER_TOPK = 16
N_ADA = 6
EPS = 1e-6

LANES = 128
SUBLANES = 8
VMEM_LIMIT = 56 * 1024 * 1024

C_QK, C_V, C_G, C_CB, C_CC, C_CH, C_AL, C_END = 0, 512, 1024, 1536, 2048, 2560, 3072, 3200

TOKEN_TILE = 256
MIX_ROWS = 1
ROUTE_TILE = 512
PEER_TOKEN_TILE = 512
EXPERT_BLOCK = 2048
EXPERT_SUB = 2048
GATE_HALF = PEER_NKEYS // 2
GATE_PITCH = GATE_HALF + SUBLANES
GATE_UNROLL = 16


def _dot(a, b):
    return jnp.dot(a, b, preferred_element_type=F32)


def _dot_nt(a, b):
    return lax.dot_general(a, b, (((1,), (1,)), ((), ())), preferred_element_type=F32)


def _dot_tn(a, b):
    return lax.dot_general(a, b, (((0,), (0,)), ((), ())), preferred_element_type=F32)


def _split2(x):
    hi = x.astype(BF16)
    lo = (x - hi.astype(F32)).astype(BF16)
    return hi, lo


def _split3(x):
    hi = x.astype(BF16)
    r = x - hi.astype(F32)
    mid = r.astype(BF16)
    lo = (r - mid.astype(F32)).astype(BF16)
    return hi, mid, lo


def _dot_f32(a, b):
    a_hi, a_lo = _split2(a)
    b_hi, b_lo = _split2(b)
    return _dot(a_hi, b_hi) + _dot(a_hi, b_lo) + _dot(a_lo, b_hi)


def _bf16_bits(x):
    return lax.bitcast_convert_type(x.astype(BF16).astype(F32), jnp.uint32)


def _rms(x):
    return lax.rsqrt(jnp.mean(x * x, axis=-1, keepdims=True) + EPS)


def _sigmoid(x):
    return 1.0 / (1.0 + jnp.exp(-x))


def _log_sigmoid(x):
    return jnp.minimum(x, 0.0) - jnp.log1p(jnp.exp(-jnp.abs(x)))


def _gelu_tanh(x):
    return 0.5 * x * (1.0 + jnp.tanh(0.7978845608028654 * (x + 0.044715 * (x * x * x))))


def _ada_kernel(c_ref, w_ref, b_ref, o_ref):
    c = c_ref[...]
    o_ref[...] = _dot_f32(c * _sigmoid(c), w_ref[...]) + b_ref[...]


def _ada(c_pad, w_ada, b_ada):
    n_col = N_ADA * D_MODEL
    blk = 1536
    return pl.pallas_call(
        _ada_kernel,
        out_shape=jax.ShapeDtypeStruct((SUBLANES, n_col), F32),
        grid=(n_col // blk,),
        in_specs=[
            pl.BlockSpec((SUBLANES, D_MODEL), lambda j: (0, 0)),
            pl.BlockSpec((D_MODEL, blk), lambda j: (0, j)),
            pl.BlockSpec((1, blk), lambda j: (0, j)),
        ],
        out_specs=pl.BlockSpec((SUBLANES, blk), lambda j: (0, j)),
        compiler_params=pltpu.CompilerParams(dimension_semantics=("arbitrary",), vmem_limit_bytes=VMEM_LIMIT),
        name="ada",
    )(c_pad, w_ada, b_ada)


def _gla_tile(q, k, v, la, st_ref, forward):
    tt = q.shape[0]
    n_chunk = tt // GLA_CHUNK
    row = lax.broadcasted_iota(jnp.int32, (tt, tt), 0)
    col = lax.broadcasted_iota(jnp.int32, (tt, tt), 1)
    ordered = (row >= col) if forward else (row <= col)
    causal = ordered & (row // GLA_CHUNK == col // GLA_CHUNK)
    tri = jnp.where(causal, 1.0, 0.0).astype(BF16)
    la_hi, la_mid, la_lo = _split3(la)
    b = _dot(tri, la_hi) + _dot(tri, la_mid) + _dot(tri, la_lo)
    end_rows = [ci * GLA_CHUNK + (GLA_CHUNK - 1 if forward else 0) for ci in range(n_chunk)]
    b_end = jnp.concatenate([jnp.broadcast_to(b[r:r + 1, :], (GLA_CHUNK, GLA_KW)) for r in end_rows], axis=0)
    q_dec = q * (jnp.exp(b) * (GLA_DK ** -0.5))
    k_dec = k * jnp.exp(-b)
    k_end = k * jnp.exp(b_end - b)
    decay = [jnp.exp(b[r:r + 1, :]) for r in end_rows]
    lane = lax.broadcasted_iota(jnp.int32, (1, LANES), 1)
    scan = range(n_chunk) if forward else range(n_chunk - 1, -1, -1)
    outs = []
    for head in range(GLA_HEADS):
        pair, half = head // 2, head % 2
        sl = slice(pair * LANES, (pair + 1) * LANES)
        in_head = (lane // GLA_DK) == half
        qd = jnp.where(in_head, q_dec[:, sl], 0.0).astype(BF16)
        kd = k_dec[:, sl].astype(BF16)
        ke = k_end[:, sl].astype(BF16)
        v_h = v[:, head * GLA_DV:(head + 1) * GLA_DV].astype(BF16)
        scores = jnp.where(causal, _dot_nt(qd, kd), 0.0)
        o_intra = _dot(scores.astype(BF16), v_h)
        st = st_ref[head]
        parts = [None] * n_chunk
        for ci in scan:
            rows = slice(ci * GLA_CHUNK, (ci + 1) * GLA_CHUNK)
            parts[ci] = o_intra[rows] + _dot_nt(qd[rows], st.astype(BF16))
            st = st * decay[ci][:, sl] + _dot_tn(v_h[rows], ke[rows])
        st_ref[head] = st
        outs.append(jnp.concatenate(parts, axis=0))
    return jnp.concatenate(outs, axis=-1)


def _mix1_kernel(x_ref, xp_ref, xn_ref, mod_ref, n1g_ref, win_ref, wdec_ref, bdec_ref, convw_ref, cng_ref,
                 mgrp_ref, qk_ref, v_ref, lab_ref, of_ref, gy_ref, st_s, *, tt, nt, nb):
    i = pl.program_id(1)

    @pl.when(i == 0)
    def _():
        st_s[...] = jnp.zeros_like(st_s)

    n1g = n1g_ref[...]
    cw = convw_ref[...]
    row = lax.broadcasted_iota(jnp.int32, (tt, 1), 0)
    for bi in range(nb):
        mod = mod_ref[bi]
        sh1, sc1 = mod[0:1], mod[1:2]

        def norm_mod(x):
            return ((x * _rms(x)) * n1g) * (1.0 + sc1) + sh1

        h = norm_mod(x_ref[bi]).astype(BF16)
        qk_ref[bi] = _dot(h, win_ref[:, C_QK:C_V])
        v_ref[bi] = _dot(h, win_ref[:, C_V:C_G])
        gy_ref[bi, :, 0:GLA_WIDTH] = _dot(h, win_ref[:, C_G:C_CB])

        hh = norm_mod(jnp.concatenate([xp_ref[bi], xn_ref[bi]], axis=0)).astype(BF16)
        h_ext = jnp.concatenate([h, hh], axis=0)
        z_ext = _dot(h_ext, win_ref[:, C_CC:C_CH]) * _dot(h_ext, win_ref[:, C_CH:C_AL])
        z = z_ext[0:tt]
        z_prev = jnp.where(i > 0, z_ext[tt + SUBLANES - 1:tt + SUBLANES, :], 0.0)
        z_next = jnp.where(i < nt - 1, z_ext[tt + SUBLANES:tt + SUBLANES + 1, :], 0.0)
        z_m1 = jnp.where(row == 0, z_prev, pltpu.roll(z, 1, axis=0))
        z_p1 = jnp.where(row == tt - 1, z_next, pltpu.roll(z, tt - 1, axis=0))
        conv = cw[0:1] * z_m1 + cw[1:2] * z + cw[2:3] * z_p1
        yc = _dot(h, win_ref[:, C_CB:C_CC]) * conv
        sq_hi, sq_lo = _split2(yc * yc)
        ss = _dot(sq_hi, mgrp_ref[...]) + _dot(sq_lo, mgrp_ref[...])
        gy_ref[bi, :, GLA_WIDTH:] = (yc * lax.rsqrt(ss * (1.0 / CONV_GDIM) + EPS)) * cng_ref[...]

        zd = _dot_f32(_dot(h, win_ref[:, C_AL:C_END]), wdec_ref[...]) + bdec_ref[...]
        la = _log_sigmoid(zd) * (1.0 / GLA_GATE_NORM)
        lab_ref[bi] = la[:, GLA_KW:]
        of_ref[bi] = _gla_tile(qk_ref[bi, :, 0:GLA_KW], qk_ref[bi, :, GLA_KW:], v_ref[bi], la[:, 0:GLA_KW],
                               st_s.at[bi], True)


def _mix1(x, mod, n1g, win, wdec, bdec, convw, cng, mgrp, tt, nb):
    bsz, seq, _ = x.shape
    nt = seq // tt
    hb = tt // SUBLANES
    last_hb = seq // SUBLANES - 1
    const = lambda shape: pl.BlockSpec(shape, lambda b, i: tuple(0 for _ in shape))
    tile = lambda w: pl.BlockSpec((nb, tt, w), lambda b, i: (b, i, 0))
    out_w = (2 * GLA_KW, GLA_WIDTH, GLA_KW, GLA_WIDTH, GLA_WIDTH + CONV_WIDTH)
    return pl.pallas_call(
        functools.partial(_mix1_kernel, tt=tt, nt=nt, nb=nb),
        out_shape=tuple(jax.ShapeDtypeStruct((bsz, seq, w), F32) for w in out_w),
        grid=(bsz // nb, nt),
        in_specs=[
            tile(D_MODEL),
            pl.BlockSpec((nb, SUBLANES, D_MODEL), lambda b, i: (b, jnp.maximum(i * hb - 1, 0), 0)),
            pl.BlockSpec((nb, SUBLANES, D_MODEL), lambda b, i: (b, jnp.minimum((i + 1) * hb, last_hb), 0)),
            pl.BlockSpec((nb, SUBLANES, D_MODEL), lambda b, i: (b, 0, 0)),
            const((1, D_MODEL)),
            const((D_MODEL, C_END)),
            const((LANES, 2 * GLA_KW)),
            const((1, 2 * GLA_KW)),
            const((3, CONV_WIDTH)),
            const((1, CONV_WIDTH)),
            const((CONV_WIDTH, CONV_WIDTH)),
        ],
        out_specs=tuple(tile(w) for w in out_w),
        scratch_shapes=[pltpu.VMEM((nb, GLA_HEADS, GLA_DV, LANES), F32)],
        compiler_params=pltpu.CompilerParams(dimension_semantics=("arbitrary", "arbitrary"),
                                             vmem_limit_bytes=VMEM_LIMIT),
        name="mix1",
    )(x, x, x, mod, n1g, win, wdec, bdec, convw, cng, mgrp)


def _mix2_kernel(x_ref, mod_ref, qk_ref, v_ref, lab_ref, of_ref, gy_ref, gng_ref, wout_ref, n2g_ref,
                 x1_ref, h2_ref, st_s, *, nb):
    i = pl.program_id(1)

    @pl.when(i == 0)
    def _():
        st_s[...] = jnp.zeros_like(st_s)

    gng = gng_ref[...]
    for bi in range(nb):
        mod = mod_ref[bi]
        gt1, sh2, sc2 = mod[2:3], mod[3:4], mod[4:5]
        o = of_ref[bi] + _gla_tile(qk_ref[bi, :, 0:GLA_KW], qk_ref[bi, :, GLA_KW:], v_ref[bi], lab_ref[bi],
                                   st_s.at[bi], False)
        y = None
        for head in range(GLA_HEADS):
            sl = slice(head * GLA_DV, (head + 1) * GLA_DV)
            oh = o[:, sl]
            g = gy_ref[bi, :, sl]
            yh = ((oh * _rms(oh)) * gng) * (g * _sigmoid(g))
            part = _dot(yh.astype(BF16), wout_ref[sl, :])
            y = part if y is None else y + part
        y = y + _dot(gy_ref[bi, :, GLA_WIDTH:].astype(BF16), wout_ref[GLA_WIDTH:, :])
        x1 = x_ref[bi] + gt1 * y
        x1_ref[bi] = x1
        h2_ref[bi] = (((x1 * _rms(x1)) * n2g_ref[...]) * (1.0 + sc2) + sh2).astype(BF16)


def _mix2(x, mod, qk, v, lab, of, gy, gng, wout, n2g, tt, nb):
    bsz, seq, _ = x.shape
    nt = seq // tt
    const = lambda shape: pl.BlockSpec(shape, lambda b, i: tuple(0 for _ in shape))
    tile = lambda w: pl.BlockSpec((nb, tt, w), lambda b, i: (b, nt - 1 - i, 0))
    return pl.pallas_call(
        functools.partial(_mix2_kernel, nb=nb),
        out_shape=(jax.ShapeDtypeStruct((bsz, seq, D_MODEL), F32), jax.ShapeDtypeStruct((bsz, seq, D_MODEL), BF16)),
        grid=(bsz // nb, nt),
        in_specs=[
            tile(D_MODEL),
            pl.BlockSpec((nb, SUBLANES, D_MODEL), lambda b, i: (b, 0, 0)),
            tile(2 * GLA_KW), tile(GLA_WIDTH), tile(GLA_KW), tile(GLA_WIDTH), tile(GLA_WIDTH + CONV_WIDTH),
            const((1, GLA_DV)),
            const((D_MODEL, D_MODEL)),
            const((1, D_MODEL)),
        ],
        out_specs=(tile(D_MODEL), tile(D_MODEL)),
        scratch_shapes=[pltpu.VMEM((nb, GLA_HEADS, GLA_DV, LANES), F32)],
        compiler_params=pltpu.CompilerParams(dimension_semantics=("arbitrary", "arbitrary"),
                                             vmem_limit_bytes=VMEM_LIMIT),
        name="mix2",
    )(x, mod, qk, v, lab, of, gy, gng, wout, n2g)


NEG_INF = float("-inf")
KEY_ROWS = 2 * PEER_NKEYS * PEER_HEADS
VREG = (SUBLANES, LANES)


def _keyproj_kernel(wq_ref, sk_ref, o_ref):
    w_hi, w_lo = _split2(wq_ref[...])
    k_hi, k_lo = _split2(sk_ref[0])
    o_ref[0] = _dot_nt(k_hi, w_hi) + _dot_nt(k_hi, w_lo) + _dot_nt(k_lo, w_hi)


def _keyproj(wq, subkeys):
    n_hp = 2 * PEER_HEADS
    return pl.pallas_call(
        _keyproj_kernel,
        out_shape=jax.ShapeDtypeStruct((n_hp, PEER_NKEYS, D_MODEL), F32),
        grid=(n_hp,),
        in_specs=[pl.BlockSpec((D_MODEL, PEER_DQH), lambda j: (0, j)),
                  pl.BlockSpec((1, PEER_NKEYS, PEER_DQH), lambda j: (j, 0, 0))],
        out_specs=pl.BlockSpec((1, PEER_NKEYS, D_MODEL), lambda j: (j, 0, 0)),
        compiler_params=pltpu.CompilerParams(dimension_semantics=("arbitrary",), vmem_limit_bytes=VMEM_LIMIT),
        name="keyproj",
    )(wq, subkeys)


def _sort16_network():
    pairs = []

    def merge(lo, hi, r):
        step = r * 2
        if step < hi - lo:
            merge(lo, hi, step)
            merge(lo + r, hi, step)
            pairs.extend((i, i + r) for i in range(lo + r, hi - r, step))
        else:
            pairs.append((lo, lo + r))

    def sort(lo, hi):
        if hi - lo >= 1:
            mid = lo + (hi - lo) // 2
            sort(lo, mid)
            sort(mid + 1, hi)
            merge(lo, hi, 1)

    sort(0, PEER_TOPK - 1)
    return tuple(pairs)


SORT16 = _sort16_network()
CAND_LISTS = ([[(0, k2) for k2 in range(PEER_TOPK)]]
              + [[(k1, k2) for k2 in range(PEER_TOPK // (k1 + 1))] for k1 in range(1, 8)]
              + [[(k1, 0) for k1 in range(8, PEER_TOPK)]])
CANDS = [c for lst in CAND_LISTS for c in lst]


def _tree(op, xs):
    xs = list(xs)
    while len(xs) > 1:
        xs = [op(xs[i], xs[i + 1]) for i in range(0, len(xs) - 1, 2)] + ([xs[-1]] if len(xs) % 2 else [])
    return xs[0]


def _ce(v, x, i, j):
    c = v[j] > v[i]
    v[i], v[j] = jnp.maximum(v[i], v[j]), jnp.minimum(v[i], v[j])
    x[i], x[j] = jnp.where(c, x[j], x[i]), jnp.where(c, x[i], x[j])


def _merge_top16(va, xa, vb, xb, sort_result):
    v, x = list(va), list(xa)
    for j in range(PEER_TOPK - len(vb), PEER_TOPK):
        b = PEER_TOPK - 1 - j
        c = vb[b] > va[j]
        v[j] = jnp.maximum(va[j], vb[b])
        x[j] = jnp.where(c, xb[b], xa[j])
    if sort_result:
        for d in (8, 4, 2, 1):
            for i in range(PEER_TOPK):
                if not i & d:
                    _ce(v, x, i, i + d)
    return v, x


def _any(flag):
    return jnp.max(flag) > 0.0


def _top16_keys(key, work_v, work_x, sv_ref, si_ref):
    for g in range(PEER_NKEYS // PEER_TOPK):
        v = [key(PEER_TOPK * g + i) for i in range(PEER_TOPK)]
        x = [float(PEER_TOPK * g + i) for i in range(PEER_TOPK)]
        for i, j in SORT16:
            _ce(v, x, i, j)
        for i in range(PEER_TOPK):
            work_v[PEER_TOPK * g + i] = v[i]
            work_x[PEER_TOPK * g + i] = x[i]
    for span in (1, 2, 4):
        for g in range(0, PEER_NKEYS // PEER_TOPK, 2 * span):
            a, b = PEER_TOPK * g, PEER_TOPK * (g + span)
            v, x = _merge_top16([work_v[a + i] for i in range(PEER_TOPK)], [work_x[a + i] for i in range(PEER_TOPK)],
                                [work_v[b + i] for i in range(PEER_TOPK)], [work_x[b + i] for i in range(PEER_TOPK)],
                                True)
            for i in range(PEER_TOPK):
                if span == 4:
                    sv_ref[i], si_ref[i] = v[i], x[i]
                else:
                    work_v[a + i], work_x[a + i] = v[i], x[i]
    repeat = _tree(jnp.maximum, [jnp.where(v[r] == v[r + 1], 1.0, 0.0) for r in range(PEER_TOPK - 1)])
    n_ge = _tree(jnp.add, [jnp.where(key(k) >= v[PEER_TOPK - 1], 1.0, 0.0) for k in range(PEER_NKEYS)])

    @pl.when(_any(jnp.maximum(repeat, jnp.where(n_ge > float(PEER_TOPK), 1.0, 0.0))))
    def _():
        for k in range(PEER_NKEYS):
            work_v[k] = key(k)

        def extract(r, carry):
            vals = [work_v[k] for k in range(PEER_NKEYS)]
            m = _tree(jnp.maximum, vals)
            idx = _tree(jnp.minimum, [jnp.where(vals[k] == m, float(k), float(PEER_NKEYS)) for k in range(PEER_NKEYS)])
            for k in range(PEER_NKEYS):
                work_v[k] = jnp.where(idx == float(k), NEG_INF, vals[k])
            sv_ref[r] = m
            si_ref[r] = idx
            return carry

        lax.fori_loop(0, PEER_TOPK, extract, 0)


def _top16_sums(sv_s, si_s, work_v, work_x, rv_s, re_s):
    sv0 = [sv_s[0, r] for r in range(PEER_TOPK)]
    sv1 = [sv_s[1, r] for r in range(PEER_TOPK)]
    e_hi = [si_s[0, r] * float(PEER_NKEYS) for r in range(PEER_TOPK)]
    si1 = [si_s[1, r] for r in range(PEER_TOPK)]
    val = {c: sv0[c[0]] + sv1[c[1]] for c in CANDS}
    eid = {c: e_hi[c[0]] + si1[c[1]] for c in CANDS}
    v, x = [val[c] for c in CAND_LISTS[0]], [eid[c] for c in CAND_LISTS[0]]
    for n, lst in enumerate(CAND_LISTS[1:]):
        v, x = _merge_top16(v, x, [val[c] for c in lst], [eid[c] for c in lst], n < len(CAND_LISTS) - 2)
    for r in range(PEER_TOPK):
        rv_s[r], re_s[r] = v[r], x[r]
    low = _tree(jnp.minimum, v)
    n_ge = _tree(jnp.add, [jnp.where(val[c] >= low, 1.0, 0.0) for c in CANDS])

    @pl.when(_any(jnp.where(n_ge > float(PEER_TOPK), 1.0, 0.0)))
    def _():
        for n, c in enumerate(CANDS):
            work_v[n], work_x[n] = val[c], eid[c]

        def extract(r, carry):
            vals = [work_v[n] for n in range(len(CANDS))]
            m = _tree(jnp.maximum, vals)
            code = [float(c[0] * PEER_TOPK + c[1]) for c in CANDS]
            sel = _tree(jnp.minimum, [jnp.where(vals[n] == m, code[n], 1e9) for n in range(len(CANDS))])
            hit = [sel == code[n] for n in range(len(CANDS))]
            rv_s[r] = m
            re_s[r] = _tree(jnp.add, [jnp.where(hit[n], work_x[n], 0.0) for n in range(len(CANDS))])
            for n in range(len(CANDS)):
                work_v[n] = jnp.where(hit[n], NEG_INF, vals[n])
            return carry

        lax.fori_loop(0, PEER_TOPK, extract, 0)


def _route_kernel(h2_ref, wpt_ref, a_ref, b_ref, g_ref, sc_s, work_v, work_x, sv_s, si_s, rv_s, re_s,
                  pa_s, pb_s, pg_s, *, tt):
    sc_s[...] = _dot_nt(wpt_ref[...], h2_ref[0])

    def lane_tile(c, carry):
        lanes = pl.ds(pl.multiple_of(c * LANES, LANES), LANES)
        for p in range(2):
            def key(k, p=p):
                r0 = (p * PEER_NKEYS + k) * SUBLANES
                return sc_s[r0:r0 + SUBLANES, lanes]
            _top16_keys(key, work_v, work_x, sv_s.at[p], si_s.at[p])
        _top16_sums(sv_s, si_s, work_v, work_x, rv_s, re_s)
        v = [rv_s[r] for r in range(PEER_TOPK)]
        m = _tree(jnp.maximum, v)
        ex = [jnp.exp(vr - m) for vr in v]
        inv = 1.0 / _tree(jnp.add, ex)
        for r in range(PEER_TOPK):
            e = re_s[r]
            i1 = jnp.floor(e * (1.0 / PEER_NKEYS))
            rows = slice(r * SUBLANES, (r + 1) * SUBLANES)
            pa_s[rows, :] = i1
            pb_s[rows, :] = e - i1 * float(PEER_NKEYS)
            pg_s[rows, :] = ex[r] * inv
        toks = pl.ds(pl.multiple_of(c * LANES, LANES), LANES)
        a_ref[0, toks, :] = jnp.transpose(pa_s[...])
        b_ref[0, toks, :] = jnp.transpose(pb_s[...])
        g_ref[0, toks, :] = jnp.transpose(pg_s[...])
        return carry

    lax.fori_loop(0, tt // LANES, lane_tile, 0)


def _route(h2, wpt, tt):
    bsz, seq, _ = h2.shape
    n_pair = PEER_HEADS * PEER_TOPK
    tile = lambda w: pl.BlockSpec((1, tt, w), lambda b, i: (b, i, 0))
    vregs = lambda *lead: pltpu.VMEM(lead + VREG, F32)
    return pl.pallas_call(
        functools.partial(_route_kernel, tt=tt),
        out_shape=tuple(jax.ShapeDtypeStruct((bsz, seq, n_pair), F32) for _ in range(3)),
        grid=(bsz, seq // tt),
        in_specs=[tile(D_MODEL), pl.BlockSpec((KEY_ROWS, D_MODEL), lambda b, i: (0, 0))],
        out_specs=(tile(n_pair), tile(n_pair), tile(n_pair)),
        scratch_shapes=[pltpu.VMEM((KEY_ROWS, tt), F32), vregs(PEER_NKEYS), vregs(PEER_NKEYS),
                        vregs(2, PEER_TOPK), vregs(2, PEER_TOPK), vregs(PEER_TOPK), vregs(PEER_TOPK),
                        pltpu.VMEM((n_pair, LANES), F32), pltpu.VMEM((n_pair, LANES), F32),
                        pltpu.VMEM((n_pair, LANES), F32)],
        compiler_params=pltpu.CompilerParams(dimension_semantics=("arbitrary", "arbitrary"),
                                             vmem_limit_bytes=VMEM_LIMIT),
        name="route",
    )(h2, wpt)


def _peer_kernel(h2_ref, x1_ref, mod_ref, a_ref, b_ref, g_ref, ut_ref, v_ref, nfg_ref, y_ref, gate_s, acc_s,
                 *, tt, eb, ne):
    j = pl.program_id(2)
    n_grp = eb // PEER_NKEYS
    steps_per_half = ne // 2

    @pl.when(j == 0)
    def _():
        acc_s[...] = jnp.zeros_like(acc_s)
        sub = lax.broadcasted_iota(jnp.int32, (PEER_NKEYS, PEER_NKEYS), 0).astype(F32).astype(BF16)
        zero = jnp.zeros((PEER_NKEYS, PEER_NKEYS), BF16)
        one = jnp.ones((PEER_NKEYS, PEER_NKEYS), BF16)

        def onehots(t):
            bc = lambda ref: jnp.broadcast_to(ref[0, pl.ds(t, 1), :], (PEER_NKEYS, PEER_NKEYS)).astype(BF16)
            at = jnp.where(sub == bc(a_ref), one, zero)
            cbt = jnp.where(sub == bc(b_ref), bc(g_ref), zero)
            return at, cbt

        def build(p, carry):
            for u in range(GATE_UNROLL):
                t0 = (p * GATE_UNROLL + u) * 2
                at0, cbt0 = onehots(t0)
                at1, cbt1 = onehots(t0 + 1)
                lhs = jnp.concatenate([at0, at1], axis=1)
                rhs = jnp.concatenate([jnp.concatenate([cbt0.T, zero], axis=1),
                                       jnp.concatenate([zero, cbt1.T], axis=1)], axis=0)
                tiles = _dot(lhs, rhs)
                for k in range(2):
                    lo = tiles[0:GATE_HALF, k * PEER_NKEYS:(k + 1) * PEER_NKEYS]
                    hi = tiles[GATE_HALF:, k * PEER_NKEYS:(k + 1) * PEER_NKEYS]
                    r0 = pl.multiple_of((t0 + k) * GATE_PITCH, SUBLANES)
                    gate_s[pl.ds(r0, GATE_HALF), :] = (_bf16_bits(lo) >> 16) | _bf16_bits(hi)
            return carry

        lax.fori_loop(0, tt // (2 * GATE_UNROLL), build, 0)

    def expert_block(half):
        acc = acc_s[...]
        for sub in range(eb // EXPERT_SUB):
            cols = slice(sub * EXPERT_SUB, (sub + 1) * EXPERT_SUB)
            s = _dot(h2_ref[0], ut_ref[:, cols])
            parts = []
            for gi in range(EXPERT_SUB // PEER_NKEYS):
                g_abs = sub * (EXPERT_SUB // PEER_NKEYS) + gi
                r = (j - half * steps_per_half) * n_grp + g_abs
                packed = gate_s[pl.ds(r, tt, stride=GATE_PITCH), :]
                word = (packed & jnp.uint32(0xFFFF0000)) if half else (packed << 16)
                gate = lax.bitcast_convert_type(word, F32)
                parts.append((gate * _gelu_tanh(s[:, gi * PEER_NKEYS:(gi + 1) * PEER_NKEYS])).astype(BF16))
            acc = acc + _dot(jnp.concatenate(parts, axis=1), v_ref[cols, :])
        acc_s[...] = acc

    @pl.when(j < steps_per_half)
    def _():
        expert_block(0)

    @pl.when(j >= steps_per_half)
    def _():
        expert_block(1)

    @pl.when(j == ne - 1)
    def _():
        gt2 = mod_ref[0][5:6]
        x2 = x1_ref[0] + gt2 * acc_s[...]
        y_ref[0] = (x2 * _rms(x2)) * nfg_ref[...]


def _peer(h2, x1, mod, a_idx, b_idx, gates, ut, v, nfg, tt, eb):
    bsz, seq, _ = h2.shape
    n_exp = v.shape[0]
    ne = n_exp // eb
    assert ne % 2 == 0 and (GATE_HALF * PEER_NKEYS) % eb == 0 and tt % (2 * GATE_UNROLL) == 0
    n_pair = PEER_HEADS * PEER_TOPK
    tile = lambda w: pl.BlockSpec((1, tt, w), lambda b, i, j: (b, i, 0))
    return pl.pallas_call(
        functools.partial(_peer_kernel, tt=tt, eb=eb, ne=ne),
        out_shape=jax.ShapeDtypeStruct((bsz, seq, D_MODEL), F32),
        grid=(bsz, seq // tt, ne),
        in_specs=[
            tile(D_MODEL), tile(D_MODEL),
            pl.BlockSpec((1, SUBLANES, D_MODEL), lambda b, i, j: (b, 0, 0)),
            tile(n_pair), tile(n_pair), tile(n_pair),
            pl.BlockSpec((D_MODEL, eb), lambda b, i, j: (0, j)),
            pl.BlockSpec((eb, D_MODEL), lambda b, i, j: (j, 0)),
            pl.BlockSpec((1, D_MODEL), lambda b, i, j: (0, 0)),
        ],
        out_specs=tile(D_MODEL),
        scratch_shapes=[pltpu.VMEM((tt * GATE_PITCH, PEER_NKEYS), jnp.uint32), pltpu.VMEM((tt, D_MODEL), F32)],
        compiler_params=pltpu.CompilerParams(dimension_semantics=("arbitrary", "arbitrary", "arbitrary"),
                                             vmem_limit_bytes=VMEM_LIMIT),
        name="peer",
    )(h2, x1, mod, a_idx, b_idx, gates, ut, v, nfg)


def _prep_weights(norm1_g, norm2_g, w_in, w_dec_f, b_dec_f, w_dec_b, b_dec_b, gla_norm_g, conv_w, conv_norm_g,
                  w_out, peer_wq, peer_subkeys, peer_u, peer_v, normf_g):
    w = w_in[0]
    q, k, v, g, alf, alb, cb, cc, ch = jnp.split(w, (256, 512, 1024, 1536, 1552, 1568, 2080, 2592), axis=-1)
    pad = jnp.zeros((D_MODEL, C_END - C_AL - 2 * GLA_RANK), F32)
    win = jnp.concatenate([q, k, v, g, cb, cc, ch, alf, alb, pad], axis=-1).astype(BF16)
    wdec = jnp.zeros((LANES, 2 * GLA_KW), F32)
    wdec = wdec.at[0:GLA_RANK, 0:GLA_KW].set(w_dec_f[0]).at[GLA_RANK:2 * GLA_RANK, GLA_KW:].set(w_dec_b[0])
    bdec = jnp.concatenate([b_dec_f[0], b_dec_b[0]])[None, :]
    grp = jnp.arange(CONV_WIDTH) // CONV_GDIM
    mgrp = (grp[:, None] == grp[None, :]).astype(BF16)
    wp = _keyproj(peer_wq[0], peer_subkeys[0].reshape(2 * PEER_HEADS, PEER_NKEYS, PEER_DQH))
    wpt = wp.reshape(PEER_HEADS, 2, PEER_NKEYS, D_MODEL).transpose(1, 2, 0, 3).reshape(KEY_ROWS, D_MODEL)
    return dict(
        n1g=norm1_g[0][None, :], n2g=norm2_g[0][None, :], win=win, wdec=wdec, bdec=bdec,
        convw=conv_w[0], cng=conv_norm_g[0][None, :], mgrp=mgrp, gng=gla_norm_g[0][None, :],
        wout=w_out[0].astype(BF16), wpt=wpt.astype(BF16),
        ut=jnp.transpose(peer_u[0]).astype(BF16), v=peer_v[0].astype(BF16), nfg=normf_g[None, :],
    )


def _trunk(x, mod, w, tt, rtt, ptt, eb):
    nb = MIX_ROWS
    qk, v, lab, of, gy = _mix1(x, mod, w["n1g"], w["win"], w["wdec"], w["bdec"], w["convw"], w["cng"], w["mgrp"],
                               tt, nb)
    x1, h2 = _mix2(x, mod, qk, v, lab, of, gy, w["gng"], w["wout"], w["n2g"], tt, nb)
    a_idx, b_idx, gates = _route(h2, w["wpt"], rtt)
    return _peer(h2, x1, mod, a_idx, b_idx, gates, w["ut"], w["v"], w["nfg"], ptt, eb)


def kernel(x_prompt, x_sample, c_prompt, c_sample, norm1_g, norm2_g, w_ada, b_ada, w_in, w_dec_f, b_dec_f,
           w_dec_b, b_dec_b, gla_norm_g, conv_w, conv_norm_g, w_out, peer_wq, peer_subkeys, peer_u, peer_v,
           normf_g):
    w = _prep_weights(norm1_g, norm2_g, w_in, w_dec_f, b_dec_f, w_dec_b, b_dec_b, gla_norm_g, conv_w,
                      conv_norm_g, w_out, peer_wq, peer_subkeys, peer_u, peer_v, normf_g)
    nb_p, nb_s = c_prompt.shape[0], c_sample.shape[0]
    c_all = jnp.concatenate([c_prompt, c_sample, jnp.zeros((SUBLANES - nb_p - nb_s, D_MODEL), F32)], axis=0)
    ada = _ada(c_all, w_ada[0], b_ada[0][None, :])
    mod = jnp.pad(ada.reshape(SUBLANES, N_ADA, D_MODEL), ((0, 0), (0, SUBLANES - N_ADA), (0, 0)))
    y_prompt = _trunk(x_prompt, mod[:nb_p], w, TOKEN_TILE, ROUTE_TILE, PEER_TOKEN_TILE, EXPERT_BLOCK)
    y_sample = _trunk(x_sample, mod[nb_p:nb_p + nb_s], w, TOKEN_TILE, ROUTE_TILE, PEER_TOKEN_TILE, EXPERT_BLOCK)
    return (y_prompt, y_sample)
```

```python
import functools

import jax
import jax.numpy as jnp
from jax import lax
from jax.experimental import pallas as pl
from jax.experimental.pallas import tpu as pltpu

F32 = jnp.float32
BF16 = jnp.bfloat16
F8 = jnp.float8_e4m3fn
F8_TOP = 224.0
F8_TINY = 1e-30

D_MODEL = 1024
GLA_HEADS = 4
GLA_DV = 128
GLA_DK = 64
GLA_KW = GLA_HEADS * GLA_DK
GLA_WIDTH = GLA_HEADS * GLA_DV
GLA_RANK = 16
GLA_GATE_NORM = 16.0
GLA_CHUNK = 64
CONV_WIDTH = 512
CONV_GDIM = 64
PEER_HEADS = 8
PEER_NKEYS = 128
PEER_DQH = 128
PEER_TOPK = 16
N_ADA = 6
EPS = 1e-6

LANES = 128
SUBLANES = 8
VMEM_LIMIT = 56 * 1024 * 1024

C_QK, C_V, C_G, C_CB, C_CC, C_CH, C_AL, C_END = 0, 512, 1024, 1536, 2048, 2560, 3072, 3200

TOKEN_TILE = 256
MIX_ROWS = 1
ROUTE_TILE = 512
PEER_TOKEN_TILE = 512
EXPERT_BLOCK = 2048
EXPERT_SUB = 2048
GATE_HALF = PEER_NKEYS // 2
GATE_PITCH = GATE_HALF + SUBLANES
GATE_UNROLL = 16


def _dot(a, b):
    return jnp.dot(a, b, preferred_element_type=F32)


def _dot_nt(a, b):
    return lax.dot_general(a, b, (((1,), (1,)), ((), ())), preferred_element_type=F32)


def _dot_tn(a, b):
    return lax.dot_general(a, b, (((0,), (0,)), ((), ())), preferred_element_type=F32)


def _split2(x):
    hi = x.astype(BF16)
    lo = (x - hi.astype(F32)).astype(BF16)
    return hi, lo


def _split3(x):
    hi = x.astype(BF16)
    r = x - hi.astype(F32)
    mid = r.astype(BF16)
    lo = (r - mid.astype(F32)).astype(BF16)
    return hi, mid, lo


def _dot_f32(a, b):
    a_hi, a_lo = _split2(a)
    b_hi, b_lo = _split2(b)
    return _dot(a_hi, b_hi) + _dot(a_hi, b_lo) + _dot(a_lo, b_hi)


def _bf16_bits(x):
    return lax.bitcast_convert_type(x.astype(BF16).astype(F32), jnp.uint32)


def _rms(x):
    return lax.rsqrt(jnp.mean(x * x, axis=-1, keepdims=True) + EPS)


def _sigmoid(x):
    return 1.0 / (1.0 + jnp.exp(-x))


def _log_sigmoid(x):
    return jnp.minimum(x, 0.0) - jnp.log1p(jnp.exp(-jnp.abs(x)))


def _gelu_tanh(x):
    return 0.5 * x * (1.0 + jnp.tanh(0.7978845608028654 * (x + 0.044715 * (x * x * x))))


def _ada_kernel(c_ref, w_ref, b_ref, o_ref):
    c = c_ref[...]
    o_ref[...] = _dot_f32(c * _sigmoid(c), w_ref[...]) + b_ref[...]


def _ada(c_pad, w_ada, b_ada):
    n_col = N_ADA * D_MODEL
    blk = 1536
    return pl.pallas_call(
        _ada_kernel,
        out_shape=jax.ShapeDtypeStruct((SUBLANES, n_col), F32),
        grid=(n_col // blk,),
        in_specs=[
            pl.BlockSpec((SUBLANES, D_MODEL), lambda j: (0, 0)),
            pl.BlockSpec((D_MODEL, blk), lambda j: (0, j)),
            pl.BlockSpec((1, blk), lambda j: (0, j)),
        ],
        out_specs=pl.BlockSpec((SUBLANES, blk), lambda j: (0, j)),
        compiler_params=pltpu.CompilerParams(dimension_semantics=("arbitrary",), vmem_limit_bytes=VMEM_LIMIT),
        name="ada",
    )(c_pad, w_ada, b_ada)


def _gla_tile(q, k, v, la, st_ref, forward):
    tt = q.shape[0]
    n_chunk = tt // GLA_CHUNK
    row = lax.broadcasted_iota(jnp.int32, (tt, tt), 0)
    col = lax.broadcasted_iota(jnp.int32, (tt, tt), 1)
    ordered = (row >= col) if forward else (row <= col)
    causal = ordered & (row // GLA_CHUNK == col // GLA_CHUNK)
    tri = jnp.where(causal, 1.0, 0.0).astype(BF16)
    la_hi, la_mid, la_lo = _split3(la)
    b = _dot(tri, la_hi) + _dot(tri, la_mid) + _dot(tri, la_lo)
    end_rows = [ci * GLA_CHUNK + (GLA_CHUNK - 1 if forward else 0) for ci in range(n_chunk)]
    b_end = jnp.concatenate([jnp.broadcast_to(b[r:r + 1, :], (GLA_CHUNK, GLA_KW)) for r in end_rows], axis=0)
    q_dec = q * (jnp.exp(b) * (GLA_DK ** -0.5))
    k_dec = k * jnp.exp(-b)
    k_end = k * jnp.exp(b_end - b)
    decay = [jnp.exp(b[r:r + 1, :]) for r in end_rows]
    lane = lax.broadcasted_iota(jnp.int32, (1, LANES), 1)
    scan = range(n_chunk) if forward else range(n_chunk - 1, -1, -1)
    outs = []
    for head in range(GLA_HEADS):
        pair, half = head // 2, head % 2
        sl = slice(pair * LANES, (pair + 1) * LANES)
        in_head = (lane // GLA_DK) == half
        qd = jnp.where(in_head, q_dec[:, sl], 0.0).astype(BF16)
        kd = k_dec[:, sl].astype(BF16)
        ke = k_end[:, sl].astype(BF16)
        v_h = v[:, head * GLA_DV:(head + 1) * GLA_DV].astype(BF16)
        scores = jnp.where(causal, _dot_nt(qd, kd), 0.0)
        o_intra = _dot(scores.astype(BF16), v_h)
        st = st_ref[head]
        parts = [None] * n_chunk
        for ci in scan:
            rows = slice(ci * GLA_CHUNK, (ci + 1) * GLA_CHUNK)
            parts[ci] = o_intra[rows] + _dot_nt(qd[rows], st.astype(BF16))
            st = st * decay[ci][:, sl] + _dot_tn(v_h[rows], ke[rows])
        st_ref[head] = st
        outs.append(jnp.concatenate(parts, axis=0))
    return jnp.concatenate(outs, axis=-1)


def _mix1_kernel(x_ref, xp_ref, xn_ref, mod_ref, n1g_ref, win_ref, wdec_ref, bdec_ref, convw_ref, cng_ref,
                 mgrp_ref, qk_ref, v_ref, lab_ref, of_ref, gy_ref, st_s, *, tt, nt, nb):
    i = pl.program_id(1)

    @pl.when(i == 0)
    def _():
        st_s[...] = jnp.zeros_like(st_s)

    n1g = n1g_ref[...]
    cw = convw_ref[...]
    row = lax.broadcasted_iota(jnp.int32, (tt, 1), 0)
    for bi in range(nb):
        mod = mod_ref[bi]
        sh1, sc1 = mod[0:1], mod[1:2]

        def norm_mod(x):
            return ((x * _rms(x)) * n1g) * (1.0 + sc1) + sh1

        h = norm_mod(x_ref[bi]).astype(BF16)
        qk_ref[bi] = _dot(h, win_ref[:, C_QK:C_V])
        v_ref[bi] = _dot(h, win_ref[:, C_V:C_G])
        gy_ref[bi, :, 0:GLA_WIDTH] = _dot(h, win_ref[:, C_G:C_CB])

        hh = norm_mod(jnp.concatenate([xp_ref[bi], xn_ref[bi]], axis=0)).astype(BF16)
        h_ext = jnp.concatenate([h, hh], axis=0)
        z_ext = _dot(h_ext, win_ref[:, C_CC:C_CH]) * _dot(h_ext, win_ref[:, C_CH:C_AL])
        z = z_ext[0:tt]
        z_prev = jnp.where(i > 0, z_ext[tt + SUBLANES - 1:tt + SUBLANES, :], 0.0)
        z_next = jnp.where(i < nt - 1, z_ext[tt + SUBLANES:tt + SUBLANES + 1, :], 0.0)
        z_m1 = jnp.where(row == 0, z_prev, pltpu.roll(z, 1, axis=0))
        z_p1 = jnp.where(row == tt - 1, z_next, pltpu.roll(z, tt - 1, axis=0))
        conv = cw[0:1] * z_m1 + cw[1:2] * z + cw[2:3] * z_p1
        yc = _dot(h, win_ref[:, C_CB:C_CC]) * conv
        sq_hi, sq_lo = _split2(yc * yc)
        ss = _dot(sq_hi, mgrp_ref[...]) + _dot(sq_lo, mgrp_ref[...])
        gy_ref[bi, :, GLA_WIDTH:] = (yc * lax.rsqrt(ss * (1.0 / CONV_GDIM) + EPS)) * cng_ref[...]

        zd = _dot_f32(_dot(h, win_ref[:, C_AL:C_END]), wdec_ref[...]) + bdec_ref[...]
        la = _log_sigmoid(zd) * (1.0 / GLA_GATE_NORM)
        lab_ref[bi] = la[:, GLA_KW:]
        of_ref[bi] = _gla_tile(qk_ref[bi, :, 0:GLA_KW], qk_ref[bi, :, GLA_KW:], v_ref[bi], la[:, 0:GLA_KW],
                               st_s.at[bi], True)


def _mix1(x, mod, n1g, win, wdec, bdec, convw, cng, mgrp, tt, nb):
    bsz, seq, _ = x.shape
    nt = seq // tt
    hb = tt // SUBLANES
    last_hb = seq // SUBLANES - 1
    const = lambda shape: pl.BlockSpec(shape, lambda b, i: tuple(0 for _ in shape))
    tile = lambda w: pl.BlockSpec((nb, tt, w), lambda b, i: (b, i, 0))
    out_w = (2 * GLA_KW, GLA_WIDTH, GLA_KW, GLA_WIDTH, GLA_WIDTH + CONV_WIDTH)
    return pl.pallas_call(
        functools.partial(_mix1_kernel, tt=tt, nt=nt, nb=nb),
        out_shape=tuple(jax.ShapeDtypeStruct((bsz, seq, w), F32) for w in out_w),
        grid=(bsz // nb, nt),
        in_specs=[
            tile(D_MODEL),
            pl.BlockSpec((nb, SUBLANES, D_MODEL), lambda b, i: (b, jnp.maximum(i * hb - 1, 0), 0)),
            pl.BlockSpec((nb, SUBLANES, D_MODEL), lambda b, i: (b, jnp.minimum((i + 1) * hb, last_hb), 0)),
            pl.BlockSpec((nb, SUBLANES, D_MODEL), lambda b, i: (b, 0, 0)),
            const((1, D_MODEL)),
            const((D_MODEL, C_END)),
            const((LANES, 2 * GLA_KW)),
            const((1, 2 * GLA_KW)),
            const((3, CONV_WIDTH)),
            const((1, CONV_WIDTH)),
            const((CONV_WIDTH, CONV_WIDTH)),
        ],
        out_specs=tuple(tile(w) for w in out_w),
        scratch_shapes=[pltpu.VMEM((nb, GLA_HEADS, GLA_DV, LANES), F32)],
        compiler_params=pltpu.CompilerParams(dimension_semantics=("arbitrary", "arbitrary"),
                                             vmem_limit_bytes=VMEM_LIMIT),
        name="mix1",
    )(x, x, x, mod, n1g, win, wdec, bdec, convw, cng, mgrp)


def _mix2_kernel(x_ref, mod_ref, qk_ref, v_ref, lab_ref, of_ref, gy_ref, gng_ref, wout_ref, n2g_ref,
                 x1_ref, h2_ref, h8_ref, hs_ref, st_s, *, nb):
    i = pl.program_id(1)

    @pl.when(i == 0)
    def _():
        st_s[...] = jnp.zeros_like(st_s)

    gng = gng_ref[...]
    for bi in range(nb):
        mod = mod_ref[bi]
        gt1, sh2, sc2 = mod[2:3], mod[3:4], mod[4:5]
        o = of_ref[bi] + _gla_tile(qk_ref[bi, :, 0:GLA_KW], qk_ref[bi, :, GLA_KW:], v_ref[bi], lab_ref[bi],
                                   st_s.at[bi], False)
        y = None
        for head in range(GLA_HEADS):
            sl = slice(head * GLA_DV, (head + 1) * GLA_DV)
            oh = o[:, sl]
            g = gy_ref[bi, :, sl]
            yh = ((oh * _rms(oh)) * gng) * (g * _sigmoid(g))
            part = _dot(yh.astype(BF16), wout_ref[sl, :])
            y = part if y is None else y + part
        y = y + _dot(gy_ref[bi, :, GLA_WIDTH:].astype(BF16), wout_ref[GLA_WIDTH:, :])
        x1 = x_ref[bi] + gt1 * y
        x1_ref[bi] = x1
        h2 = ((x1 * _rms(x1)) * n2g_ref[...]) * (1.0 + sc2) + sh2
        h2_ref[bi] = h2.astype(BF16)
        amax = jnp.maximum(jnp.max(jnp.abs(h2), axis=-1, keepdims=True), F8_TINY)
        h8_ref[bi] = (h2 * (F8_TOP / amax)).astype(F8)
        hs_ref[bi] = jnp.broadcast_to(amax * (1.0 / F8_TOP), (h2.shape[0], LANES))


def _mix2(x, mod, qk, v, lab, of, gy, gng, wout, n2g, tt, nb):
    bsz, seq, _ = x.shape
    nt = seq // tt
    const = lambda shape: pl.BlockSpec(shape, lambda b, i: tuple(0 for _ in shape))
    tile = lambda w: pl.BlockSpec((nb, tt, w), lambda b, i: (b, nt - 1 - i, 0))
    return pl.pallas_call(
        functools.partial(_mix2_kernel, nb=nb),
        out_shape=(jax.ShapeDtypeStruct((bsz, seq, D_MODEL), F32), jax.ShapeDtypeStruct((bsz, seq, D_MODEL), BF16),
                   jax.ShapeDtypeStruct((bsz, seq, D_MODEL), F8), jax.ShapeDtypeStruct((bsz, seq, LANES), F32)),
        grid=(bsz // nb, nt),
        in_specs=[
            tile(D_MODEL),
            pl.BlockSpec((nb, SUBLANES, D_MODEL), lambda b, i: (b, 0, 0)),
            tile(2 * GLA_KW), tile(GLA_WIDTH), tile(GLA_KW), tile(GLA_WIDTH), tile(GLA_WIDTH + CONV_WIDTH),
            const((1, GLA_DV)),
            const((D_MODEL, D_MODEL)),
            const((1, D_MODEL)),
        ],
        out_specs=(tile(D_MODEL), tile(D_MODEL), tile(D_MODEL), tile(LANES)),
        scratch_shapes=[pltpu.VMEM((nb, GLA_HEADS, GLA_DV, LANES), F32)],
        compiler_params=pltpu.CompilerParams(dimension_semantics=("arbitrary", "arbitrary"),
                                             vmem_limit_bytes=VMEM_LIMIT),
        name="mix2",
    )(x, mod, qk, v, lab, of, gy, gng, wout, n2g)


NEG_INF = float("-inf")
KEY_ROWS = 2 * PEER_NKEYS * PEER_HEADS
VREG = (SUBLANES, LANES)


def _keyproj_kernel(wq_ref, sk_ref, o_ref):
    w_hi, w_lo = _split2(wq_ref[...])
    k_hi, k_lo = _split2(sk_ref[0])
    o_ref[0] = _dot_nt(k_hi, w_hi) + _dot_nt(k_hi, w_lo) + _dot_nt(k_lo, w_hi)


def _keyproj(wq, subkeys):
    n_hp = 2 * PEER_HEADS
    return pl.pallas_call(
        _keyproj_kernel,
        out_shape=jax.ShapeDtypeStruct((n_hp, PEER_NKEYS, D_MODEL), F32),
        grid=(n_hp,),
        in_specs=[pl.BlockSpec((D_MODEL, PEER_DQH), lambda j: (0, j)),
                  pl.BlockSpec((1, PEER_NKEYS, PEER_DQH), lambda j: (j, 0, 0))],
        out_specs=pl.BlockSpec((1, PEER_NKEYS, D_MODEL), lambda j: (j, 0, 0)),
        compiler_params=pltpu.CompilerParams(dimension_semantics=("arbitrary",), vmem_limit_bytes=VMEM_LIMIT),
        name="keyproj",
    )(wq, subkeys)


def _sort16_network():
    pairs = []

    def merge(lo, hi, r):
        step = r * 2
        if step < hi - lo:
            merge(lo, hi, step)
            merge(lo + r, hi, step)
            pairs.extend((i, i + r) for i in range(lo + r, hi - r, step))
        else:
            pairs.append((lo, lo + r))

    def sort(lo, hi):
        if hi - lo >= 1:
            mid = lo + (hi - lo) // 2
            sort(lo, mid)
            sort(mid + 1, hi)
            merge(lo, hi, 1)

    sort(0, PEER_TOPK - 1)
    return tuple(pairs)


SORT16 = _sort16_network()
CAND_LISTS = ([[(0, k2) for k2 in range(PEER_TOPK)]]
              + [[(k1, k2) for k2 in range(PEER_TOPK // (k1 + 1))] for k1 in range(1, 8)]
              + [[(k1, 0) for k1 in range(8, PEER_TOPK)]])
CANDS = [c for lst in CAND_LISTS for c in lst]


def _tree(op, xs):
    xs = list(xs)
    while len(xs) > 1:
        xs = [op(xs[i], xs[i + 1]) for i in range(0, len(xs) - 1, 2)] + ([xs[-1]] if len(xs) % 2 else [])
    return xs[0]


def _ce(v, x, i, j):
    c = v[j] > v[i]
    v[i], v[j] = jnp.maximum(v[i], v[j]), jnp.minimum(v[i], v[j])
    x[i], x[j] = jnp.where(c, x[j], x[i]), jnp.where(c, x[i], x[j])


def _merge_top16(va, xa, vb, xb, sort_result):
    v, x = list(va), list(xa)
    for j in range(PEER_TOPK - len(vb), PEER_TOPK):
        b = PEER_TOPK - 1 - j
        c = vb[b] > va[j]
        v[j] = jnp.maximum(va[j], vb[b])
        x[j] = jnp.where(c, xb[b], xa[j])
    if sort_result:
        for d in (8, 4, 2, 1):
            for i in range(PEER_TOPK):
                if not i & d:
                    _ce(v, x, i, i + d)
    return v, x


def _any(flag):
    return jnp.max(flag) > 0.0


def _top16_keys(key, work_v, work_x, sv_ref, si_ref):
    for g in range(PEER_NKEYS // PEER_TOPK):
        v = [key(PEER_TOPK * g + i) for i in range(PEER_TOPK)]
        x = [float(PEER_TOPK * g + i) for i in range(PEER_TOPK)]
        for i, j in SORT16:
            _ce(v, x, i, j)
        for i in range(PEER_TOPK):
            work_v[PEER_TOPK * g + i] = v[i]
            work_x[PEER_TOPK * g + i] = x[i]
    for span in (1, 2, 4):
        for g in range(0, PEER_NKEYS // PEER_TOPK, 2 * span):
            a, b = PEER_TOPK * g, PEER_TOPK * (g + span)
            v, x = _merge_top16([work_v[a + i] for i in range(PEER_TOPK)], [work_x[a + i] for i in range(PEER_TOPK)],
                                [work_v[b + i] for i in range(PEER_TOPK)], [work_x[b + i] for i in range(PEER_TOPK)],
                                True)
            for i in range(PEER_TOPK):
                if span == 4:
                    sv_ref[i], si_ref[i] = v[i], x[i]
                else:
                    work_v[a + i], work_x[a + i] = v[i], x[i]
    repeat = _tree(jnp.maximum, [jnp.where(v[r] == v[r + 1], 1.0, 0.0) for r in range(PEER_TOPK - 1)])
    n_ge = _tree(jnp.add, [jnp.where(key(k) >= v[PEER_TOPK - 1], 1.0, 0.0) for k in range(PEER_NKEYS)])

    @pl.when(_any(jnp.maximum(repeat, jnp.where(n_ge > float(PEER_TOPK), 1.0, 0.0))))
    def _():
        for k in range(PEER_NKEYS):
            work_v[k] = key(k)

        def extract(r, carry):
            vals = [work_v[k] for k in range(PEER_NKEYS)]
            m = _tree(jnp.maximum, vals)
            idx = _tree(jnp.minimum, [jnp.where(vals[k] == m, float(k), float(PEER_NKEYS)) for k in range(PEER_NKEYS)])
            for k in range(PEER_NKEYS):
                work_v[k] = jnp.where(idx == float(k), NEG_INF, vals[k])
            sv_ref[r] = m
            si_ref[r] = idx
            return carry

        lax.fori_loop(0, PEER_TOPK, extract, 0)


def _top16_sums(sv_s, si_s, work_v, work_x, rv_s, re_s):
    sv0 = [sv_s[0, r] for r in range(PEER_TOPK)]
    sv1 = [sv_s[1, r] for r in range(PEER_TOPK)]
    e_hi = [si_s[0, r] * float(PEER_NKEYS) for r in range(PEER_TOPK)]
    si1 = [si_s[1, r] for r in range(PEER_TOPK)]
    val = {c: sv0[c[0]] + sv1[c[1]] for c in CANDS}
    eid = {c: e_hi[c[0]] + si1[c[1]] for c in CANDS}
    v, x = [val[c] for c in CAND_LISTS[0]], [eid[c] for c in CAND_LISTS[0]]
    for n, lst in enumerate(CAND_LISTS[1:]):
        v, x = _merge_top16(v, x, [val[c] for c in lst], [eid[c] for c in lst], n < len(CAND_LISTS) - 2)
    for r in range(PEER_TOPK):
        rv_s[r], re_s[r] = v[r], x[r]
    low = _tree(jnp.minimum, v)
    n_ge = _tree(jnp.add, [jnp.where(val[c] >= low, 1.0, 0.0) for c in CANDS])

    @pl.when(_any(jnp.where(n_ge > float(PEER_TOPK), 1.0, 0.0)))
    def _():
        for n, c in enumerate(CANDS):
            work_v[n], work_x[n] = val[c], eid[c]

        def extract(r, carry):
            vals = [work_v[n] for n in range(len(CANDS))]
            m = _tree(jnp.maximum, vals)
            code = [float(c[0] * PEER_TOPK + c[1]) for c in CANDS]
            sel = _tree(jnp.minimum, [jnp.where(vals[n] == m, code[n], 1e9) for n in range(len(CANDS))])
            hit = [sel == code[n] for n in range(len(CANDS))]
            rv_s[r] = m
            re_s[r] = _tree(jnp.add, [jnp.where(hit[n], work_x[n], 0.0) for n in range(len(CANDS))])
            for n in range(len(CANDS)):
                work_v[n] = jnp.where(hit[n], NEG_INF, vals[n])
            return carry

        lax.fori_loop(0, PEER_TOPK, extract, 0)


def _route_kernel(h2_ref, wpt_ref, a_ref, b_ref, g_ref, sc_s, work_v, work_x, sv_s, si_s, rv_s, re_s,
                  pa_s, pb_s, pg_s, *, tt):
    sc_s[...] = _dot_nt(wpt_ref[...], h2_ref[0])

    def lane_tile(c, carry):
        lanes = pl.ds(pl.multiple_of(c * LANES, LANES), LANES)
        for p in range(2):
            def key(k, p=p):
                r0 = (p * PEER_NKEYS + k) * SUBLANES
                return sc_s[r0:r0 + SUBLANES, lanes]
            _top16_keys(key, work_v, work_x, sv_s.at[p], si_s.at[p])
        _top16_sums(sv_s, si_s, work_v, work_x, rv_s, re_s)
        v = [rv_s[r] for r in range(PEER_TOPK)]
        m = _tree(jnp.maximum, v)
        ex = [jnp.exp(vr - m) for vr in v]
        inv = 1.0 / _tree(jnp.add, ex)
        for r in range(PEER_TOPK):
            e = re_s[r]
            i1 = jnp.floor(e * (1.0 / PEER_NKEYS))
            rows = slice(r * SUBLANES, (r + 1) * SUBLANES)
            pa_s[rows, :] = i1
            pb_s[rows, :] = e - i1 * float(PEER_NKEYS)
            pg_s[rows, :] = ex[r] * inv
        toks = pl.ds(pl.multiple_of(c * LANES, LANES), LANES)
        a_ref[0, toks, :] = jnp.transpose(pa_s[...])
        b_ref[0, toks, :] = jnp.transpose(pb_s[...])
        g_ref[0, toks, :] = jnp.transpose(pg_s[...])
        return carry

    lax.fori_loop(0, tt // LANES, lane_tile, 0)


def _route(h2, wpt, tt):
    bsz, seq, _ = h2.shape
    n_pair = PEER_HEADS * PEER_TOPK
    tile = lambda w: pl.BlockSpec((1, tt, w), lambda b, i: (b, i, 0))
    vregs = lambda *lead: pltpu.VMEM(lead + VREG, F32)
    return pl.pallas_call(
        functools.partial(_route_kernel, tt=tt),
        out_shape=tuple(jax.ShapeDtypeStruct((bsz, seq, n_pair), F32) for _ in range(3)),
        grid=(bsz, seq // tt),
        in_specs=[tile(D_MODEL), pl.BlockSpec((KEY_ROWS, D_MODEL), lambda b, i: (0, 0))],
        out_specs=(tile(n_pair), tile(n_pair), tile(n_pair)),
        scratch_shapes=[pltpu.VMEM((KEY_ROWS, tt), F32), vregs(PEER_NKEYS), vregs(PEER_NKEYS),
                        vregs(2, PEER_TOPK), vregs(2, PEER_TOPK), vregs(PEER_TOPK), vregs(PEER_TOPK),
                        pltpu.VMEM((n_pair, LANES), F32), pltpu.VMEM((n_pair, LANES), F32),
                        pltpu.VMEM((n_pair, LANES), F32)],
        compiler_params=pltpu.CompilerParams(dimension_semantics=("arbitrary", "arbitrary"),
                                             vmem_limit_bytes=VMEM_LIMIT),
        name="route",
    )(h2, wpt)


def _peer_kernel(h8_ref, hs_ref, x1_ref, mod_ref, a_ref, b_ref, g_ref, ut_ref, us_ref, v_ref, nfg_ref, y_ref,
                 gate_s, acc_s, *, tt, eb, ne):
    j = pl.program_id(2)
    n_grp = eb // PEER_NKEYS
    steps_per_half = ne // 2

    @pl.when(j == 0)
    def _():
        acc_s[...] = jnp.zeros_like(acc_s)
        sub = lax.broadcasted_iota(jnp.int32, (PEER_NKEYS, PEER_NKEYS), 0).astype(F32).astype(BF16)
        zero = jnp.zeros((PEER_NKEYS, PEER_NKEYS), BF16)
        one = jnp.ones((PEER_NKEYS, PEER_NKEYS), BF16)

        def onehots(t):
            bc = lambda ref: jnp.broadcast_to(ref[0, pl.ds(t, 1), :], (PEER_NKEYS, PEER_NKEYS)).astype(BF16)
            at = jnp.where(sub == bc(a_ref), one, zero)
            cbt = jnp.where(sub == bc(b_ref), bc(g_ref), zero)
            return at, cbt

        def build(p, carry):
            for u in range(GATE_UNROLL):
                t0 = (p * GATE_UNROLL + u) * 2
                at0, cbt0 = onehots(t0)
                at1, cbt1 = onehots(t0 + 1)
                lhs = jnp.concatenate([at0, at1], axis=1)
                rhs = jnp.concatenate([jnp.concatenate([cbt0.T, zero], axis=1),
                                       jnp.concatenate([zero, cbt1.T], axis=1)], axis=0)
                tiles = _dot(lhs, rhs)
                for k in range(2):
                    lo = tiles[0:GATE_HALF, k * PEER_NKEYS:(k + 1) * PEER_NKEYS]
                    hi = tiles[GATE_HALF:, k * PEER_NKEYS:(k + 1) * PEER_NKEYS]
                    r0 = pl.multiple_of((t0 + k) * GATE_PITCH, SUBLANES)
                    gate_s[pl.ds(r0, GATE_HALF), :] = (_bf16_bits(lo) >> 16) | _bf16_bits(hi)
            return carry

        lax.fori_loop(0, tt // (2 * GATE_UNROLL), build, 0)

    def expert_block(half):
        acc = acc_s[...]
        for sub in range(eb // EXPERT_SUB):
            cols = slice(sub * EXPERT_SUB, (sub + 1) * EXPERT_SUB)
            s = _dot(h8_ref[0], ut_ref[:, cols])
            row_scale = hs_ref[0]
            parts = []
            for gi in range(EXPERT_SUB // PEER_NKEYS):
                g_abs = sub * (EXPERT_SUB // PEER_NKEYS) + gi
                lanes = slice(g_abs * PEER_NKEYS, (g_abs + 1) * PEER_NKEYS)
                r = (j - half * steps_per_half) * n_grp + g_abs
                packed = gate_s[pl.ds(r, tt, stride=GATE_PITCH), :]
                word = (packed & jnp.uint32(0xFFFF0000)) if half else (packed << 16)
                gate = lax.bitcast_convert_type(word, F32)
                act = s[:, gi * PEER_NKEYS:(gi + 1) * PEER_NKEYS] * (row_scale * us_ref[:, lanes])
                parts.append((gate * _gelu_tanh(act)).astype(BF16))
            acc = acc + _dot(jnp.concatenate(parts, axis=1), v_ref[cols, :])
        acc_s[...] = acc

    @pl.when(j < steps_per_half)
    def _():
        expert_block(0)

    @pl.when(j >= steps_per_half)
    def _():
        expert_block(1)

    @pl.when(j == ne - 1)
    def _():
        gt2 = mod_ref[0][5:6]
        x2 = x1_ref[0] + gt2 * acc_s[...]
        y_ref[0] = (x2 * _rms(x2)) * nfg_ref[...]


def _peer(h8, hs, x1, mod, a_idx, b_idx, gates, ut8, us, v, nfg, tt, eb):
    bsz, seq, _ = h8.shape
    n_exp = v.shape[0]
    ne = n_exp // eb
    assert ne % 2 == 0 and (GATE_HALF * PEER_NKEYS) % eb == 0 and tt % (2 * GATE_UNROLL) == 0
    n_pair = PEER_HEADS * PEER_TOPK
    tile = lambda w: pl.BlockSpec((1, tt, w), lambda b, i, j: (b, i, 0))
    return pl.pallas_call(
        functools.partial(_peer_kernel, tt=tt, eb=eb, ne=ne),
        out_shape=jax.ShapeDtypeStruct((bsz, seq, D_MODEL), F32),
        grid=(bsz, seq // tt, ne),
        in_specs=[
            tile(D_MODEL), tile(LANES), tile(D_MODEL),
            pl.BlockSpec((1, SUBLANES, D_MODEL), lambda b, i, j: (b, 0, 0)),
            tile(n_pair), tile(n_pair), tile(n_pair),
            pl.BlockSpec((D_MODEL, eb), lambda b, i, j: (0, j)),
            pl.BlockSpec((1, eb), lambda b, i, j: (0, j)),
            pl.BlockSpec((eb, D_MODEL), lambda b, i, j: (j, 0)),
            pl.BlockSpec((1, D_MODEL), lambda b, i, j: (0, 0)),
        ],
        out_specs=tile(D_MODEL),
        scratch_shapes=[pltpu.VMEM((tt * GATE_PITCH, PEER_NKEYS), jnp.uint32), pltpu.VMEM((tt, D_MODEL), F32)],
        compiler_params=pltpu.CompilerParams(dimension_semantics=("arbitrary", "arbitrary", "arbitrary"),
                                             vmem_limit_bytes=VMEM_LIMIT),
        name="peer",
    )(h8, hs, x1, mod, a_idx, b_idx, gates, ut8, us, v, nfg)


def _uquant_kernel(ut_ref, u8_ref, us_ref):
    ut = ut_ref[...]
    amax = jnp.maximum(jnp.max(jnp.abs(ut), axis=0, keepdims=True), F8_TINY)
    u8_ref[...] = (ut * (F8_TOP / amax)).astype(F8)
    us_ref[...] = amax * (1.0 / F8_TOP)


def _uquant(ut, eb):
    d, n_exp = ut.shape
    return pl.pallas_call(
        _uquant_kernel,
        out_shape=(jax.ShapeDtypeStruct((d, n_exp), F8), jax.ShapeDtypeStruct((1, n_exp), F32)),
        grid=(n_exp // eb,),
        in_specs=[pl.BlockSpec((d, eb), lambda j: (0, j))],
        out_specs=(pl.BlockSpec((d, eb), lambda j: (0, j)), pl.BlockSpec((1, eb), lambda j: (0, j))),
        compiler_params=pltpu.CompilerParams(dimension_semantics=("arbitrary",), vmem_limit_bytes=VMEM_LIMIT),
        name="uquant",
    )(ut)


def _prep_weights(norm1_g, norm2_g, w_in, w_dec_f, b_dec_f, w_dec_b, b_dec_b, gla_norm_g, conv_w, conv_norm_g,
                  w_out, peer_wq, peer_subkeys, peer_u, peer_v, normf_g):
    w = w_in[0]
    q, k, v, g, alf, alb, cb, cc, ch = jnp.split(w, (256, 512, 1024, 1536, 1552, 1568, 2080, 2592), axis=-1)
    pad = jnp.zeros((D_MODEL, C_END - C_AL - 2 * GLA_RANK), F32)
    win = jnp.concatenate([q, k, v, g, cb, cc, ch, alf, alb, pad], axis=-1).astype(BF16)
    wdec = jnp.zeros((LANES, 2 * GLA_KW), F32)
    wdec = wdec.at[0:GLA_RANK, 0:GLA_KW].set(w_dec_f[0]).at[GLA_RANK:2 * GLA_RANK, GLA_KW:].set(w_dec_b[0])
    bdec = jnp.concatenate([b_dec_f[0], b_dec_b[0]])[None, :]
    grp = jnp.arange(CONV_WIDTH) // CONV_GDIM
    mgrp = (grp[:, None] == grp[None, :]).astype(BF16)
    wp = _keyproj(peer_wq[0], peer_subkeys[0].reshape(2 * PEER_HEADS, PEER_NKEYS, PEER_DQH))
    wpt = wp.reshape(PEER_HEADS, 2, PEER_NKEYS, D_MODEL).transpose(1, 2, 0, 3).reshape(KEY_ROWS, D_MODEL)
    ut8, us = _uquant(jnp.transpose(peer_u[0]), EXPERT_BLOCK)
    return dict(
        n1g=norm1_g[0][None, :], n2g=norm2_g[0][None, :], win=win, wdec=wdec, bdec=bdec,
        convw=conv_w[0], cng=conv_norm_g[0][None, :], mgrp=mgrp, gng=gla_norm_g[0][None, :],
        wout=w_out[0].astype(BF16), wpt=wpt.astype(BF16),
        ut8=ut8, us=us, v=peer_v[0].astype(BF16), nfg=normf_g[None, :],
    )


def _trunk(x, mod, w, tt, rtt, ptt, eb):
    nb = MIX_ROWS
    qk, v, lab, of, gy = _mix1(x, mod, w["n1g"], w["win"], w["wdec"], w["bdec"], w["convw"], w["cng"], w["mgrp"],
                               tt, nb)
    x1, h2, h8, hs = _mix2(x, mod, qk, v, lab, of, gy, w["gng"], w["wout"], w["n2g"], tt, nb)
    a_idx, b_idx, gates = _route(h2, w["wpt"], rtt)
    return _peer(h8, hs, x1, mod, a_idx, b_idx, gates, w["ut8"], w["us"], w["v"], w["nfg"], ptt, eb)


def kernel(x_prompt, x_sample, c_prompt, c_sample, norm1_g, norm2_g, w_ada, b_ada, w_in, w_dec_f, b_dec_f,
           w_dec_b, b_dec_b, gla_norm_g, conv_w, conv_norm_g, w_out, peer_wq, peer_subkeys, peer_u, peer_v,
           normf_g):
    w = _prep_weights(norm1_g, norm2_g, w_in, w_dec_f, b_dec_f, w_dec_b, b_dec_b, gla_norm_g, conv_w,
                      conv_norm_g, w_out, peer_wq, peer_subkeys, peer_u, peer_v, normf_g)
    nb_p, nb_s = c_prompt.shape[0], c_sample.shape[0]
    c_all = jnp.concatenate([c_prompt, c_sample, jnp.zeros((SUBLANES - nb_p - nb_s, D_MODEL), F32)], axis=0)
    ada = _ada(c_all, w_ada[0], b_ada[0][None, :])
    mod = jnp.pad(ada.reshape(SUBLANES, N_ADA, D_MODEL), ((0, 0), (0, SUBLANES - N_ADA), (0, 0)))
    y_prompt = _trunk(x_prompt, mod[:nb_p], w, TOKEN_TILE, ROUTE_TILE, PEER_TOKEN_TILE, EXPERT_BLOCK)
    y_sample = _trunk(x_sample, mod[nb_p:nb_p + nb_s], w, TOKEN_TILE, ROUTE_TILE, PEER_TOKEN_TILE, EXPERT_BLOCK)
    return (y_prompt, y_sample)
```

```python
import functools

import jax
import jax.numpy as jnp
from jax import lax
from jax.experimental import pallas as pl
from jax.experimental.pallas import tpu as pltpu

F32 = jnp.float32
BF16 = jnp.bfloat16
F8 = jnp.float8_e4m3fn
F8_TOP = 224.0
F8_TINY = 1e-30

D_MODEL = 1024
GLA_HEADS = 4
GLA_DV = 128
GLA_DK = 64
GLA_KW = GLA_HEADS * GLA_DK
GLA_WIDTH = GLA_HEADS * GLA_DV
GLA_RANK = 16
GLA_GATE_NORM = 16.0
GLA_CHUNK = 64
CONV_WIDTH = 512
CONV_GDIM = 64
PEER_HEADS = 8
PEER_NKEYS = 128
PEER_DQH = 128
PEER_TOPK = 16
N_ADA = 6
EPS = 1e-6

LANES = 128
SUBLANES = 8
VMEM_LIMIT = 56 * 1024 * 1024

C_QK, C_V, C_G, C_CB, C_CC, C_CH, C_AL, C_END = 0, 512, 1024, 1536, 2048, 2560, 3072, 3200

TOKEN_TILE = 256
MIX_ROWS = 1
ROUTE_TILE = 512
PEER_TOKEN_TILE = 512
EXPERT_BLOCK = 2048
EXPERT_SUB = 2048
GATE_HALF = PEER_NKEYS // 2
GATE_PITCH = GATE_HALF + SUBLANES
GATE_UNROLL = 64


def _dot(a, b):
    return jnp.dot(a, b, preferred_element_type=F32)


def _dot_nt(a, b):
    return lax.dot_general(a, b, (((1,), (1,)), ((), ())), preferred_element_type=F32)


def _dot_tn(a, b):
    return lax.dot_general(a, b, (((0,), (0,)), ((), ())), preferred_element_type=F32)


def _split2(x):
    hi = x.astype(BF16)
    lo = (x - hi.astype(F32)).astype(BF16)
    return hi, lo


def _split3(x):
    hi = x.astype(BF16)
    r = x - hi.astype(F32)
    mid = r.astype(BF16)
    lo = (r - mid.astype(F32)).astype(BF16)
    return hi, mid, lo


def _dot_f32(a, b):
    a_hi, a_lo = _split2(a)
    b_hi, b_lo = _split2(b)
    return _dot(a_hi, b_hi) + _dot(a_hi, b_lo) + _dot(a_lo, b_hi)


def _bf16_bits(x):
    return lax.bitcast_convert_type(x.astype(BF16).astype(F32), jnp.uint32)


def _rms(x):
    return lax.rsqrt(jnp.mean(x * x, axis=-1, keepdims=True) + EPS)


def _sigmoid(x):
    return 1.0 / (1.0 + jnp.exp(-x))


def _log_sigmoid(x):
    return jnp.minimum(x, 0.0) - jnp.log1p(jnp.exp(-jnp.abs(x)))


GELU_C1 = 0.7978845608028654
GELU_C3 = GELU_C1 * 0.044715


def _half_gated_gelu(x, half_gate):
    gx = half_gate * x
    return gx * jnp.tanh(x * (GELU_C1 + GELU_C3 * (x * x))) + gx


def _ada_kernel(c_ref, w_ref, b_ref, o_ref):
    c = c_ref[...]
    o_ref[...] = _dot_f32(c * _sigmoid(c), w_ref[...]) + b_ref[...]


def _ada(c_pad, w_ada, b_ada):
    n_col = N_ADA * D_MODEL
    blk = 1536
    return pl.pallas_call(
        _ada_kernel,
        out_shape=jax.ShapeDtypeStruct((SUBLANES, n_col), F32),
        grid=(n_col // blk,),
        in_specs=[
            pl.BlockSpec((SUBLANES, D_MODEL), lambda j: (0, 0)),
            pl.BlockSpec((D_MODEL, blk), lambda j: (0, j)),
            pl.BlockSpec((1, blk), lambda j: (0, j)),
        ],
        out_specs=pl.BlockSpec((SUBLANES, blk), lambda j: (0, j)),
        compiler_params=pltpu.CompilerParams(dimension_semantics=("arbitrary",), vmem_limit_bytes=VMEM_LIMIT),
        name="ada",
    )(c_pad, w_ada, b_ada)


def _gla_tile(q, k, v, la, st_ref, forward):
    tt = q.shape[0]
    n_chunk = tt // GLA_CHUNK
    row = lax.broadcasted_iota(jnp.int32, (tt, tt), 0)
    col = lax.broadcasted_iota(jnp.int32, (tt, tt), 1)
    ordered = (row >= col) if forward else (row <= col)
    causal = ordered & (row // GLA_CHUNK == col // GLA_CHUNK)
    tri = jnp.where(causal, 1.0, 0.0).astype(BF16)
    la_hi, la_mid, la_lo = _split3(la)
    b = _dot(tri, la_hi) + _dot(tri, la_mid) + _dot(tri, la_lo)
    end_rows = [ci * GLA_CHUNK + (GLA_CHUNK - 1 if forward else 0) for ci in range(n_chunk)]
    b_end = jnp.concatenate([jnp.broadcast_to(b[r:r + 1, :], (GLA_CHUNK, GLA_KW)) for r in end_rows], axis=0)
    q_dec = q * (jnp.exp(b) * (GLA_DK ** -0.5))
    k_dec = k * jnp.exp(-b)
    k_end = k * jnp.exp(b_end - b)
    decay = [jnp.exp(b[r:r + 1, :]) for r in end_rows]
    lane = lax.broadcasted_iota(jnp.int32, (1, LANES), 1)
    scan = range(n_chunk) if forward else range(n_chunk - 1, -1, -1)
    outs = []
    for head in range(GLA_HEADS):
        pair, half = head // 2, head % 2
        sl = slice(pair * LANES, (pair + 1) * LANES)
        in_head = (lane // GLA_DK) == half
        qd = jnp.where(in_head, q_dec[:, sl], 0.0).astype(BF16)
        kd = k_dec[:, sl].astype(BF16)
        ke = k_end[:, sl].astype(BF16)
        v_h = v[:, head * GLA_DV:(head + 1) * GLA_DV].astype(BF16)
        scores = jnp.where(causal, _dot_nt(qd, kd), 0.0)
        o_intra = _dot(scores.astype(BF16), v_h)
        st = st_ref[head]
        parts = [None] * n_chunk
        for ci in scan:
            rows = slice(ci * GLA_CHUNK, (ci + 1) * GLA_CHUNK)
            parts[ci] = o_intra[rows] + _dot_nt(qd[rows], st.astype(BF16))
            st = st * decay[ci][:, sl] + _dot_tn(v_h[rows], ke[rows])
        st_ref[head] = st
        outs.append(jnp.concatenate(parts, axis=0))
    return jnp.concatenate(outs, axis=-1)


def _mix1_kernel(x_ref, xp_ref, xn_ref, mod_ref, n1g_ref, win_ref, wdec_ref, bdec_ref, convw_ref, cng_ref,
                 mgrp_ref, qk_ref, v_ref, lab_ref, of_ref, gy_ref, st_s, *, tt, nt, nb):
    i = pl.program_id(1)

    @pl.when(i == 0)
    def _():
        st_s[...] = jnp.zeros_like(st_s)

    n1g = n1g_ref[...]
    cw = convw_ref[...]
    row = lax.broadcasted_iota(jnp.int32, (tt, 1), 0)
    for bi in range(nb):
        mod = mod_ref[bi]
        sh1, sc1 = mod[0:1], mod[1:2]

        def norm_mod(x):
            return ((x * _rms(x)) * n1g) * (1.0 + sc1) + sh1

        h = norm_mod(x_ref[bi]).astype(BF16)
        qk_ref[bi] = _dot(h, win_ref[:, C_QK:C_V])
        v_ref[bi] = _dot(h, win_ref[:, C_V:C_G])
        gy_ref[bi, :, 0:GLA_WIDTH] = _dot(h, win_ref[:, C_G:C_CB])

        hh = norm_mod(jnp.concatenate([xp_ref[bi], xn_ref[bi]], axis=0)).astype(BF16)
        h_ext = jnp.concatenate([h, hh], axis=0)
        z_ext = _dot(h_ext, win_ref[:, C_CC:C_CH]) * _dot(h_ext, win_ref[:, C_CH:C_AL])
        z = z_ext[0:tt]
        z_prev = jnp.where(i > 0, z_ext[tt + SUBLANES - 1:tt + SUBLANES, :], 0.0)
        z_next = jnp.where(i < nt - 1, z_ext[tt + SUBLANES:tt + SUBLANES + 1, :], 0.0)
        z_m1 = jnp.where(row == 0, z_prev, pltpu.roll(z, 1, axis=0))
        z_p1 = jnp.where(row == tt - 1, z_next, pltpu.roll(z, tt - 1, axis=0))
        conv = cw[0:1] * z_m1 + cw[1:2] * z + cw[2:3] * z_p1
        yc = _dot(h, win_ref[:, C_CB:C_CC]) * conv
        sq_hi, sq_lo = _split2(yc * yc)
        ss = _dot(sq_hi, mgrp_ref[...]) + _dot(sq_lo, mgrp_ref[...])
        gy_ref[bi, :, GLA_WIDTH:] = (yc * lax.rsqrt(ss * (1.0 / CONV_GDIM) + EPS)) * cng_ref[...]

        zd = _dot_f32(_dot(h, win_ref[:, C_AL:C_END]), wdec_ref[...]) + bdec_ref[...]
        la = _log_sigmoid(zd) * (1.0 / GLA_GATE_NORM)
        lab_ref[bi] = la[:, GLA_KW:]
        of_ref[bi] = _gla_tile(qk_ref[bi, :, 0:GLA_KW], qk_ref[bi, :, GLA_KW:], v_ref[bi], la[:, 0:GLA_KW],
                               st_s.at[bi], True)


def _mix1(x, mod, n1g, win, wdec, bdec, convw, cng, mgrp, tt, nb):
    bsz, seq, _ = x.shape
    nt = seq // tt
    hb = tt // SUBLANES
    last_hb = seq // SUBLANES - 1
    const = lambda shape: pl.BlockSpec(shape, lambda b, i: tuple(0 for _ in shape))
    tile = lambda w: pl.BlockSpec((nb, tt, w), lambda b, i: (b, i, 0))
    out_w = (2 * GLA_KW, GLA_WIDTH, GLA_KW, GLA_WIDTH, GLA_WIDTH + CONV_WIDTH)
    return pl.pallas_call(
        functools.partial(_mix1_kernel, tt=tt, nt=nt, nb=nb),
        out_shape=tuple(jax.ShapeDtypeStruct((bsz, seq, w), F32) for w in out_w),
        grid=(bsz // nb, nt),
        in_specs=[
            tile(D_MODEL),
            pl.BlockSpec((nb, SUBLANES, D_MODEL), lambda b, i: (b, jnp.maximum(i * hb - 1, 0), 0)),
            pl.BlockSpec((nb, SUBLANES, D_MODEL), lambda b, i: (b, jnp.minimum((i + 1) * hb, last_hb), 0)),
            pl.BlockSpec((nb, SUBLANES, D_MODEL), lambda b, i: (b, 0, 0)),
            const((1, D_MODEL)),
            const((D_MODEL, C_END)),
            const((LANES, 2 * GLA_KW)),
            const((1, 2 * GLA_KW)),
            const((3, CONV_WIDTH)),
            const((1, CONV_WIDTH)),
            const((CONV_WIDTH, CONV_WIDTH)),
        ],
        out_specs=tuple(tile(w) for w in out_w),
        scratch_shapes=[pltpu.VMEM((nb, GLA_HEADS, GLA_DV, LANES), F32)],
        compiler_params=pltpu.CompilerParams(dimension_semantics=("arbitrary", "arbitrary"),
                                             vmem_limit_bytes=VMEM_LIMIT),
        name="mix1",
    )(x, x, x, mod, n1g, win, wdec, bdec, convw, cng, mgrp)


def _mix2_kernel(x_ref, mod_ref, qk_ref, v_ref, lab_ref, of_ref, gy_ref, gng_ref, wout_ref, n2g_ref,
                 x1_ref, h2_ref, h8_ref, hs_ref, st_s, *, nb):
    i = pl.program_id(1)

    @pl.when(i == 0)
    def _():
        st_s[...] = jnp.zeros_like(st_s)

    gng = gng_ref[...]
    for bi in range(nb):
        mod = mod_ref[bi]
        gt1, sh2, sc2 = mod[2:3], mod[3:4], mod[4:5]
        o = of_ref[bi] + _gla_tile(qk_ref[bi, :, 0:GLA_KW], qk_ref[bi, :, GLA_KW:], v_ref[bi], lab_ref[bi],
                                   st_s.at[bi], False)
        y = None
        for head in range(GLA_HEADS):
            sl = slice(head * GLA_DV, (head + 1) * GLA_DV)
            oh = o[:, sl]
            g = gy_ref[bi, :, sl]
            yh = ((oh * _rms(oh)) * gng) * (g * _sigmoid(g))
            part = _dot(yh.astype(BF16), wout_ref[sl, :])
            y = part if y is None else y + part
        y = y + _dot(gy_ref[bi, :, GLA_WIDTH:].astype(BF16), wout_ref[GLA_WIDTH:, :])
        x1 = x_ref[bi] + gt1 * y
        x1_ref[bi] = x1
        h2 = ((x1 * _rms(x1)) * n2g_ref[...]) * (1.0 + sc2) + sh2
        h2_ref[bi] = h2.astype(BF16)
        amax = jnp.maximum(jnp.max(jnp.abs(h2), axis=-1, keepdims=True), F8_TINY)
        h8_ref[bi] = (h2 * (F8_TOP / amax)).astype(F8)
        hs_ref[bi] = jnp.broadcast_to(amax * (1.0 / F8_TOP), (h2.shape[0], LANES))


def _mix2(x, mod, qk, v, lab, of, gy, gng, wout, n2g, tt, nb):
    bsz, seq, _ = x.shape
    nt = seq // tt
    const = lambda shape: pl.BlockSpec(shape, lambda b, i: tuple(0 for _ in shape))
    tile = lambda w: pl.BlockSpec((nb, tt, w), lambda b, i: (b, nt - 1 - i, 0))
    return pl.pallas_call(
        functools.partial(_mix2_kernel, nb=nb),
        out_shape=(jax.ShapeDtypeStruct((bsz, seq, D_MODEL), F32), jax.ShapeDtypeStruct((bsz, seq, D_MODEL), BF16),
                   jax.ShapeDtypeStruct((bsz, seq, D_MODEL), F8), jax.ShapeDtypeStruct((bsz, seq, LANES), F32)),
        grid=(bsz // nb, nt),
        in_specs=[
            tile(D_MODEL),
            pl.BlockSpec((nb, SUBLANES, D_MODEL), lambda b, i: (b, 0, 0)),
            tile(2 * GLA_KW), tile(GLA_WIDTH), tile(GLA_KW), tile(GLA_WIDTH), tile(GLA_WIDTH + CONV_WIDTH),
            const((1, GLA_DV)),
            const((D_MODEL, D_MODEL)),
            const((1, D_MODEL)),
        ],
        out_specs=(tile(D_MODEL), tile(D_MODEL), tile(D_MODEL), tile(LANES)),
        scratch_shapes=[pltpu.VMEM((nb, GLA_HEADS, GLA_DV, LANES), F32)],
        compiler_params=pltpu.CompilerParams(dimension_semantics=("arbitrary", "arbitrary"),
                                             vmem_limit_bytes=VMEM_LIMIT),
        name="mix2",
    )(x, mod, qk, v, lab, of, gy, gng, wout, n2g)


NEG_INF = float("-inf")
KEY_ROWS = 2 * PEER_NKEYS * PEER_HEADS
VREG = (SUBLANES, LANES)


def _keyproj_kernel(wq_ref, sk_ref, o_ref):
    w_hi, w_lo = _split2(wq_ref[...])
    k_hi, k_lo = _split2(sk_ref[0])
    o_ref[0] = _dot_nt(k_hi, w_hi) + _dot_nt(k_hi, w_lo) + _dot_nt(k_lo, w_hi)


def _keyproj(wq, subkeys):
    n_hp = 2 * PEER_HEADS
    return pl.pallas_call(
        _keyproj_kernel,
        out_shape=jax.ShapeDtypeStruct((n_hp, PEER_NKEYS, D_MODEL), F32),
        grid=(n_hp,),
        in_specs=[pl.BlockSpec((D_MODEL, PEER_DQH), lambda j: (0, j)),
                  pl.BlockSpec((1, PEER_NKEYS, PEER_DQH), lambda j: (j, 0, 0))],
        out_specs=pl.BlockSpec((1, PEER_NKEYS, D_MODEL), lambda j: (j, 0, 0)),
        compiler_params=pltpu.CompilerParams(dimension_semantics=("arbitrary",), vmem_limit_bytes=VMEM_LIMIT),
        name="keyproj",
    )(wq, subkeys)


def _sort16_network():
    pairs = []

    def merge(lo, hi, r):
        step = r * 2
        if step < hi - lo:
            merge(lo, hi, step)
            merge(lo + r, hi, step)
            pairs.extend((i, i + r) for i in range(lo + r, hi - r, step))
        else:
            pairs.append((lo, lo + r))

    def sort(lo, hi):
        if hi - lo >= 1:
            mid = lo + (hi - lo) // 2
            sort(lo, mid)
            sort(mid + 1, hi)
            merge(lo, hi, 1)

    sort(0, PEER_TOPK - 1)
    return tuple(pairs)


SORT16 = _sort16_network()
CAND_LISTS = ([[(0, k2) for k2 in range(PEER_TOPK)]]
              + [[(k1, k2) for k2 in range(PEER_TOPK // (k1 + 1))] for k1 in range(1, 8)]
              + [[(k1, 0) for k1 in range(8, PEER_TOPK)]])
CANDS = [c for lst in CAND_LISTS for c in lst]


def _tree(op, xs):
    xs = list(xs)
    while len(xs) > 1:
        xs = [op(xs[i], xs[i + 1]) for i in range(0, len(xs) - 1, 2)] + ([xs[-1]] if len(xs) % 2 else [])
    return xs[0]


def _ce(v, x, i, j):
    c = v[j] > v[i]
    v[i], v[j] = jnp.maximum(v[i], v[j]), jnp.minimum(v[i], v[j])
    x[i], x[j] = jnp.where(c, x[j], x[i]), jnp.where(c, x[i], x[j])


def _merge_top16(va, xa, vb, xb, sort_result):
    v, x = list(va), list(xa)
    for j in range(PEER_TOPK - len(vb), PEER_TOPK):
        b = PEER_TOPK - 1 - j
        c = vb[b] > va[j]
        v[j] = jnp.maximum(va[j], vb[b])
        x[j] = jnp.where(c, xb[b], xa[j])
    if sort_result:
        for d in (8, 4, 2, 1):
            for i in range(PEER_TOPK):
                if not i & d:
                    _ce(v, x, i, i + d)
    return v, x


def _any(flag):
    return jnp.max(flag) > 0.0


def _top16_keys(key, work_v, work_x, sv_ref, si_ref):
    for g in range(PEER_NKEYS // PEER_TOPK):
        v = [key(PEER_TOPK * g + i) for i in range(PEER_TOPK)]
        x = [float(PEER_TOPK * g + i) for i in range(PEER_TOPK)]
        for i, j in SORT16:
            _ce(v, x, i, j)
        for i in range(PEER_TOPK):
            work_v[PEER_TOPK * g + i] = v[i]
            work_x[PEER_TOPK * g + i] = x[i]
    for span in (1, 2, 4):
        for g in range(0, PEER_NKEYS // PEER_TOPK, 2 * span):
            a, b = PEER_TOPK * g, PEER_TOPK * (g + span)
            v, x = _merge_top16([work_v[a + i] for i in range(PEER_TOPK)], [work_x[a + i] for i in range(PEER_TOPK)],
                                [work_v[b + i] for i in range(PEER_TOPK)], [work_x[b + i] for i in range(PEER_TOPK)],
                                True)
            for i in range(PEER_TOPK):
                if span == 4:
                    sv_ref[i], si_ref[i] = v[i], x[i]
                else:
                    work_v[a + i], work_x[a + i] = v[i], x[i]
    repeat = _tree(jnp.maximum, [jnp.where(v[r] == v[r + 1], 1.0, 0.0) for r in range(PEER_TOPK - 1)])
    n_ge = _tree(jnp.add, [jnp.where(key(k) >= v[PEER_TOPK - 1], 1.0, 0.0) for k in range(PEER_NKEYS)])

    @pl.when(_any(jnp.maximum(repeat, jnp.where(n_ge > float(PEER_TOPK), 1.0, 0.0))))
    def _():
        for k in range(PEER_NKEYS):
            work_v[k] = key(k)

        def extract(r, carry):
            vals = [work_v[k] for k in range(PEER_NKEYS)]
            m = _tree(jnp.maximum, vals)
            idx = _tree(jnp.minimum, [jnp.where(vals[k] == m, float(k), float(PEER_NKEYS)) for k in range(PEER_NKEYS)])
            for k in range(PEER_NKEYS):
                work_v[k] = jnp.where(idx == float(k), NEG_INF, vals[k])
            sv_ref[r] = m
            si_ref[r] = idx
            return carry

        lax.fori_loop(0, PEER_TOPK, extract, 0)


def _top16_sums(sv_s, si_s, work_v, work_x, rv_s, re_s):
    sv0 = [sv_s[0, r] for r in range(PEER_TOPK)]
    sv1 = [sv_s[1, r] for r in range(PEER_TOPK)]
    e_hi = [si_s[0, r] * float(PEER_NKEYS) for r in range(PEER_TOPK)]
    si1 = [si_s[1, r] for r in range(PEER_TOPK)]
    val = {c: sv0[c[0]] + sv1[c[1]] for c in CANDS}
    eid = {c: e_hi[c[0]] + si1[c[1]] for c in CANDS}
    v, x = [val[c] for c in CAND_LISTS[0]], [eid[c] for c in CAND_LISTS[0]]
    for n, lst in enumerate(CAND_LISTS[1:]):
        v, x = _merge_top16(v, x, [val[c] for c in lst], [eid[c] for c in lst], n < len(CAND_LISTS) - 2)
    for r in range(PEER_TOPK):
        rv_s[r], re_s[r] = v[r], x[r]
    low = _tree(jnp.minimum, v)
    n_ge = _tree(jnp.add, [jnp.where(val[c] >= low, 1.0, 0.0) for c in CANDS])

    @pl.when(_any(jnp.where(n_ge > float(PEER_TOPK), 1.0, 0.0)))
    def _():
        for n, c in enumerate(CANDS):
            work_v[n], work_x[n] = val[c], eid[c]

        def extract(r, carry):
            vals = [work_v[n] for n in range(len(CANDS))]
            m = _tree(jnp.maximum, vals)
            code = [float(c[0] * PEER_TOPK + c[1]) for c in CANDS]
            sel = _tree(jnp.minimum, [jnp.where(vals[n] == m, code[n], 1e9) for n in range(len(CANDS))])
            hit = [sel == code[n] for n in range(len(CANDS))]
            rv_s[r] = m
            re_s[r] = _tree(jnp.add, [jnp.where(hit[n], work_x[n], 0.0) for n in range(len(CANDS))])
            for n in range(len(CANDS)):
                work_v[n] = jnp.where(hit[n], NEG_INF, vals[n])
            return carry

        lax.fori_loop(0, PEER_TOPK, extract, 0)


def _route_kernel(h2_ref, wpt_ref, a_ref, b_ref, g_ref, sc_s, work_v, work_x, sv_s, si_s, rv_s, re_s,
                  pa_s, pb_s, pg_s, *, tt):
    sc_s[...] = _dot_nt(wpt_ref[...], h2_ref[0])

    def lane_tile(c, carry):
        lanes = pl.ds(pl.multiple_of(c * LANES, LANES), LANES)
        for p in range(2):
            def key(k, p=p):
                r0 = (p * PEER_NKEYS + k) * SUBLANES
                return sc_s[r0:r0 + SUBLANES, lanes]
            _top16_keys(key, work_v, work_x, sv_s.at[p], si_s.at[p])
        _top16_sums(sv_s, si_s, work_v, work_x, rv_s, re_s)
        v = [rv_s[r] for r in range(PEER_TOPK)]
        m = _tree(jnp.maximum, v)
        ex = [jnp.exp(vr - m) for vr in v]
        inv = 1.0 / _tree(jnp.add, ex)
        for r in range(PEER_TOPK):
            e = re_s[r]
            i1 = jnp.floor(e * (1.0 / PEER_NKEYS))
            rows = slice(r * SUBLANES, (r + 1) * SUBLANES)
            pa_s[rows, :] = i1
            pb_s[rows, :] = e - i1 * float(PEER_NKEYS)
            pg_s[rows, :] = ex[r] * inv
        toks = pl.ds(pl.multiple_of(c * LANES, LANES), LANES)
        a_ref[0, toks, :] = jnp.transpose(pa_s[...])
        b_ref[0, toks, :] = jnp.transpose(pb_s[...])
        g_ref[0, toks, :] = jnp.transpose(pg_s[...])
        return carry

    lax.fori_loop(0, tt // LANES, lane_tile, 0)


def _route(h2, wpt, tt):
    bsz, seq, _ = h2.shape
    n_pair = PEER_HEADS * PEER_TOPK
    tile = lambda w: pl.BlockSpec((1, tt, w), lambda b, i: (b, i, 0))
    vregs = lambda *lead: pltpu.VMEM(lead + VREG, F32)
    return pl.pallas_call(
        functools.partial(_route_kernel, tt=tt),
        out_shape=tuple(jax.ShapeDtypeStruct((bsz, seq, n_pair), F32) for _ in range(3)),
        grid=(bsz, seq // tt),
        in_specs=[tile(D_MODEL), pl.BlockSpec((KEY_ROWS, D_MODEL), lambda b, i: (0, 0))],
        out_specs=(tile(n_pair), tile(n_pair), tile(n_pair)),
        scratch_shapes=[pltpu.VMEM((KEY_ROWS, tt), F32), vregs(PEER_NKEYS), vregs(PEER_NKEYS),
                        vregs(2, PEER_TOPK), vregs(2, PEER_TOPK), vregs(PEER_TOPK), vregs(PEER_TOPK),
                        pltpu.VMEM((n_pair, LANES), F32), pltpu.VMEM((n_pair, LANES), F32),
                        pltpu.VMEM((n_pair, LANES), F32)],
        compiler_params=pltpu.CompilerParams(dimension_semantics=("arbitrary", "arbitrary"),
                                             vmem_limit_bytes=VMEM_LIMIT),
        name="route",
    )(h2, wpt)


def _peer_kernel(h8_ref, hs_ref, x1_ref, mod_ref, a_ref, b_ref, g_ref, ut_ref, us_ref, v_ref, nfg_ref, y_ref,
                 gate_s, acc_s, *, tt, eb, ne):
    j = pl.program_id(2)
    n_grp = eb // PEER_NKEYS
    steps_per_half = ne // 2

    @pl.when(j == 0)
    def _():
        acc_s[...] = jnp.zeros_like(acc_s)
        sub = lax.broadcasted_iota(jnp.int32, (PEER_NKEYS, PEER_NKEYS), 0).astype(F32).astype(BF16)
        zero = jnp.zeros((PEER_NKEYS, PEER_NKEYS), BF16)
        half = jnp.full((PEER_NKEYS, PEER_NKEYS), 0.5, BF16)

        def onehots(t):
            bc = lambda ref: jnp.broadcast_to(ref[0, pl.ds(t, 1), :], (PEER_NKEYS, PEER_NKEYS)).astype(BF16)
            at = jnp.where(sub == bc(a_ref), half, zero)
            cbt = jnp.where(sub == bc(b_ref), bc(g_ref), zero)
            return at, cbt

        def build(p, carry):
            for u in range(GATE_UNROLL):
                t0 = (p * GATE_UNROLL + u) * 2
                at0, cbt0 = onehots(t0)
                at1, cbt1 = onehots(t0 + 1)
                lhs = jnp.concatenate([at0, at1], axis=1)
                rhs = jnp.concatenate([jnp.concatenate([cbt0.T, zero], axis=1),
                                       jnp.concatenate([zero, cbt1.T], axis=1)], axis=0)
                tiles = _dot(lhs, rhs)
                for k in range(2):
                    lo = tiles[0:GATE_HALF, k * PEER_NKEYS:(k + 1) * PEER_NKEYS]
                    hi = tiles[GATE_HALF:, k * PEER_NKEYS:(k + 1) * PEER_NKEYS]
                    r0 = pl.multiple_of((t0 + k) * GATE_PITCH, SUBLANES)
                    gate_s[pl.ds(r0, GATE_HALF), :] = (_bf16_bits(lo) >> 16) | _bf16_bits(hi)
            return carry

        lax.fori_loop(0, tt // (2 * GATE_UNROLL), build, 0)

    def expert_block(half):
        acc = acc_s[...]
        for sub in range(eb // EXPERT_SUB):
            cols = slice(sub * EXPERT_SUB, (sub + 1) * EXPERT_SUB)
            s = _dot(h8_ref[0], ut_ref[:, cols])
            row_scale = hs_ref[0]
            parts = []
            for gi in range(EXPERT_SUB // PEER_NKEYS):
                g_abs = sub * (EXPERT_SUB // PEER_NKEYS) + gi
                lanes = slice(g_abs * PEER_NKEYS, (g_abs + 1) * PEER_NKEYS)
                r = (j - half * steps_per_half) * n_grp + g_abs
                packed = gate_s[pl.ds(r, tt, stride=GATE_PITCH), :]
                word = (packed & jnp.uint32(0xFFFF0000)) if half else (packed << 16)
                gate = lax.bitcast_convert_type(word, F32)
                act = s[:, gi * PEER_NKEYS:(gi + 1) * PEER_NKEYS] * (row_scale * us_ref[:, lanes])
                parts.append(_half_gated_gelu(act, gate).astype(BF16))
            acc = acc + _dot(jnp.concatenate(parts, axis=1), v_ref[cols, :])
        acc_s[...] = acc

    @pl.when(j < steps_per_half)
    def _():
        expert_block(0)

    @pl.when(j >= steps_per_half)
    def _():
        expert_block(1)

    @pl.when(j == ne - 1)
    def _():
        gt2 = mod_ref[0][5:6]
        x2 = x1_ref[0] + gt2 * acc_s[...]
        y_ref[0] = (x2 * _rms(x2)) * nfg_ref[...]


def _peer(h8, hs, x1, mod, a_idx, b_idx, gates, ut8, us, v, nfg, tt, eb):
    bsz, seq, _ = h8.shape
    n_exp = v.shape[0]
    ne = n_exp // eb
    assert ne % 2 == 0 and (GATE_HALF * PEER_NKEYS) % eb == 0 and tt % (2 * GATE_UNROLL) == 0
    n_pair = PEER_HEADS * PEER_TOPK
    tile = lambda w: pl.BlockSpec((1, tt, w), lambda b, i, j: (b, i, 0))
    return pl.pallas_call(
        functools.partial(_peer_kernel, tt=tt, eb=eb, ne=ne),
        out_shape=jax.ShapeDtypeStruct((bsz, seq, D_MODEL), F32),
        grid=(bsz, seq // tt, ne),
        in_specs=[
            tile(D_MODEL), tile(LANES), tile(D_MODEL),
            pl.BlockSpec((1, SUBLANES, D_MODEL), lambda b, i, j: (b, 0, 0)),
            tile(n_pair), tile(n_pair), tile(n_pair),
            pl.BlockSpec((D_MODEL, eb), lambda b, i, j: (0, j)),
            pl.BlockSpec((1, eb), lambda b, i, j: (0, j)),
            pl.BlockSpec((eb, D_MODEL), lambda b, i, j: (j, 0)),
            pl.BlockSpec((1, D_MODEL), lambda b, i, j: (0, 0)),
        ],
        out_specs=tile(D_MODEL),
        scratch_shapes=[pltpu.VMEM((tt * GATE_PITCH, PEER_NKEYS), jnp.uint32), pltpu.VMEM((tt, D_MODEL), F32)],
        compiler_params=pltpu.CompilerParams(dimension_semantics=("arbitrary", "arbitrary", "arbitrary"),
                                             vmem_limit_bytes=VMEM_LIMIT),
        name="peer",
    )(h8, hs, x1, mod, a_idx, b_idx, gates, ut8, us, v, nfg)


def _uquant_kernel(ut_ref, u8_ref, us_ref):
    ut = ut_ref[...]
    amax = jnp.maximum(jnp.max(jnp.abs(ut), axis=0, keepdims=True), F8_TINY)
    u8_ref[...] = (ut * (F8_TOP / amax)).astype(F8)
    us_ref[...] = amax * (1.0 / F8_TOP)


def _uquant(ut, eb):
    d, n_exp = ut.shape
    return pl.pallas_call(
        _uquant_kernel,
        out_shape=(jax.ShapeDtypeStruct((d, n_exp), F8), jax.ShapeDtypeStruct((1, n_exp), F32)),
        grid=(n_exp // eb,),
        in_specs=[pl.BlockSpec((d, eb), lambda j: (0, j))],
        out_specs=(pl.BlockSpec((d, eb), lambda j: (0, j)), pl.BlockSpec((1, eb), lambda j: (0, j))),
        compiler_params=pltpu.CompilerParams(dimension_semantics=("arbitrary",), vmem_limit_bytes=VMEM_LIMIT),
        name="uquant",
    )(ut)


def _prep_weights(norm1_g, norm2_g, w_in, w_dec_f, b_dec_f, w_dec_b, b_dec_b, gla_norm_g, conv_w, conv_norm_g,
                  w_out, peer_wq, peer_subkeys, peer_u, peer_v, normf_g):
    w = w_in[0]
    q, k, v, g, alf, alb, cb, cc, ch = jnp.split(w, (256, 512, 1024, 1536, 1552, 1568, 2080, 2592), axis=-1)
    pad = jnp.zeros((D_MODEL, C_END - C_AL - 2 * GLA_RANK), F32)
    win = jnp.concatenate([q, k, v, g, cb, cc, ch, alf, alb, pad], axis=-1).astype(BF16)
    wdec = jnp.zeros((LANES, 2 * GLA_KW), F32)
    wdec = wdec.at[0:GLA_RANK, 0:GLA_KW].set(w_dec_f[0]).at[GLA_RANK:2 * GLA_RANK, GLA_KW:].set(w_dec_b[0])
    bdec = jnp.concatenate([b_dec_f[0], b_dec_b[0]])[None, :]
    grp = jnp.arange(CONV_WIDTH) // CONV_GDIM
    mgrp = (grp[:, None] == grp[None, :]).astype(BF16)
    wp = _keyproj(peer_wq[0], peer_subkeys[0].reshape(2 * PEER_HEADS, PEER_NKEYS, PEER_DQH))
    wpt = wp.reshape(PEER_HEADS, 2, PEER_NKEYS, D_MODEL).transpose(1, 2, 0, 3).reshape(KEY_ROWS, D_MODEL)
    ut8, us = _uquant(jnp.transpose(peer_u[0]), EXPERT_BLOCK)
    return dict(
        n1g=norm1_g[0][None, :], n2g=norm2_g[0][None, :], win=win, wdec=wdec, bdec=bdec,
        convw=conv_w[0], cng=conv_norm_g[0][None, :], mgrp=mgrp, gng=gla_norm_g[0][None, :],
        wout=w_out[0].astype(BF16), wpt=wpt.astype(BF16),
        ut8=ut8, us=us, v=peer_v[0].astype(BF16), nfg=normf_g[None, :],
    )


def _trunk(x, mod, w, tt, rtt, ptt, eb):
    nb = MIX_ROWS
    qk, v, lab, of, gy = _mix1(x, mod, w["n1g"], w["win"], w["wdec"], w["bdec"], w["convw"], w["cng"], w["mgrp"],
                               tt, nb)
    x1, h2, h8, hs = _mix2(x, mod, qk, v, lab, of, gy, w["gng"], w["wout"], w["n2g"], tt, nb)
    a_idx, b_idx, gates = _route(h2, w["wpt"], rtt)
    return _peer(h8, hs, x1, mod, a_idx, b_idx, gates, w["ut8"], w["us"], w["v"], w["nfg"], ptt, eb)


def kernel(x_prompt, x_sample, c_prompt, c_sample, norm1_g, norm2_g, w_ada, b_ada, w_in, w_dec_f, b_dec_f,
           w_dec_b, b_dec_b, gla_norm_g, conv_w, conv_norm_g, w_out, peer_wq, peer_subkeys, peer_u, peer_v,
           normf_g):
    w = _prep_weights(norm1_g, norm2_g, w_in, w_dec_f, b_dec_f, w_dec_b, b_dec_b, gla_norm_g, conv_w,
                      conv_norm_g, w_out, peer_wq, peer_subkeys, peer_u, peer_v, normf_g)
    nb_p, nb_s = c_prompt.shape[0], c_sample.shape[0]
    c_all = jnp.concatenate([c_prompt, c_sample, jnp.zeros((SUBLANES - nb_p - nb_s, D_MODEL), F32)], axis=0)
    ada = _ada(c_all, w_ada[0], b_ada[0][None, :])
    mod = jnp.pad(ada.reshape(SUBLANES, N_ADA, D_MODEL), ((0, 0), (0, SUBLANES - N_ADA), (0, 0)))
    y_prompt = _trunk(x_prompt, mod[:nb_p], w, TOKEN_TILE, ROUTE_TILE, PEER_TOKEN_TILE, EXPERT_BLOCK)
    y_sample = _trunk(x_sample, mod[nb_p:nb_p + nb_s], w, TOKEN_TILE, ROUTE_TILE, PEER_TOKEN_TILE, EXPERT_BLOCK)
    return (y_prompt, y_sample)
```

```python
import functools

import jax
import jax.numpy as jnp
from jax import lax
from jax.experimental import pallas as pl
from jax.experimental.pallas import tpu as pltpu

F32 = jnp.float32
BF16 = jnp.bfloat16
F8 = jnp.float8_e4m3fn
F8_TOP = 224.0
F8_TINY = 1e-30

D_MODEL = 1024
GLA_HEADS = 4
GLA_DV = 128
GLA_DK = 64
GLA_KW = GLA_HEADS * GLA_DK
GLA_WIDTH = GLA_HEADS * GLA_DV
GLA_RANK = 16
GLA_GATE_NORM = 16.0
GLA_CHUNK = 64
CONV_WIDTH = 512
CONV_GDIM = 64
PEER_HEADS = 8
PEER_NKEYS = 128
PEER_DQH = 128
PEER_TOPK = 16
N_ADA = 6
EPS = 1e-6

LANES = 128
SUBLANES = 8
VMEM_LIMIT = 56 * 1024 * 1024

C_QK, C_V, C_G, C_CB, C_CC, C_CH, C_AL, C_END = 0, 512, 1024, 1536, 2048, 2560, 3072, 3200

TOKEN_TILE = 256
ROUTE_TILE = 512
PEER_TOKEN_TILE = 512
EXPERT_BLOCK = 2048
GATE_HALF = PEER_NKEYS // 2
GATE_PITCH = GATE_HALF + SUBLANES
GATE_UNROLL = 64


def _dot(a, b):
    return jnp.dot(a, b, preferred_element_type=F32)


def _dot_nt(a, b):
    return lax.dot_general(a, b, (((1,), (1,)), ((), ())), preferred_element_type=F32)


def _dot_tn(a, b):
    return lax.dot_general(a, b, (((0,), (0,)), ((), ())), preferred_element_type=F32)


def _split2(x):
    hi = x.astype(BF16)
    lo = (x - hi.astype(F32)).astype(BF16)
    return hi, lo


def _split3(x):
    hi = x.astype(BF16)
    r = x - hi.astype(F32)
    mid = r.astype(BF16)
    lo = (r - mid.astype(F32)).astype(BF16)
    return hi, mid, lo


def _dot_f32(a, b):
    a_hi, a_lo = _split2(a)
    b_hi, b_lo = _split2(b)
    return _dot(a_hi, b_hi) + _dot(a_hi, b_lo) + _dot(a_lo, b_hi)


def _bf16_bits(x):
    return lax.bitcast_convert_type(x.astype(BF16).astype(F32), jnp.uint32)


def _rms(x):
    return lax.rsqrt(jnp.mean(x * x, axis=-1, keepdims=True) + EPS)


def _sigmoid(x):
    return 1.0 / (1.0 + jnp.exp(-x))


def _log_sigmoid(x):
    return jnp.minimum(x, 0.0) - jnp.log1p(jnp.exp(-jnp.abs(x)))


GELU_C1 = 0.7978845608028654
GELU_C3 = GELU_C1 * 0.044715


def _half_gated_gelu(x, half_gate):
    gx = half_gate * x
    return gx * jnp.tanh(x * (GELU_C1 + GELU_C3 * (x * x))) + gx


def _ada_kernel(c_ref, w_ref, b_ref, o_ref):
    c = c_ref[...]
    o_ref[...] = _dot_f32(c * _sigmoid(c), w_ref[...]) + b_ref[...]


def _ada(c_pad, w_ada, b_ada):
    n_col = N_ADA * D_MODEL
    blk = 1536
    return pl.pallas_call(
        _ada_kernel,
        out_shape=jax.ShapeDtypeStruct((SUBLANES, n_col), F32),
        grid=(n_col // blk,),
        in_specs=[
            pl.BlockSpec((SUBLANES, D_MODEL), lambda j: (0, 0)),
            pl.BlockSpec((D_MODEL, blk), lambda j: (0, j)),
            pl.BlockSpec((1, blk), lambda j: (0, j)),
        ],
        out_specs=pl.BlockSpec((SUBLANES, blk), lambda j: (0, j)),
        compiler_params=pltpu.CompilerParams(dimension_semantics=("arbitrary",), vmem_limit_bytes=VMEM_LIMIT),
        name="ada",
    )(c_pad, w_ada, b_ada)


def _gla_tile(q, k, v, la, st_ref, forward):
    tt = q.shape[0]
    n_chunk = tt // GLA_CHUNK
    row = lax.broadcasted_iota(jnp.int32, (tt, tt), 0)
    col = lax.broadcasted_iota(jnp.int32, (tt, tt), 1)
    ordered = (row >= col) if forward else (row <= col)
    causal = ordered & (row // GLA_CHUNK == col // GLA_CHUNK)
    tri = jnp.where(causal, 1.0, 0.0).astype(BF16)
    la_hi, la_mid, la_lo = _split3(la)
    b = _dot(tri, la_hi) + _dot(tri, la_mid) + _dot(tri, la_lo)
    end_rows = [ci * GLA_CHUNK + (GLA_CHUNK - 1 if forward else 0) for ci in range(n_chunk)]
    b_end = jnp.concatenate([jnp.broadcast_to(b[r:r + 1, :], (GLA_CHUNK, GLA_KW)) for r in end_rows], axis=0)
    q_dec = q * (jnp.exp(b) * (GLA_DK ** -0.5))
    k_dec = k * jnp.exp(-b)
    k_end = k * jnp.exp(b_end - b)
    decay = [jnp.exp(b[r:r + 1, :]) for r in end_rows]
    lane = lax.broadcasted_iota(jnp.int32, (1, LANES), 1)
    scan = range(n_chunk) if forward else range(n_chunk - 1, -1, -1)
    outs = []
    for head in range(GLA_HEADS):
        pair, half = head // 2, head % 2
        sl = slice(pair * LANES, (pair + 1) * LANES)
        in_head = (lane // GLA_DK) == half
        qd = jnp.where(in_head, q_dec[:, sl], 0.0).astype(BF16)
        kd = k_dec[:, sl].astype(BF16)
        ke = k_end[:, sl].astype(BF16)
        v_h = v[:, head * GLA_DV:(head + 1) * GLA_DV].astype(BF16)
        scores = jnp.where(causal, _dot_nt(qd, kd), 0.0)
        o_intra = _dot(scores.astype(BF16), v_h)
        st = st_ref[head]
        parts = [None] * n_chunk
        for ci in scan:
            rows = slice(ci * GLA_CHUNK, (ci + 1) * GLA_CHUNK)
            parts[ci] = o_intra[rows] + _dot_nt(qd[rows], st.astype(BF16))
            st = st * decay[ci][:, sl] + _dot_tn(v_h[rows], ke[rows])
        st_ref[head] = st
        outs.append(jnp.concatenate(parts, axis=0))
    return jnp.concatenate(outs, axis=-1)


def _mix1_kernel(x_ref, xp_ref, xn_ref, mod_ref, n1g_ref, win_ref, wdec_ref, bdec_ref, convw_ref, cng_ref,
                 mgrp_ref, qk_ref, v_ref, lab_ref, of_ref, gy_ref, st_s, *, tt, nt):
    i = pl.program_id(1)

    @pl.when(i == 0)
    def _():
        st_s[...] = jnp.zeros_like(st_s)

    n1g = n1g_ref[...]
    cw = convw_ref[...]
    row = lax.broadcasted_iota(jnp.int32, (tt, 1), 0)
    mod = mod_ref[0]
    sh1, sc1 = mod[0:1], mod[1:2]

    def norm_mod(x):
        return ((x * _rms(x)) * n1g) * (1.0 + sc1) + sh1

    h = norm_mod(x_ref[0]).astype(BF16)
    qk_ref[0] = _dot(h, win_ref[:, C_QK:C_V])
    v_ref[0] = _dot(h, win_ref[:, C_V:C_G])
    gy_ref[0, :, 0:GLA_WIDTH] = _dot(h, win_ref[:, C_G:C_CB])

    hh = norm_mod(jnp.concatenate([xp_ref[0], xn_ref[0]], axis=0)).astype(BF16)
    h_ext = jnp.concatenate([h, hh], axis=0)
    z_ext = _dot(h_ext, win_ref[:, C_CC:C_CH]) * _dot(h_ext, win_ref[:, C_CH:C_AL])
    z = z_ext[0:tt]
    z_prev = jnp.where(i > 0, z_ext[tt + SUBLANES - 1:tt + SUBLANES, :], 0.0)
    z_next = jnp.where(i < nt - 1, z_ext[tt + SUBLANES:tt + SUBLANES + 1, :], 0.0)
    z_m1 = jnp.where(row == 0, z_prev, pltpu.roll(z, 1, axis=0))
    z_p1 = jnp.where(row == tt - 1, z_next, pltpu.roll(z, tt - 1, axis=0))
    conv = cw[0:1] * z_m1 + cw[1:2] * z + cw[2:3] * z_p1
    yc = _dot(h, win_ref[:, C_CB:C_CC]) * conv
    sq_hi, sq_lo = _split2(yc * yc)
    ss = _dot(sq_hi, mgrp_ref[...]) + _dot(sq_lo, mgrp_ref[...])
    gy_ref[0, :, GLA_WIDTH:] = (yc * lax.rsqrt(ss * (1.0 / CONV_GDIM) + EPS)) * cng_ref[...]

    zd = _dot_f32(_dot(h, win_ref[:, C_AL:C_END]), wdec_ref[...]) + bdec_ref[...]
    la = _log_sigmoid(zd) * (1.0 / GLA_GATE_NORM)
    lab_ref[0] = la[:, GLA_KW:]
    of_ref[0] = _gla_tile(qk_ref[0, :, 0:GLA_KW], qk_ref[0, :, GLA_KW:], v_ref[0], la[:, 0:GLA_KW],
                           st_s, True)


def _mix1(x, mod, n1g, win, wdec, bdec, convw, cng, mgrp, tt):
    bsz, seq, _ = x.shape
    nt = seq // tt
    hb = tt // SUBLANES
    last_hb = seq // SUBLANES - 1
    const = lambda shape: pl.BlockSpec(shape, lambda b, i: tuple(0 for _ in shape))
    tile = lambda w: pl.BlockSpec((1, tt, w), lambda b, i: (b, i, 0))
    out_w = (2 * GLA_KW, GLA_WIDTH, GLA_KW, GLA_WIDTH, GLA_WIDTH + CONV_WIDTH)
    return pl.pallas_call(
        functools.partial(_mix1_kernel, tt=tt, nt=nt),
        out_shape=tuple(jax.ShapeDtypeStruct((bsz, seq, w), F32) for w in out_w),
        grid=(bsz, nt),
        in_specs=[
            tile(D_MODEL),
            pl.BlockSpec((1, SUBLANES, D_MODEL), lambda b, i: (b, jnp.maximum(i * hb - 1, 0), 0)),
            pl.BlockSpec((1, SUBLANES, D_MODEL), lambda b, i: (b, jnp.minimum((i + 1) * hb, last_hb), 0)),
            pl.BlockSpec((1, SUBLANES, D_MODEL), lambda b, i: (b, 0, 0)),
            const((1, D_MODEL)),
            const((D_MODEL, C_END)),
            const((LANES, 2 * GLA_KW)),
            const((1, 2 * GLA_KW)),
            const((3, CONV_WIDTH)),
            const((1, CONV_WIDTH)),
            const((CONV_WIDTH, CONV_WIDTH)),
        ],
        out_specs=tuple(tile(w) for w in out_w),
        scratch_shapes=[pltpu.VMEM((GLA_HEADS, GLA_DV, LANES), F32)],
        compiler_params=pltpu.CompilerParams(dimension_semantics=("arbitrary", "arbitrary"),
                                             vmem_limit_bytes=VMEM_LIMIT),
        name="mix1",
    )(x, x, x, mod, n1g, win, wdec, bdec, convw, cng, mgrp)


def _mix2_kernel(x_ref, mod_ref, qk_ref, v_ref, lab_ref, of_ref, gy_ref, gng_ref, wout_ref, n2g_ref,
                 x1_ref, h2_ref, h8_ref, hs_ref, st_s):
    i = pl.program_id(1)

    @pl.when(i == 0)
    def _():
        st_s[...] = jnp.zeros_like(st_s)

    gng = gng_ref[...]
    mod = mod_ref[0]
    gt1, sh2, sc2 = mod[2:3], mod[3:4], mod[4:5]
    o = of_ref[0] + _gla_tile(qk_ref[0, :, 0:GLA_KW], qk_ref[0, :, GLA_KW:], v_ref[0], lab_ref[0],
                               st_s, False)
    y = None
    for head in range(GLA_HEADS):
        sl = slice(head * GLA_DV, (head + 1) * GLA_DV)
        oh = o[:, sl]
        g = gy_ref[0, :, sl]
        yh = ((oh * _rms(oh)) * gng) * (g * _sigmoid(g))
        part = _dot(yh.astype(BF16), wout_ref[sl, :])
        y = part if y is None else y + part
    y = y + _dot(gy_ref[0, :, GLA_WIDTH:].astype(BF16), wout_ref[GLA_WIDTH:, :])
    x1 = x_ref[0] + gt1 * y
    x1_ref[0] = x1
    h2 = ((x1 * _rms(x1)) * n2g_ref[...]) * (1.0 + sc2) + sh2
    h2_ref[0] = h2.astype(BF16)
    amax = jnp.maximum(jnp.max(jnp.abs(h2), axis=-1, keepdims=True), F8_TINY)
    h8_ref[0] = (h2 * (F8_TOP / amax)).astype(F8)
    hs_ref[0] = jnp.broadcast_to(amax * (1.0 / F8_TOP), (h2.shape[0], LANES))


def _mix2(x, mod, qk, v, lab, of, gy, gng, wout, n2g, tt):
    bsz, seq, _ = x.shape
    nt = seq // tt
    const = lambda shape: pl.BlockSpec(shape, lambda b, i: tuple(0 for _ in shape))
    tile = lambda w: pl.BlockSpec((1, tt, w), lambda b, i: (b, nt - 1 - i, 0))
    return pl.pallas_call(
        _mix2_kernel,
        out_shape=(jax.ShapeDtypeStruct((bsz, seq, D_MODEL), F32), jax.ShapeDtypeStruct((bsz, seq, D_MODEL), BF16),
                   jax.ShapeDtypeStruct((bsz, seq, D_MODEL), F8), jax.ShapeDtypeStruct((bsz, seq, LANES), F32)),
        grid=(bsz, nt),
        in_specs=[
            tile(D_MODEL),
            pl.BlockSpec((1, SUBLANES, D_MODEL), lambda b, i: (b, 0, 0)),
            tile(2 * GLA_KW), tile(GLA_WIDTH), tile(GLA_KW), tile(GLA_WIDTH), tile(GLA_WIDTH + CONV_WIDTH),
            const((1, GLA_DV)),
            const((D_MODEL, D_MODEL)),
            const((1, D_MODEL)),
        ],
        out_specs=(tile(D_MODEL), tile(D_MODEL), tile(D_MODEL), tile(LANES)),
        scratch_shapes=[pltpu.VMEM((GLA_HEADS, GLA_DV, LANES), F32)],
        compiler_params=pltpu.CompilerParams(dimension_semantics=("arbitrary", "arbitrary"),
                                             vmem_limit_bytes=VMEM_LIMIT),
        name="mix2",
    )(x, mod, qk, v, lab, of, gy, gng, wout, n2g)


NEG_INF = float("-inf")
KEY_ROWS = 2 * PEER_NKEYS * PEER_HEADS
VREG = (SUBLANES, LANES)


def _keyproj_kernel(wq_ref, sk_ref, o_ref):
    w_hi, w_lo = _split2(wq_ref[...])
    k_hi, k_lo = _split2(sk_ref[0])
    o_ref[0] = _dot_nt(k_hi, w_hi) + _dot_nt(k_hi, w_lo) + _dot_nt(k_lo, w_hi)


def _keyproj(wq, subkeys):
    n_hp = 2 * PEER_HEADS
    return pl.pallas_call(
        _keyproj_kernel,
        out_shape=jax.ShapeDtypeStruct((n_hp, PEER_NKEYS, D_MODEL), F32),
        grid=(n_hp,),
        in_specs=[pl.BlockSpec((D_MODEL, PEER_DQH), lambda j: (0, j)),
                  pl.BlockSpec((1, PEER_NKEYS, PEER_DQH), lambda j: (j, 0, 0))],
        out_specs=pl.BlockSpec((1, PEER_NKEYS, D_MODEL), lambda j: (j, 0, 0)),
        compiler_params=pltpu.CompilerParams(dimension_semantics=("arbitrary",), vmem_limit_bytes=VMEM_LIMIT),
        name="keyproj",
    )(wq, subkeys)


def _sort16_network():
    pairs = []

    def merge(lo, hi, r):
        step = r * 2
        if step < hi - lo:
            merge(lo, hi, step)
            merge(lo + r, hi, step)
            pairs.extend((i, i + r) for i in range(lo + r, hi - r, step))
        else:
            pairs.append((lo, lo + r))

    def sort(lo, hi):
        if hi - lo >= 1:
            mid = lo + (hi - lo) // 2
            sort(lo, mid)
            sort(mid + 1, hi)
            merge(lo, hi, 1)

    sort(0, PEER_TOPK - 1)
    return tuple(pairs)


SORT16 = _sort16_network()
CAND_LISTS = ([[(0, k2) for k2 in range(PEER_TOPK)]]
              + [[(k1, k2) for k2 in range(PEER_TOPK // (k1 + 1))] for k1 in range(1, 8)]
              + [[(k1, 0) for k1 in range(8, PEER_TOPK)]])
CANDS = [c for lst in CAND_LISTS for c in lst]


def _tree(op, xs):
    xs = list(xs)
    while len(xs) > 1:
        xs = [op(xs[i], xs[i + 1]) for i in range(0, len(xs) - 1, 2)] + ([xs[-1]] if len(xs) % 2 else [])
    return xs[0]


def _ce(v, x, i, j):
    c = v[j] > v[i]
    v[i], v[j] = jnp.maximum(v[i], v[j]), jnp.minimum(v[i], v[j])
    x[i], x[j] = jnp.where(c, x[j], x[i]), jnp.where(c, x[i], x[j])


def _merge_top16(va, xa, vb, xb, sort_result):
    v, x = list(va), list(xa)
    for j in range(PEER_TOPK - len(vb), PEER_TOPK):
        b = PEER_TOPK - 1 - j
        c = vb[b] > va[j]
        v[j] = jnp.maximum(va[j], vb[b])
        x[j] = jnp.where(c, xb[b], xa[j])
    if sort_result:
        for d in (8, 4, 2, 1):
            for i in range(PEER_TOPK):
                if not i & d:
                    _ce(v, x, i, i + d)
    return v, x


def _any(flag):
    return jnp.max(flag) > 0.0


def _top16_keys(key, work_v, work_x, sv_ref, si_ref):
    for g in range(PEER_NKEYS // PEER_TOPK):
        v = [key(PEER_TOPK * g + i) for i in range(PEER_TOPK)]
        x = [float(PEER_TOPK * g + i) for i in range(PEER_TOPK)]
        for i, j in SORT16:
            _ce(v, x, i, j)
        for i in range(PEER_TOPK):
            work_v[PEER_TOPK * g + i] = v[i]
            work_x[PEER_TOPK * g + i] = x[i]
    for span in (1, 2, 4):
        for g in range(0, PEER_NKEYS // PEER_TOPK, 2 * span):
            a, b = PEER_TOPK * g, PEER_TOPK * (g + span)
            v, x = _merge_top16([work_v[a + i] for i in range(PEER_TOPK)], [work_x[a + i] for i in range(PEER_TOPK)],
                                [work_v[b + i] for i in range(PEER_TOPK)], [work_x[b + i] for i in range(PEER_TOPK)],
                                True)
            for i in range(PEER_TOPK):
                if span == 4:
                    sv_ref[i], si_ref[i] = v[i], x[i]
                else:
                    work_v[a + i], work_x[a + i] = v[i], x[i]
    repeat = _tree(jnp.maximum, [jnp.where(v[r] == v[r + 1], 1.0, 0.0) for r in range(PEER_TOPK - 1)])
    n_ge = _tree(jnp.add, [jnp.where(key(k) >= v[PEER_TOPK - 1], 1.0, 0.0) for k in range(PEER_NKEYS)])

    @pl.when(_any(jnp.maximum(repeat, jnp.where(n_ge > float(PEER_TOPK), 1.0, 0.0))))
    def _():
        for k in range(PEER_NKEYS):
            work_v[k] = key(k)

        def extract(r, carry):
            vals = [work_v[k] for k in range(PEER_NKEYS)]
            m = _tree(jnp.maximum, vals)
            idx = _tree(jnp.minimum, [jnp.where(vals[k] == m, float(k), float(PEER_NKEYS)) for k in range(PEER_NKEYS)])
            for k in range(PEER_NKEYS):
                work_v[k] = jnp.where(idx == float(k), NEG_INF, vals[k])
            sv_ref[r] = m
            si_ref[r] = idx
            return carry

        lax.fori_loop(0, PEER_TOPK, extract, 0)


def _top16_sums(sv_s, si_s, work_v, work_x, rv_s, re_s):
    sv0 = [sv_s[0, r] for r in range(PEER_TOPK)]
    sv1 = [sv_s[1, r] for r in range(PEER_TOPK)]
    e_hi = [si_s[0, r] * float(PEER_NKEYS) for r in range(PEER_TOPK)]
    si1 = [si_s[1, r] for r in range(PEER_TOPK)]
    val = {c: sv0[c[0]] + sv1[c[1]] for c in CANDS}
    eid = {c: e_hi[c[0]] + si1[c[1]] for c in CANDS}
    v, x = [val[c] for c in CAND_LISTS[0]], [eid[c] for c in CAND_LISTS[0]]
    for n, lst in enumerate(CAND_LISTS[1:]):
        v, x = _merge_top16(v, x, [val[c] for c in lst], [eid[c] for c in lst], n < len(CAND_LISTS) - 2)
    for r in range(PEER_TOPK):
        rv_s[r], re_s[r] = v[r], x[r]
    low = _tree(jnp.minimum, v)
    n_ge = _tree(jnp.add, [jnp.where(val[c] >= low, 1.0, 0.0) for c in CANDS])

    @pl.when(_any(jnp.where(n_ge > float(PEER_TOPK), 1.0, 0.0)))
    def _():
        for n, c in enumerate(CANDS):
            work_v[n], work_x[n] = val[c], eid[c]

        def extract(r, carry):
            vals = [work_v[n] for n in range(len(CANDS))]
            m = _tree(jnp.maximum, vals)
            code = [float(c[0] * PEER_TOPK + c[1]) for c in CANDS]
            sel = _tree(jnp.minimum, [jnp.where(vals[n] == m, code[n], 1e9) for n in range(len(CANDS))])
            hit = [sel == code[n] for n in range(len(CANDS))]
            rv_s[r] = m
            re_s[r] = _tree(jnp.add, [jnp.where(hit[n], work_x[n], 0.0) for n in range(len(CANDS))])
            for n in range(len(CANDS)):
                work_v[n] = jnp.where(hit[n], NEG_INF, vals[n])
            return carry

        lax.fori_loop(0, PEER_TOPK, extract, 0)


def _route_kernel(h2_ref, wpt_ref, a_ref, b_ref, g_ref, sc_s, work_v, work_x, sv_s, si_s, rv_s, re_s,
                  pa_s, pb_s, pg_s, *, tt):
    sc_s[...] = _dot_nt(wpt_ref[...], h2_ref[0])

    def lane_tile(c, carry):
        lanes = pl.ds(pl.multiple_of(c * LANES, LANES), LANES)
        for p in range(2):
            def key(k, p=p):
                r0 = (p * PEER_NKEYS + k) * SUBLANES
                return sc_s[r0:r0 + SUBLANES, lanes]
            _top16_keys(key, work_v, work_x, sv_s.at[p], si_s.at[p])
        _top16_sums(sv_s, si_s, work_v, work_x, rv_s, re_s)
        v = [rv_s[r] for r in range(PEER_TOPK)]
        m = _tree(jnp.maximum, v)
        ex = [jnp.exp(vr - m) for vr in v]
        inv = 1.0 / _tree(jnp.add, ex)
        for r in range(PEER_TOPK):
            e = re_s[r]
            i1 = jnp.floor(e * (1.0 / PEER_NKEYS))
            rows = slice(r * SUBLANES, (r + 1) * SUBLANES)
            pa_s[rows, :] = i1
            pb_s[rows, :] = e - i1 * float(PEER_NKEYS)
            pg_s[rows, :] = ex[r] * inv
        toks = pl.ds(pl.multiple_of(c * LANES, LANES), LANES)
        a_ref[0, toks, :] = jnp.transpose(pa_s[...])
        b_ref[0, toks, :] = jnp.transpose(pb_s[...])
        g_ref[0, toks, :] = jnp.transpose(pg_s[...])
        return carry

    lax.fori_loop(0, tt // LANES, lane_tile, 0)


def _route(h2, wpt, tt):
    bsz, seq, _ = h2.shape
    n_pair = PEER_HEADS * PEER_TOPK
    tile = lambda w: pl.BlockSpec((1, tt, w), lambda b, i: (b, i, 0))
    vregs = lambda *lead: pltpu.VMEM(lead + VREG, F32)
    return pl.pallas_call(
        functools.partial(_route_kernel, tt=tt),
        out_shape=tuple(jax.ShapeDtypeStruct((bsz, seq, n_pair), F32) for _ in range(3)),
        grid=(bsz, seq // tt),
        in_specs=[tile(D_MODEL), pl.BlockSpec((KEY_ROWS, D_MODEL), lambda b, i: (0, 0))],
        out_specs=(tile(n_pair), tile(n_pair), tile(n_pair)),
        scratch_shapes=[pltpu.VMEM((KEY_ROWS, tt), F32), vregs(PEER_NKEYS), vregs(PEER_NKEYS),
                        vregs(2, PEER_TOPK), vregs(2, PEER_TOPK), vregs(PEER_TOPK), vregs(PEER_TOPK),
                        pltpu.VMEM((n_pair, LANES), F32), pltpu.VMEM((n_pair, LANES), F32),
                        pltpu.VMEM((n_pair, LANES), F32)],
        compiler_params=pltpu.CompilerParams(dimension_semantics=("arbitrary", "arbitrary"),
                                             vmem_limit_bytes=VMEM_LIMIT),
        name="route",
    )(h2, wpt)


def _peer_kernel(h8_ref, hs_ref, x1_ref, mod_ref, a_ref, b_ref, g_ref, ut_ref, us_ref, v_ref, nfg_ref, y_ref,
                 gate_s, acc_s, *, tt, eb, ne):
    j = pl.program_id(2)
    n_grp = eb // PEER_NKEYS
    steps_per_half = ne // 2

    @pl.when(j == 0)
    def _():
        acc_s[...] = jnp.zeros_like(acc_s)
        sub = lax.broadcasted_iota(jnp.int32, (PEER_NKEYS, PEER_NKEYS), 0).astype(F32).astype(BF16)
        zero = jnp.zeros((PEER_NKEYS, PEER_NKEYS), BF16)
        half = jnp.full((PEER_NKEYS, PEER_NKEYS), 0.5, BF16)

        def onehots(t):
            bc = lambda ref: jnp.broadcast_to(ref[0, pl.ds(t, 1), :], (PEER_NKEYS, PEER_NKEYS)).astype(BF16)
            at = jnp.where(sub == bc(a_ref), half, zero)
            cbt = jnp.where(sub == bc(b_ref), bc(g_ref), zero)
            return at, cbt

        def build(p, carry):
            for u in range(GATE_UNROLL):
                t0 = (p * GATE_UNROLL + u) * 2
                at0, cbt0 = onehots(t0)
                at1, cbt1 = onehots(t0 + 1)
                lhs = jnp.concatenate([at0, at1], axis=1)
                rhs = jnp.concatenate([jnp.concatenate([cbt0.T, zero], axis=1),
                                       jnp.concatenate([zero, cbt1.T], axis=1)], axis=0)
                tiles = _dot(lhs, rhs)
                for k in range(2):
                    lo = tiles[0:GATE_HALF, k * PEER_NKEYS:(k + 1) * PEER_NKEYS]
                    hi = tiles[GATE_HALF:, k * PEER_NKEYS:(k + 1) * PEER_NKEYS]
                    r0 = pl.multiple_of((t0 + k) * GATE_PITCH, SUBLANES)
                    gate_s[pl.ds(r0, GATE_HALF), :] = (_bf16_bits(lo) >> 16) | _bf16_bits(hi)
            return carry

        lax.fori_loop(0, tt // (2 * GATE_UNROLL), build, 0)

    def expert_block(half):
        s = _dot(h8_ref[0], ut_ref[...])
        row_scale = hs_ref[0]
        parts = []
        for gi in range(n_grp):
            lanes = slice(gi * PEER_NKEYS, (gi + 1) * PEER_NKEYS)
            r = (j - half * steps_per_half) * n_grp + gi
            packed = gate_s[pl.ds(r, tt, stride=GATE_PITCH), :]
            word = (packed & jnp.uint32(0xFFFF0000)) if half else (packed << 16)
            half_gate = lax.bitcast_convert_type(word, F32)
            act = s[:, lanes] * (row_scale * us_ref[:, lanes])
            parts.append(_half_gated_gelu(act, half_gate).astype(BF16))
        acc_s[...] += _dot(jnp.concatenate(parts, axis=1), v_ref[...])

    @pl.when(j < steps_per_half)
    def _():
        expert_block(0)

    @pl.when(j >= steps_per_half)
    def _():
        expert_block(1)

    @pl.when(j == ne - 1)
    def _():
        gt2 = mod_ref[0][5:6]
        x2 = x1_ref[0] + gt2 * acc_s[...]
        y_ref[0] = (x2 * _rms(x2)) * nfg_ref[...]


def _peer(h8, hs, x1, mod, a_idx, b_idx, gates, ut8, us, v, nfg, tt, eb):
    bsz, seq, _ = h8.shape
    n_exp = v.shape[0]
    ne = n_exp // eb
    assert ne % 2 == 0 and (GATE_HALF * PEER_NKEYS) % eb == 0 and tt % (2 * GATE_UNROLL) == 0
    n_pair = PEER_HEADS * PEER_TOPK
    tile = lambda w: pl.BlockSpec((1, tt, w), lambda b, i, j: (b, i, 0))
    return pl.pallas_call(
        functools.partial(_peer_kernel, tt=tt, eb=eb, ne=ne),
        out_shape=jax.ShapeDtypeStruct((bsz, seq, D_MODEL), F32),
        grid=(bsz, seq // tt, ne),
        in_specs=[
            tile(D_MODEL), tile(LANES), tile(D_MODEL),
            pl.BlockSpec((1, SUBLANES, D_MODEL), lambda b, i, j: (b, 0, 0)),
            tile(n_pair), tile(n_pair), tile(n_pair),
            pl.BlockSpec((D_MODEL, eb), lambda b, i, j: (0, j)),
            pl.BlockSpec((1, eb), lambda b, i, j: (0, j)),
            pl.BlockSpec((eb, D_MODEL), lambda b, i, j: (j, 0)),
            pl.BlockSpec((1, D_MODEL), lambda b, i, j: (0, 0)),
        ],
        out_specs=tile(D_MODEL),
        scratch_shapes=[pltpu.VMEM((tt * GATE_PITCH, PEER_NKEYS), jnp.uint32), pltpu.VMEM((tt, D_MODEL), F32)],
        compiler_params=pltpu.CompilerParams(dimension_semantics=("arbitrary", "arbitrary", "arbitrary"),
                                             vmem_limit_bytes=VMEM_LIMIT),
        name="peer",
    )(h8, hs, x1, mod, a_idx, b_idx, gates, ut8, us, v, nfg)


def _uquant_kernel(ut_ref, u8_ref, us_ref):
    ut = ut_ref[...]
    amax = jnp.maximum(jnp.max(jnp.abs(ut), axis=0, keepdims=True), F8_TINY)
    u8_ref[...] = (ut * (F8_TOP / amax)).astype(F8)
    us_ref[...] = amax * (1.0 / F8_TOP)


def _uquant(ut, eb):
    d, n_exp = ut.shape
    return pl.pallas_call(
        _uquant_kernel,
        out_shape=(jax.ShapeDtypeStruct((d, n_exp), F8), jax.ShapeDtypeStruct((1, n_exp), F32)),
        grid=(n_exp // eb,),
        in_specs=[pl.BlockSpec((d, eb), lambda j: (0, j))],
        out_specs=(pl.BlockSpec((d, eb), lambda j: (0, j)), pl.BlockSpec((1, eb), lambda j: (0, j))),
        compiler_params=pltpu.CompilerParams(dimension_semantics=("arbitrary",), vmem_limit_bytes=VMEM_LIMIT),
        name="uquant",
    )(ut)


def _prep_weights(norm1_g, norm2_g, w_in, w_dec_f, b_dec_f, w_dec_b, b_dec_b, gla_norm_g, conv_w, conv_norm_g,
                  w_out, peer_wq, peer_subkeys, peer_u, peer_v, normf_g):
    w = w_in[0]
    q, k, v, g, alf, alb, cb, cc, ch = jnp.split(w, (256, 512, 1024, 1536, 1552, 1568, 2080, 2592), axis=-1)
    pad = jnp.zeros((D_MODEL, C_END - C_AL - 2 * GLA_RANK), F32)
    win = jnp.concatenate([q, k, v, g, cb, cc, ch, alf, alb, pad], axis=-1).astype(BF16)
    wdec = jnp.zeros((LANES, 2 * GLA_KW), F32)
    wdec = wdec.at[0:GLA_RANK, 0:GLA_KW].set(w_dec_f[0]).at[GLA_RANK:2 * GLA_RANK, GLA_KW:].set(w_dec_b[0])
    bdec = jnp.concatenate([b_dec_f[0], b_dec_b[0]])[None, :]
    grp = jnp.arange(CONV_WIDTH) // CONV_GDIM
    mgrp = (grp[:, None] == grp[None, :]).astype(BF16)
    wp = _keyproj(peer_wq[0], peer_subkeys[0].reshape(2 * PEER_HEADS, PEER_NKEYS, PEER_DQH))
    wpt = wp.reshape(PEER_HEADS, 2, PEER_NKEYS, D_MODEL).transpose(1, 2, 0, 3).reshape(KEY_ROWS, D_MODEL)
    ut8, us = _uquant(jnp.transpose(peer_u[0]), EXPERT_BLOCK)
    return dict(
        n1g=norm1_g[0][None, :], n2g=norm2_g[0][None, :], win=win, wdec=wdec, bdec=bdec,
        convw=conv_w[0], cng=conv_norm_g[0][None, :], mgrp=mgrp, gng=gla_norm_g[0][None, :],
        wout=w_out[0].astype(BF16), wpt=wpt.astype(BF16),
        ut8=ut8, us=us, v=peer_v[0].astype(BF16), nfg=normf_g[None, :],
    )


def _trunk(x, mod, w, tt, rtt, ptt, eb):
    qk, v, lab, of, gy = _mix1(x, mod, w["n1g"], w["win"], w["wdec"], w["bdec"], w["convw"], w["cng"], w["mgrp"], tt)
    x1, h2, h8, hs = _mix2(x, mod, qk, v, lab, of, gy, w["gng"], w["wout"], w["n2g"], tt)
    a_idx, b_idx, gates = _route(h2, w["wpt"], rtt)
    return _peer(h8, hs, x1, mod, a_idx, b_idx, gates, w["ut8"], w["us"], w["v"], w["nfg"], ptt, eb)


def kernel(x_prompt, x_sample, c_prompt, c_sample, norm1_g, norm2_g, w_ada, b_ada, w_in, w_dec_f, b_dec_f,
           w_dec_b, b_dec_b, gla_norm_g, conv_w, conv_norm_g, w_out, peer_wq, peer_subkeys, peer_u, peer_v,
           normf_g):
    w = _prep_weights(norm1_g, norm2_g, w_in, w_dec_f, b_dec_f, w_dec_b, b_dec_b, gla_norm_g, conv_w,
                      conv_norm_g, w_out, peer_wq, peer_subkeys, peer_u, peer_v, normf_g)
    nb_p, nb_s = c_prompt.shape[0], c_sample.shape[0]
    c_all = jnp.concatenate([c_prompt, c_sample, jnp.zeros((SUBLANES - nb_p - nb_s, D_MODEL), F32)], axis=0)
    ada = _ada(c_all, w_ada[0], b_ada[0][None, :])
    mod = jnp.pad(ada.reshape(SUBLANES, N_ADA, D_MODEL), ((0, 0), (0, SUBLANES - N_ADA), (0, 0)))
    y_prompt = _trunk(x_prompt, mod[:nb_p], w, TOKEN_TILE, ROUTE_TILE, PEER_TOKEN_TILE, EXPERT_BLOCK)
    y_sample = _trunk(x_sample, mod[nb_p:nb_p + nb_s], w, TOKEN_TILE, ROUTE_TILE, PEER_TOKEN_TILE, EXPERT_BLOCK)
    return (y_prompt, y_sample)
```

```python
import functools

import jax
import jax.numpy as jnp
from jax import lax
from jax.experimental import pallas as pl
from jax.experimental.pallas import tpu as pltpu

F32 = jnp.float32
BF16 = jnp.bfloat16
F8 = jnp.float8_e4m3fn
F8_TOP = 224.0
F8_TINY = 1e-30

D_MODEL = 1024
GLA_HEADS = 4
GLA_DV = 128
GLA_DK = 64
GLA_KW = GLA_HEADS * GLA_DK
GLA_WIDTH = GLA_HEADS * GLA_DV
GLA_RANK = 16
GLA_GATE_NORM = 16.0
GLA_CHUNK = 64
CONV_WIDTH = 512
CONV_GDIM = 64
PEER_HEADS = 8
PEER_NKEYS = 128
PEER_DQH = 128
PEER_TOPK = 16
N_ADA = 6
EPS = 1e-6

LANES = 128
SUBLANES = 8
VMEM_LIMIT = 56 * 1024 * 1024

C_QK, C_V, C_G, C_CB, C_CC, C_CH, C_AL, C_END = 0, 512, 1024, 1536, 2048, 2560, 3072, 3200

TOKEN_TILE = 256
ROUTE_TILE = 512
PEER_TOKEN_TILE = 512
EXPERT_BLOCK = 2048
GATE_HALF = PEER_NKEYS // 2
GATE_PITCH = GATE_HALF + SUBLANES
GATE_UNROLL = 64


def _dot(a, b):
    return jnp.dot(a, b, preferred_element_type=F32)


def _dot_nt(a, b):
    return lax.dot_general(a, b, (((1,), (1,)), ((), ())), preferred_element_type=F32)


def _dot_tn(a, b):
    return lax.dot_general(a, b, (((0,), (0,)), ((), ())), preferred_element_type=F32)


def _split2(x):
    hi = x.astype(BF16)
    lo = (x - hi.astype(F32)).astype(BF16)
    return hi, lo


def _split3(x):
    hi = x.astype(BF16)
    r = x - hi.astype(F32)
    mid = r.astype(BF16)
    lo = (r - mid.astype(F32)).astype(BF16)
    return hi, mid, lo


def _dot_f32(a, b):
    a_hi, a_lo = _split2(a)
    b_hi, b_lo = _split2(b)
    return _dot(a_hi, b_hi) + _dot(a_hi, b_lo) + _dot(a_lo, b_hi)


def _bf16_bits(x):
    return lax.bitcast_convert_type(x.astype(BF16).astype(F32), jnp.uint32)


def _rms(x):
    return lax.rsqrt(jnp.mean(x * x, axis=-1, keepdims=True) + EPS)


def _sigmoid(x):
    return 1.0 / (1.0 + jnp.exp(-x))


def _log_sigmoid(x):
    return jnp.minimum(x, 0.0) - jnp.log1p(jnp.exp(-jnp.abs(x)))


GELU_C1 = 0.7978845608028654
GELU_C3 = GELU_C1 * 0.044715


def _half_gated_gelu(x, half_gate):
    x = x.astype(BF16)
    gx = half_gate.astype(BF16) * x
    return gx * jnp.tanh(x * (GELU_C1 + GELU_C3 * (x * x))) + gx


def _ada_kernel(c_ref, w_ref, b_ref, o_ref):
    c = c_ref[...]
    o_ref[...] = _dot_f32(c * _sigmoid(c), w_ref[...]) + b_ref[...]


def _ada(c_pad, w_ada, b_ada):
    n_col = N_ADA * D_MODEL
    blk = 1536
    return pl.pallas_call(
        _ada_kernel,
        out_shape=jax.ShapeDtypeStruct((SUBLANES, n_col), F32),
        grid=(n_col // blk,),
        in_specs=[
            pl.BlockSpec((SUBLANES, D_MODEL), lambda j: (0, 0)),
            pl.BlockSpec((D_MODEL, blk), lambda j: (0, j)),
            pl.BlockSpec((1, blk), lambda j: (0, j)),
        ],
        out_specs=pl.BlockSpec((SUBLANES, blk), lambda j: (0, j)),
        compiler_params=pltpu.CompilerParams(dimension_semantics=("arbitrary",), vmem_limit_bytes=VMEM_LIMIT),
        name="ada",
    )(c_pad, w_ada, b_ada)


def _gla_tile(q, k, v, la, st_ref, forward):
    tt = q.shape[0]
    n_chunk = tt // GLA_CHUNK
    row = lax.broadcasted_iota(jnp.int32, (tt, tt), 0)
    col = lax.broadcasted_iota(jnp.int32, (tt, tt), 1)
    ordered = (row >= col) if forward else (row <= col)
    causal = ordered & (row // GLA_CHUNK == col // GLA_CHUNK)
    tri = jnp.where(causal, 1.0, 0.0).astype(BF16)
    la_hi, la_mid, la_lo = _split3(la)
    b = _dot(tri, la_hi) + _dot(tri, la_mid) + _dot(tri, la_lo)
    end_rows = [ci * GLA_CHUNK + (GLA_CHUNK - 1 if forward else 0) for ci in range(n_chunk)]
    b_end = jnp.concatenate([jnp.broadcast_to(b[r:r + 1, :], (GLA_CHUNK, GLA_KW)) for r in end_rows], axis=0)
    q_dec = q * (jnp.exp(b) * (GLA_DK ** -0.5))
    k_dec = k * jnp.exp(-b)
    k_end = k * jnp.exp(b_end - b)
    decay = [jnp.exp(b[r:r + 1, :]) for r in end_rows]
    lane = lax.broadcasted_iota(jnp.int32, (1, LANES), 1)
    scan = range(n_chunk) if forward else range(n_chunk - 1, -1, -1)
    outs = []
    for head in range(GLA_HEADS):
        pair, half = head // 2, head % 2
        sl = slice(pair * LANES, (pair + 1) * LANES)
        in_head = (lane // GLA_DK) == half
        qd = jnp.where(in_head, q_dec[:, sl], 0.0).astype(BF16)
        kd = k_dec[:, sl].astype(BF16)
        ke = k_end[:, sl].astype(BF16)
        v_h = v[:, head * GLA_DV:(head + 1) * GLA_DV].astype(BF16)
        scores = jnp.where(causal, _dot_nt(qd, kd), 0.0)
        o_intra = _dot(scores.astype(BF16), v_h)
        st = st_ref[head]
        parts = [None] * n_chunk
        for ci in scan:
            rows = slice(ci * GLA_CHUNK, (ci + 1) * GLA_CHUNK)
            parts[ci] = o_intra[rows] + _dot_nt(qd[rows], st.astype(BF16))
            st = st * decay[ci][:, sl] + _dot_tn(v_h[rows], ke[rows])
        st_ref[head] = st
        outs.append(jnp.concatenate(parts, axis=0))
    return jnp.concatenate(outs, axis=-1)


def _mix1_kernel(x_ref, xp_ref, xn_ref, mod_ref, n1g_ref, win_ref, wdec_ref, bdec_ref, convw_ref, cng_ref,
                 mgrp_ref, qk_ref, v_ref, lab_ref, of_ref, gy_ref, st_s, *, tt, nt):
    i = pl.program_id(1)

    @pl.when(i == 0)
    def _():
        st_s[...] = jnp.zeros_like(st_s)

    n1g = n1g_ref[...]
    cw = convw_ref[...]
    row = lax.broadcasted_iota(jnp.int32, (tt, 1), 0)
    mod = mod_ref[0]
    sh1, sc1 = mod[0:1], mod[1:2]

    def norm_mod(x):
        return ((x * _rms(x)) * n1g) * (1.0 + sc1) + sh1

    h = norm_mod(x_ref[0]).astype(BF16)
    qk_ref[0] = _dot(h, win_ref[:, C_QK:C_V])
    v_ref[0] = _dot(h, win_ref[:, C_V:C_G])
    gy_ref[0, :, 0:GLA_WIDTH] = _dot(h, win_ref[:, C_G:C_CB])

    hh = norm_mod(jnp.concatenate([xp_ref[0], xn_ref[0]], axis=0)).astype(BF16)
    h_ext = jnp.concatenate([h, hh], axis=0)
    z_ext = _dot(h_ext, win_ref[:, C_CC:C_CH]) * _dot(h_ext, win_ref[:, C_CH:C_AL])
    z = z_ext[0:tt]
    z_prev = jnp.where(i > 0, z_ext[tt + SUBLANES - 1:tt + SUBLANES, :], 0.0)
    z_next = jnp.where(i < nt - 1, z_ext[tt + SUBLANES:tt + SUBLANES + 1, :], 0.0)
    z_m1 = jnp.where(row == 0, z_prev, pltpu.roll(z, 1, axis=0))
    z_p1 = jnp.where(row == tt - 1, z_next, pltpu.roll(z, tt - 1, axis=0))
    conv = cw[0:1] * z_m1 + cw[1:2] * z + cw[2:3] * z_p1
    yc = _dot(h, win_ref[:, C_CB:C_CC]) * conv
    sq_hi, sq_lo = _split2(yc * yc)
    ss = _dot(sq_hi, mgrp_ref[...]) + _dot(sq_lo, mgrp_ref[...])
    gy_ref[0, :, GLA_WIDTH:] = (yc * lax.rsqrt(ss * (1.0 / CONV_GDIM) + EPS)) * cng_ref[...]

    zd = _dot_f32(_dot(h, win_ref[:, C_AL:C_END]), wdec_ref[...]) + bdec_ref[...]
    la = _log_sigmoid(zd) * (1.0 / GLA_GATE_NORM)
    lab_ref[0] = la[:, GLA_KW:]
    of_ref[0] = _gla_tile(qk_ref[0, :, 0:GLA_KW], qk_ref[0, :, GLA_KW:], v_ref[0], la[:, 0:GLA_KW],
                           st_s, True)


def _mix1(x, mod, n1g, win, wdec, bdec, convw, cng, mgrp, tt):
    bsz, seq, _ = x.shape
    nt = seq // tt
    hb = tt // SUBLANES
    last_hb = seq // SUBLANES - 1
    const = lambda shape: pl.BlockSpec(shape, lambda b, i: tuple(0 for _ in shape))
    tile = lambda w: pl.BlockSpec((1, tt, w), lambda b, i: (b, i, 0))
    out_w = (2 * GLA_KW, GLA_WIDTH, GLA_KW, GLA_WIDTH, GLA_WIDTH + CONV_WIDTH)
    return pl.pallas_call(
        functools.partial(_mix1_kernel, tt=tt, nt=nt),
        out_shape=tuple(jax.ShapeDtypeStruct((bsz, seq, w), F32) for w in out_w),
        grid=(bsz, nt),
        in_specs=[
            tile(D_MODEL),
            pl.BlockSpec((1, SUBLANES, D_MODEL), lambda b, i: (b, jnp.maximum(i * hb - 1, 0), 0)),
            pl.BlockSpec((1, SUBLANES, D_MODEL), lambda b, i: (b, jnp.minimum((i + 1) * hb, last_hb), 0)),
            pl.BlockSpec((1, SUBLANES, D_MODEL), lambda b, i: (b, 0, 0)),
            const((1, D_MODEL)),
            const((D_MODEL, C_END)),
            const((LANES, 2 * GLA_KW)),
            const((1, 2 * GLA_KW)),
            const((3, CONV_WIDTH)),
            const((1, CONV_WIDTH)),
            const((CONV_WIDTH, CONV_WIDTH)),
        ],
        out_specs=tuple(tile(w) for w in out_w),
        scratch_shapes=[pltpu.VMEM((GLA_HEADS, GLA_DV, LANES), F32)],
        compiler_params=pltpu.CompilerParams(dimension_semantics=("arbitrary", "arbitrary"),
                                             vmem_limit_bytes=VMEM_LIMIT),
        name="mix1",
    )(x, x, x, mod, n1g, win, wdec, bdec, convw, cng, mgrp)


def _mix2_kernel(x_ref, mod_ref, qk_ref, v_ref, lab_ref, of_ref, gy_ref, gng_ref, wout_ref, n2g_ref,
                 x1_ref, h2_ref, h8_ref, hs_ref, st_s):
    i = pl.program_id(1)

    @pl.when(i == 0)
    def _():
        st_s[...] = jnp.zeros_like(st_s)

    gng = gng_ref[...]
    mod = mod_ref[0]
    gt1, sh2, sc2 = mod[2:3], mod[3:4], mod[4:5]
    o = of_ref[0] + _gla_tile(qk_ref[0, :, 0:GLA_KW], qk_ref[0, :, GLA_KW:], v_ref[0], lab_ref[0],
                               st_s, False)
    y = None
    for head in range(GLA_HEADS):
        sl = slice(head * GLA_DV, (head + 1) * GLA_DV)
        oh = o[:, sl]
        g = gy_ref[0, :, sl]
        yh = ((oh * _rms(oh)) * gng) * (g * _sigmoid(g))
        part = _dot(yh.astype(BF16), wout_ref[sl, :])
        y = part if y is None else y + part
    y = y + _dot(gy_ref[0, :, GLA_WIDTH:].astype(BF16), wout_ref[GLA_WIDTH:, :])
    x1 = x_ref[0] + gt1 * y
    x1_ref[0] = x1
    h2 = ((x1 * _rms(x1)) * n2g_ref[...]) * (1.0 + sc2) + sh2
    h2_ref[0] = h2.astype(BF16)
    amax = jnp.maximum(jnp.max(jnp.abs(h2), axis=-1, keepdims=True), F8_TINY)
    h8_ref[0] = (h2 * (F8_TOP / amax)).astype(F8)
    hs_ref[0] = jnp.broadcast_to(amax * (1.0 / F8_TOP), (h2.shape[0], LANES))


def _mix2(x, mod, qk, v, lab, of, gy, gng, wout, n2g, tt):
    bsz, seq, _ = x.shape
    nt = seq // tt
    const = lambda shape: pl.BlockSpec(shape, lambda b, i: tuple(0 for _ in shape))
    tile = lambda w: pl.BlockSpec((1, tt, w), lambda b, i: (b, nt - 1 - i, 0))
    return pl.pallas_call(
        _mix2_kernel,
        out_shape=(jax.ShapeDtypeStruct((bsz, seq, D_MODEL), F32), jax.ShapeDtypeStruct((bsz, seq, D_MODEL), BF16),
                   jax.ShapeDtypeStruct((bsz, seq, D_MODEL), F8), jax.ShapeDtypeStruct((bsz, seq, LANES), F32)),
        grid=(bsz, nt),
        in_specs=[
            tile(D_MODEL),
            pl.BlockSpec((1, SUBLANES, D_MODEL), lambda b, i: (b, 0, 0)),
            tile(2 * GLA_KW), tile(GLA_WIDTH), tile(GLA_KW), tile(GLA_WIDTH), tile(GLA_WIDTH + CONV_WIDTH),
            const((1, GLA_DV)),
            const((D_MODEL, D_MODEL)),
            const((1, D_MODEL)),
        ],
        out_specs=(tile(D_MODEL), tile(D_MODEL), tile(D_MODEL), tile(LANES)),
        scratch_shapes=[pltpu.VMEM((GLA_HEADS, GLA_DV, LANES), F32)],
        compiler_params=pltpu.CompilerParams(dimension_semantics=("arbitrary", "arbitrary"),
                                             vmem_limit_bytes=VMEM_LIMIT),
        name="mix2",
    )(x, mod, qk, v, lab, of, gy, gng, wout, n2g)


NEG_INF = float("-inf")
KEY_ROWS = 2 * PEER_NKEYS * PEER_HEADS
VREG = (SUBLANES, LANES)


def _keyproj_kernel(wq_ref, sk_ref, o_ref):
    w_hi, w_lo = _split2(wq_ref[...])
    k_hi, k_lo = _split2(sk_ref[0])
    o_ref[0] = _dot_nt(k_hi, w_hi) + _dot_nt(k_hi, w_lo) + _dot_nt(k_lo, w_hi)


def _keyproj(wq, subkeys):
    n_hp = 2 * PEER_HEADS
    return pl.pallas_call(
        _keyproj_kernel,
        out_shape=jax.ShapeDtypeStruct((n_hp, PEER_NKEYS, D_MODEL), F32),
        grid=(n_hp,),
        in_specs=[pl.BlockSpec((D_MODEL, PEER_DQH), lambda j: (0, j)),
                  pl.BlockSpec((1, PEER_NKEYS, PEER_DQH), lambda j: (j, 0, 0))],
        out_specs=pl.BlockSpec((1, PEER_NKEYS, D_MODEL), lambda j: (j, 0, 0)),
        compiler_params=pltpu.CompilerParams(dimension_semantics=("arbitrary",), vmem_limit_bytes=VMEM_LIMIT),
        name="keyproj",
    )(wq, subkeys)


def _sort16_network():
    pairs = []

    def merge(lo, hi, r):
        step = r * 2
        if step < hi - lo:
            merge(lo, hi, step)
            merge(lo + r, hi, step)
            pairs.extend((i, i + r) for i in range(lo + r, hi - r, step))
        else:
            pairs.append((lo, lo + r))

    def sort(lo, hi):
        if hi - lo >= 1:
            mid = lo + (hi - lo) // 2
            sort(lo, mid)
            sort(mid + 1, hi)
            merge(lo, hi, 1)

    sort(0, PEER_TOPK - 1)
    return tuple(pairs)


SORT16 = _sort16_network()
CAND_LISTS = ([[(0, k2) for k2 in range(PEER_TOPK)]]
              + [[(k1, k2) for k2 in range(PEER_TOPK // (k1 + 1))] for k1 in range(1, 8)]
              + [[(k1, 0) for k1 in range(8, PEER_TOPK)]])
CANDS = [c for lst in CAND_LISTS for c in lst]


def _tree(op, xs):
    xs = list(xs)
    while len(xs) > 1:
        xs = [op(xs[i], xs[i + 1]) for i in range(0, len(xs) - 1, 2)] + ([xs[-1]] if len(xs) % 2 else [])
    return xs[0]


def _ce(v, x, i, j):
    c = v[j] > v[i]
    v[i], v[j] = jnp.maximum(v[i], v[j]), jnp.minimum(v[i], v[j])
    x[i], x[j] = jnp.where(c, x[j], x[i]), jnp.where(c, x[i], x[j])


def _merge_top16(va, xa, vb, xb, sort_result):
    v, x = list(va), list(xa)
    for j in range(PEER_TOPK - len(vb), PEER_TOPK):
        b = PEER_TOPK - 1 - j
        c = vb[b] > va[j]
        v[j] = jnp.maximum(va[j], vb[b])
        x[j] = jnp.where(c, xb[b], xa[j])
    if sort_result:
        for d in (8, 4, 2, 1):
            for i in range(PEER_TOPK):
                if not i & d:
                    _ce(v, x, i, i + d)
    return v, x


def _any(flag):
    return jnp.max(flag) > 0.0


def _top16_keys(key, work_v, work_x, sv_ref, si_ref):
    for g in range(PEER_NKEYS // PEER_TOPK):
        v = [key(PEER_TOPK * g + i) for i in range(PEER_TOPK)]
        x = [float(PEER_TOPK * g + i) for i in range(PEER_TOPK)]
        for i, j in SORT16:
            _ce(v, x, i, j)
        for i in range(PEER_TOPK):
            work_v[PEER_TOPK * g + i] = v[i]
            work_x[PEER_TOPK * g + i] = x[i]
    for span in (1, 2, 4):
        for g in range(0, PEER_NKEYS // PEER_TOPK, 2 * span):
            a, b = PEER_TOPK * g, PEER_TOPK * (g + span)
            v, x = _merge_top16([work_v[a + i] for i in range(PEER_TOPK)], [work_x[a + i] for i in range(PEER_TOPK)],
                                [work_v[b + i] for i in range(PEER_TOPK)], [work_x[b + i] for i in range(PEER_TOPK)],
                                True)
            for i in range(PEER_TOPK):
                if span == 4:
                    sv_ref[i], si_ref[i] = v[i], x[i]
                else:
                    work_v[a + i], work_x[a + i] = v[i], x[i]
    repeat = _tree(jnp.maximum, [jnp.where(v[r] == v[r + 1], 1.0, 0.0) for r in range(PEER_TOPK - 1)])
    n_ge = _tree(jnp.add, [jnp.where(key(k) >= v[PEER_TOPK - 1], 1.0, 0.0) for k in range(PEER_NKEYS)])

    @pl.when(_any(jnp.maximum(repeat, jnp.where(n_ge > float(PEER_TOPK), 1.0, 0.0))))
    def _():
        for k in range(PEER_NKEYS):
            work_v[k] = key(k)

        def extract(r, carry):
            vals = [work_v[k] for k in range(PEER_NKEYS)]
            m = _tree(jnp.maximum, vals)
            idx = _tree(jnp.minimum, [jnp.where(vals[k] == m, float(k), float(PEER_NKEYS)) for k in range(PEER_NKEYS)])
            for k in range(PEER_NKEYS):
                work_v[k] = jnp.where(idx == float(k), NEG_INF, vals[k])
            sv_ref[r] = m
            si_ref[r] = idx
            return carry

        lax.fori_loop(0, PEER_TOPK, extract, 0)


def _top16_sums(sv_s, si_s, work_v, work_x, rv_s, re_s):
    sv0 = [sv_s[0, r] for r in range(PEER_TOPK)]
    sv1 = [sv_s[1, r] for r in range(PEER_TOPK)]
    e_hi = [si_s[0, r] * float(PEER_NKEYS) for r in range(PEER_TOPK)]
    si1 = [si_s[1, r] for r in range(PEER_TOPK)]
    val = {c: sv0[c[0]] + sv1[c[1]] for c in CANDS}
    eid = {c: e_hi[c[0]] + si1[c[1]] for c in CANDS}
    v, x = [val[c] for c in CAND_LISTS[0]], [eid[c] for c in CAND_LISTS[0]]
    for n, lst in enumerate(CAND_LISTS[1:]):
        v, x = _merge_top16(v, x, [val[c] for c in lst], [eid[c] for c in lst], n < len(CAND_LISTS) - 2)
    for r in range(PEER_TOPK):
        rv_s[r], re_s[r] = v[r], x[r]
    low = _tree(jnp.minimum, v)
    n_ge = _tree(jnp.add, [jnp.where(val[c] >= low, 1.0, 0.0) for c in CANDS])

    @pl.when(_any(jnp.where(n_ge > float(PEER_TOPK), 1.0, 0.0)))
    def _():
        for n, c in enumerate(CANDS):
            work_v[n], work_x[n] = val[c], eid[c]

        def extract(r, carry):
            vals = [work_v[n] for n in range(len(CANDS))]
            m = _tree(jnp.maximum, vals)
            code = [float(c[0] * PEER_TOPK + c[1]) for c in CANDS]
            sel = _tree(jnp.minimum, [jnp.where(vals[n] == m, code[n], 1e9) for n in range(len(CANDS))])
            hit = [sel == code[n] for n in range(len(CANDS))]
            rv_s[r] = m
            re_s[r] = _tree(jnp.add, [jnp.where(hit[n], work_x[n], 0.0) for n in range(len(CANDS))])
            for n in range(len(CANDS)):
                work_v[n] = jnp.where(hit[n], NEG_INF, vals[n])
            return carry

        lax.fori_loop(0, PEER_TOPK, extract, 0)


def _route_kernel(h2_ref, wpt_ref, a_ref, b_ref, g_ref, sc_s, work_v, work_x, sv_s, si_s, rv_s, re_s,
                  pa_s, pb_s, pg_s, *, tt):
    sc_s[...] = _dot_nt(wpt_ref[...], h2_ref[0])

    def lane_tile(c, carry):
        lanes = pl.ds(pl.multiple_of(c * LANES, LANES), LANES)
        for p in range(2):
            def key(k, p=p):
                r0 = (p * PEER_NKEYS + k) * SUBLANES
                return sc_s[r0:r0 + SUBLANES, lanes]
            _top16_keys(key, work_v, work_x, sv_s.at[p], si_s.at[p])
        _top16_sums(sv_s, si_s, work_v, work_x, rv_s, re_s)
        v = [rv_s[r] for r in range(PEER_TOPK)]
        m = _tree(jnp.maximum, v)
        ex = [jnp.exp(vr - m) for vr in v]
        inv = 1.0 / _tree(jnp.add, ex)
        for r in range(PEER_TOPK):
            e = re_s[r]
            i1 = jnp.floor(e * (1.0 / PEER_NKEYS))
            rows = slice(r * SUBLANES, (r + 1) * SUBLANES)
            pa_s[rows, :] = i1
            pb_s[rows, :] = e - i1 * float(PEER_NKEYS)
            pg_s[rows, :] = ex[r] * inv
        toks = pl.ds(pl.multiple_of(c * LANES, LANES), LANES)
        a_ref[0, toks, :] = jnp.transpose(pa_s[...])
        b_ref[0, toks, :] = jnp.transpose(pb_s[...])
        g_ref[0, toks, :] = jnp.transpose(pg_s[...])
        return carry

    lax.fori_loop(0, tt // LANES, lane_tile, 0)


def _route(h2, wpt, tt):
    bsz, seq, _ = h2.shape
    n_pair = PEER_HEADS * PEER_TOPK
    tile = lambda w: pl.BlockSpec((1, tt, w), lambda b, i: (b, i, 0))
    vregs = lambda *lead: pltpu.VMEM(lead + VREG, F32)
    return pl.pallas_call(
        functools.partial(_route_kernel, tt=tt),
        out_shape=tuple(jax.ShapeDtypeStruct((bsz, seq, n_pair), F32) for _ in range(3)),
        grid=(bsz, seq // tt),
        in_specs=[tile(D_MODEL), pl.BlockSpec((KEY_ROWS, D_MODEL), lambda b, i: (0, 0))],
        out_specs=(tile(n_pair), tile(n_pair), tile(n_pair)),
        scratch_shapes=[pltpu.VMEM((KEY_ROWS, tt), F32), vregs(PEER_NKEYS), vregs(PEER_NKEYS),
                        vregs(2, PEER_TOPK), vregs(2, PEER_TOPK), vregs(PEER_TOPK), vregs(PEER_TOPK),
                        pltpu.VMEM((n_pair, LANES), F32), pltpu.VMEM((n_pair, LANES), F32),
                        pltpu.VMEM((n_pair, LANES), F32)],
        compiler_params=pltpu.CompilerParams(dimension_semantics=("arbitrary", "arbitrary"),
                                             vmem_limit_bytes=VMEM_LIMIT),
        name="route",
    )(h2, wpt)


def _peer_kernel(h8_ref, hs_ref, x1_ref, mod_ref, a_ref, b_ref, g_ref, ut_ref, us_ref, v_ref, nfg_ref, y_ref,
                 gate_s, acc_s, *, tt, eb, ne):
    j = pl.program_id(2)
    n_grp = eb // PEER_NKEYS
    steps_per_half = ne // 2

    @pl.when(j == 0)
    def _():
        acc_s[...] = jnp.zeros_like(acc_s)
        sub = lax.broadcasted_iota(jnp.int32, (PEER_NKEYS, PEER_NKEYS), 0).astype(F32).astype(BF16)
        zero = jnp.zeros((PEER_NKEYS, PEER_NKEYS), BF16)
        half = jnp.full((PEER_NKEYS, PEER_NKEYS), 0.5, BF16)

        def onehots(t):
            bc = lambda ref: jnp.broadcast_to(ref[0, pl.ds(t, 1), :], (PEER_NKEYS, PEER_NKEYS)).astype(BF16)
            at = jnp.where(sub == bc(a_ref), half, zero)
            cbt = jnp.where(sub == bc(b_ref), bc(g_ref), zero)
            return at, cbt

        def build(p, carry):
            for u in range(GATE_UNROLL):
                t0 = (p * GATE_UNROLL + u) * 2
                at0, cbt0 = onehots(t0)
                at1, cbt1 = onehots(t0 + 1)
                lhs = jnp.concatenate([at0, at1], axis=1)
                rhs = jnp.concatenate([jnp.concatenate([cbt0.T, zero], axis=1),
                                       jnp.concatenate([zero, cbt1.T], axis=1)], axis=0)
                tiles = _dot(lhs, rhs)
                for k in range(2):
                    lo = tiles[0:GATE_HALF, k * PEER_NKEYS:(k + 1) * PEER_NKEYS]
                    hi = tiles[GATE_HALF:, k * PEER_NKEYS:(k + 1) * PEER_NKEYS]
                    r0 = pl.multiple_of((t0 + k) * GATE_PITCH, SUBLANES)
                    gate_s[pl.ds(r0, GATE_HALF), :] = (_bf16_bits(lo) >> 16) | _bf16_bits(hi)
            return carry

        lax.fori_loop(0, tt // (2 * GATE_UNROLL), build, 0)

    def expert_block(half):
        s = _dot(h8_ref[0], ut_ref[...])
        row_scale = hs_ref[0]
        parts = []
        for gi in range(n_grp):
            lanes = slice(gi * PEER_NKEYS, (gi + 1) * PEER_NKEYS)
            r = (j - half * steps_per_half) * n_grp + gi
            packed = gate_s[pl.ds(r, tt, stride=GATE_PITCH), :]
            word = (packed & jnp.uint32(0xFFFF0000)) if half else (packed << 16)
            half_gate = lax.bitcast_convert_type(word, F32)
            act = s[:, lanes] * (row_scale * us_ref[:, lanes])
            parts.append(_half_gated_gelu(act, half_gate).astype(BF16))
        acc_s[...] += _dot(jnp.concatenate(parts, axis=1), v_ref[...])

    @pl.when(j < steps_per_half)
    def _():
        expert_block(0)

    @pl.when(j >= steps_per_half)
    def _():
        expert_block(1)

    @pl.when(j == ne - 1)
    def _():
        gt2 = mod_ref[0][5:6]
        x2 = x1_ref[0] + gt2 * acc_s[...]
        y_ref[0] = (x2 * _rms(x2)) * nfg_ref[...]


def _peer(h8, hs, x1, mod, a_idx, b_idx, gates, ut8, us, v, nfg, tt, eb):
    bsz, seq, _ = h8.shape
    n_exp = v.shape[0]
    ne = n_exp // eb
    assert ne % 2 == 0 and (GATE_HALF * PEER_NKEYS) % eb == 0 and tt % (2 * GATE_UNROLL) == 0
    n_pair = PEER_HEADS * PEER_TOPK
    tile = lambda w: pl.BlockSpec((1, tt, w), lambda b, i, j: (b, i, 0))
    return pl.pallas_call(
        functools.partial(_peer_kernel, tt=tt, eb=eb, ne=ne),
        out_shape=jax.ShapeDtypeStruct((bsz, seq, D_MODEL), F32),
        grid=(bsz, seq // tt, ne),
        in_specs=[
            tile(D_MODEL), tile(LANES), tile(D_MODEL),
            pl.BlockSpec((1, SUBLANES, D_MODEL), lambda b, i, j: (b, 0, 0)),
            tile(n_pair), tile(n_pair), tile(n_pair),
            pl.BlockSpec((D_MODEL, eb), lambda b, i, j: (0, j)),
            pl.BlockSpec((1, eb), lambda b, i, j: (0, j)),
            pl.BlockSpec((eb, D_MODEL), lambda b, i, j: (j, 0)),
            pl.BlockSpec((1, D_MODEL), lambda b, i, j: (0, 0)),
        ],
        out_specs=tile(D_MODEL),
        scratch_shapes=[pltpu.VMEM((tt * GATE_PITCH, PEER_NKEYS), jnp.uint32), pltpu.VMEM((tt, D_MODEL), F32)],
        compiler_params=pltpu.CompilerParams(dimension_semantics=("arbitrary", "arbitrary", "arbitrary"),
                                             vmem_limit_bytes=VMEM_LIMIT),
        name="peer",
    )(h8, hs, x1, mod, a_idx, b_idx, gates, ut8, us, v, nfg)


def _uquant_kernel(ut_ref, u8_ref, us_ref):
    ut = ut_ref[...]
    amax = jnp.maximum(jnp.max(jnp.abs(ut), axis=0, keepdims=True), F8_TINY)
    u8_ref[...] = (ut * (F8_TOP / amax)).astype(F8)
    us_ref[...] = amax * (1.0 / F8_TOP)


def _uquant(ut, eb):
    d, n_exp = ut.shape
    return pl.pallas_call(
        _uquant_kernel,
        out_shape=(jax.ShapeDtypeStruct((d, n_exp), F8), jax.ShapeDtypeStruct((1, n_exp), F32)),
        grid=(n_exp // eb,),
        in_specs=[pl.BlockSpec((d, eb), lambda j: (0, j))],
        out_specs=(pl.BlockSpec((d, eb), lambda j: (0, j)), pl.BlockSpec((1, eb), lambda j: (0, j))),
        compiler_params=pltpu.CompilerParams(dimension_semantics=("arbitrary",), vmem_limit_bytes=VMEM_LIMIT),
        name="uquant",
    )(ut)


def _prep_weights(norm1_g, norm2_g, w_in, w_dec_f, b_dec_f, w_dec_b, b_dec_b, gla_norm_g, conv_w, conv_norm_g,
                  w_out, peer_wq, peer_subkeys, peer_u, peer_v, normf_g):
    w = w_in[0]
    q, k, v, g, alf, alb, cb, cc, ch = jnp.split(w, (256, 512, 1024, 1536, 1552, 1568, 2080, 2592), axis=-1)
    pad = jnp.zeros((D_MODEL, C_END - C_AL - 2 * GLA_RANK), F32)
    win = jnp.concatenate([q, k, v, g, cb, cc, ch, alf, alb, pad], axis=-1).astype(BF16)
    wdec = jnp.zeros((LANES, 2 * GLA_KW), F32)
    wdec = wdec.at[0:GLA_RANK, 0:GLA_KW].set(w_dec_f[0]).at[GLA_RANK:2 * GLA_RANK, GLA_KW:].set(w_dec_b[0])
    bdec = jnp.concatenate([b_dec_f[0], b_dec_b[0]])[None, :]
    grp = jnp.arange(CONV_WIDTH) // CONV_GDIM
    mgrp = (grp[:, None] == grp[None, :]).astype(BF16)
    wp = _keyproj(peer_wq[0], peer_subkeys[0].reshape(2 * PEER_HEADS, PEER_NKEYS, PEER_DQH))
    wpt = wp.reshape(PEER_HEADS, 2, PEER_NKEYS, D_MODEL).transpose(1, 2, 0, 3).reshape(KEY_ROWS, D_MODEL)
    ut8, us = _uquant(jnp.transpose(peer_u[0]), EXPERT_BLOCK)
    return dict(
        n1g=norm1_g[0][None, :], n2g=norm2_g[0][None, :], win=win, wdec=wdec, bdec=bdec,
        convw=conv_w[0], cng=conv_norm_g[0][None, :], mgrp=mgrp, gng=gla_norm_g[0][None, :],
        wout=w_out[0].astype(BF16), wpt=wpt.astype(BF16),
        ut8=ut8, us=us, v=peer_v[0].astype(BF16), nfg=normf_g[None, :],
    )


def _trunk(x, mod, w, tt, rtt, ptt, eb):
    qk, v, lab, of, gy = _mix1(x, mod, w["n1g"], w["win"], w["wdec"], w["bdec"], w["convw"], w["cng"], w["mgrp"], tt)
    x1, h2, h8, hs = _mix2(x, mod, qk, v, lab, of, gy, w["gng"], w["wout"], w["n2g"], tt)
    a_idx, b_idx, gates = _route(h2, w["wpt"], rtt)
    return _peer(h8, hs, x1, mod, a_idx, b_idx, gates, w["ut8"], w["us"], w["v"], w["nfg"], ptt, eb)


def kernel(x_prompt, x_sample, c_prompt, c_sample, norm1_g, norm2_g, w_ada, b_ada, w_in, w_dec_f, b_dec_f,
           w_dec_b, b_dec_b, gla_norm_g, conv_w, conv_norm_g, w_out, peer_wq, peer_subkeys, peer_u, peer_v,
           normf_g):
    w = _prep_weights(norm1_g, norm2_g, w_in, w_dec_f, b_dec_f, w_dec_b, b_dec_b, gla_norm_g, conv_w,
                      conv_norm_g, w_out, peer_wq, peer_subkeys, peer_u, peer_v, normf_g)
    nb_p, nb_s = c_prompt.shape[0], c_sample.shape[0]
    c_all = jnp.concatenate([c_prompt, c_sample, jnp.zeros((SUBLANES - nb_p - nb_s, D_MODEL), F32)], axis=0)
    ada = _ada(c_all, w_ada[0], b_ada[0][None, :])
    mod = jnp.pad(ada.reshape(SUBLANES, N_ADA, D_MODEL), ((0, 0), (0, SUBLANES - N_ADA), (0, 0)))
    y_prompt = _trunk(x_prompt, mod[:nb_p], w, TOKEN_TILE, ROUTE_TILE, PEER_TOKEN_TILE, EXPERT_BLOCK)
    y_sample = _trunk(x_sample, mod[nb_p:nb_p + nb_s], w, TOKEN_TILE, ROUTE_TILE, PEER_TOKEN_TILE, EXPERT_BLOCK)
    return (y_prompt, y_sample)
```

```python
import functools

import jax
import jax.numpy as jnp
from jax import lax
from jax.experimental import pallas as pl
from jax.experimental.pallas import tpu as pltpu

F32 = jnp.float32
BF16 = jnp.bfloat16
F8 = jnp.float8_e4m3fn
F8_TOP = 224.0
F8_TINY = 1e-30

D_MODEL = 1024
GLA_HEADS = 4
GLA_DV = 128
GLA_DK = 64
GLA_KW = GLA_HEADS * GLA_DK
GLA_WIDTH = GLA_HEADS * GLA_DV
GLA_RANK = 16
GLA_GATE_NORM = 16.0
GLA_CHUNK = 64
CONV_WIDTH = 512
CONV_GDIM = 64
PEER_HEADS = 8
PEER_NKEYS = 128
PEER_DQH = 128
PEER_TOPK = 16
N_ADA = 6
EPS = 1e-6

LANES = 128
SUBLANES = 8
VMEM_LIMIT = 56 * 1024 * 1024

C_QK, C_V, C_G, C_CB, C_CC, C_CH, C_AL, C_END = 0, 512, 1024, 1536, 2048, 2560, 3072, 3200

TOKEN_TILE = 256
ROUTE_TILE = 512
PEER_TOKEN_TILE = 512
EXPERT_BLOCK = 2048
GATE_HALF = PEER_NKEYS // 2
GATE_PITCH = GATE_HALF + SUBLANES
GATE_UNROLL = 64


def _dot(a, b):
    return jnp.dot(a, b, preferred_element_type=F32)


def _dot_nt(a, b):
    return lax.dot_general(a, b, (((1,), (1,)), ((), ())), preferred_element_type=F32)


def _dot_tn(a, b):
    return lax.dot_general(a, b, (((0,), (0,)), ((), ())), preferred_element_type=F32)


def _split2(x):
    hi = x.astype(BF16)
    lo = (x - hi.astype(F32)).astype(BF16)
    return hi, lo


def _split3(x):
    hi = x.astype(BF16)
    r = x - hi.astype(F32)
    mid = r.astype(BF16)
    lo = (r - mid.astype(F32)).astype(BF16)
    return hi, mid, lo


def _dot_f32(a, b):
    a_hi, a_lo = _split2(a)
    b_hi, b_lo = _split2(b)
    return _dot(a_hi, b_hi) + _dot(a_hi, b_lo) + _dot(a_lo, b_hi)


def _bf16_bits(x):
    return lax.bitcast_convert_type(x.astype(BF16).astype(F32), jnp.uint32)


def _rms(x):
    return lax.rsqrt(jnp.mean(x * x, axis=-1, keepdims=True) + EPS)


def _sigmoid(x):
    return 1.0 / (1.0 + jnp.exp(-x))


def _log_sigmoid(x):
    return jnp.minimum(x, 0.0) - jnp.log1p(jnp.exp(-jnp.abs(x)))


GELU_C1 = 0.7978845608028654
GELU_C3 = GELU_C1 * 0.044715


def _half_gated_gelu(x, half_gate):
    x = x.astype(BF16)
    gx = half_gate.astype(BF16) * x
    return gx * jnp.tanh(x * (GELU_C1 + GELU_C3 * (x * x))) + gx


def _ada_kernel(c_ref, w_ref, b_ref, o_ref):
    c = c_ref[...]
    o_ref[...] = _dot_f32(c * _sigmoid(c), w_ref[...]) + b_ref[...]


def _ada(c_pad, w_ada, b_ada):
    n_col = N_ADA * D_MODEL
    blk = 1536
    return pl.pallas_call(
        _ada_kernel,
        out_shape=jax.ShapeDtypeStruct((SUBLANES, n_col), F32),
        grid=(n_col // blk,),
        in_specs=[
            pl.BlockSpec((SUBLANES, D_MODEL), lambda j: (0, 0)),
            pl.BlockSpec((D_MODEL, blk), lambda j: (0, j)),
            pl.BlockSpec((1, blk), lambda j: (0, j)),
        ],
        out_specs=pl.BlockSpec((SUBLANES, blk), lambda j: (0, j)),
        compiler_params=pltpu.CompilerParams(dimension_semantics=("arbitrary",), vmem_limit_bytes=VMEM_LIMIT),
        name="ada",
    )(c_pad, w_ada, b_ada)


def _gla_tile(q, k, v, la, st_ref, forward):
    tt = q.shape[0]
    n_chunk = tt // GLA_CHUNK
    row = lax.broadcasted_iota(jnp.int32, (tt, tt), 0)
    col = lax.broadcasted_iota(jnp.int32, (tt, tt), 1)
    ordered = (row >= col) if forward else (row <= col)
    causal = ordered & (row // GLA_CHUNK == col // GLA_CHUNK)
    tri = jnp.where(causal, 1.0, 0.0).astype(BF16)
    la_hi, la_mid, la_lo = _split3(la)
    b = _dot(tri, la_hi) + _dot(tri, la_mid) + _dot(tri, la_lo)
    end_rows = [ci * GLA_CHUNK + (GLA_CHUNK - 1 if forward else 0) for ci in range(n_chunk)]
    b_end = jnp.concatenate([jnp.broadcast_to(b[r:r + 1, :], (GLA_CHUNK, GLA_KW)) for r in end_rows], axis=0)
    q_dec = q * (jnp.exp(b) * (GLA_DK ** -0.5))
    k_dec = k * jnp.exp(-b)
    k_end = k * jnp.exp(b_end - b)
    decay = [jnp.exp(b[r:r + 1, :]) for r in end_rows]
    lane = lax.broadcasted_iota(jnp.int32, (1, LANES), 1)
    scan = range(n_chunk) if forward else range(n_chunk - 1, -1, -1)
    outs = []
    for head in range(GLA_HEADS):
        pair, half = head // 2, head % 2
        sl = slice(pair * LANES, (pair + 1) * LANES)
        in_head = (lane // GLA_DK) == half
        qd = jnp.where(in_head, q_dec[:, sl], 0.0).astype(BF16)
        kd = k_dec[:, sl].astype(BF16)
        ke = k_end[:, sl].astype(BF16)
        v_h = v[:, head * GLA_DV:(head + 1) * GLA_DV].astype(BF16)
        scores = jnp.where(causal, _dot_nt(qd, kd), 0.0)
        o_intra = _dot(scores.astype(BF16), v_h)
        st = st_ref[head]
        parts = [None] * n_chunk
        for ci in scan:
            rows = slice(ci * GLA_CHUNK, (ci + 1) * GLA_CHUNK)
            parts[ci] = o_intra[rows] + _dot_nt(qd[rows], st.astype(BF16))
            st = st * decay[ci][:, sl] + _dot_tn(v_h[rows], ke[rows])
        st_ref[head] = st
        outs.append(jnp.concatenate(parts, axis=0))
    return jnp.concatenate(outs, axis=-1)


def _mix1_kernel(x_ref, xp_ref, xn_ref, mod_ref, n1g_ref, win_ref, wdec_ref, bdec_ref, convw_ref, cng_ref,
                 mgrp_ref, qk_ref, v_ref, lab_ref, of_ref, gy_ref, st_s, *, tt, nt):
    i = pl.program_id(1)

    @pl.when(i == 0)
    def _():
        st_s[...] = jnp.zeros_like(st_s)

    n1g = n1g_ref[...]
    cw = convw_ref[...]
    row = lax.broadcasted_iota(jnp.int32, (tt, 1), 0)
    mod = mod_ref[0]
    sh1, sc1 = mod[0:1], mod[1:2]

    def norm_mod(x):
        return ((x * _rms(x)) * n1g) * (1.0 + sc1) + sh1

    h = norm_mod(x_ref[0]).astype(BF16)
    qk_ref[0] = _dot(h, win_ref[:, C_QK:C_V])
    v_ref[0] = _dot(h, win_ref[:, C_V:C_G])
    gy_ref[0, :, 0:GLA_WIDTH] = _dot(h, win_ref[:, C_G:C_CB])

    hh = norm_mod(jnp.concatenate([xp_ref[0], xn_ref[0]], axis=0)).astype(BF16)
    h_ext = jnp.concatenate([h, hh], axis=0)
    z_ext = _dot(h_ext, win_ref[:, C_CC:C_CH]) * _dot(h_ext, win_ref[:, C_CH:C_AL])
    z = z_ext[0:tt]
    z_prev = jnp.where(i > 0, z_ext[tt + SUBLANES - 1:tt + SUBLANES, :], 0.0)
    z_next = jnp.where(i < nt - 1, z_ext[tt + SUBLANES:tt + SUBLANES + 1, :], 0.0)
    z_m1 = jnp.where(row == 0, z_prev, pltpu.roll(z, 1, axis=0))
    z_p1 = jnp.where(row == tt - 1, z_next, pltpu.roll(z, tt - 1, axis=0))
    conv = cw[0:1] * z_m1 + cw[1:2] * z + cw[2:3] * z_p1
    yc = _dot(h, win_ref[:, C_CB:C_CC]) * conv
    sq_hi, sq_lo = _split2(yc * yc)
    ss = _dot(sq_hi, mgrp_ref[...]) + _dot(sq_lo, mgrp_ref[...])
    gy_ref[0, :, GLA_WIDTH:] = (yc * lax.rsqrt(ss * (1.0 / CONV_GDIM) + EPS)) * cng_ref[...]

    zd = _dot_f32(_dot(h, win_ref[:, C_AL:C_END]), wdec_ref[...]) + bdec_ref[...]
    la = _log_sigmoid(zd) * (1.0 / GLA_GATE_NORM)
    lab_ref[0] = la[:, GLA_KW:]
    of_ref[0] = _gla_tile(qk_ref[0, :, 0:GLA_KW], qk_ref[0, :, GLA_KW:], v_ref[0], la[:, 0:GLA_KW],
                           st_s, True)


def _mix1(x, mod, n1g, win, wdec, bdec, convw, cng, mgrp, tt):
    bsz, seq, _ = x.shape
    nt = seq // tt
    hb = tt // SUBLANES
    last_hb = seq // SUBLANES - 1
    const = lambda shape: pl.BlockSpec(shape, lambda b, i: tuple(0 for _ in shape))
    tile = lambda w: pl.BlockSpec((1, tt, w), lambda b, i: (b, i, 0))
    out_w = (2 * GLA_KW, GLA_WIDTH, GLA_KW, GLA_WIDTH, GLA_WIDTH + CONV_WIDTH)
    return pl.pallas_call(
        functools.partial(_mix1_kernel, tt=tt, nt=nt),
        out_shape=tuple(jax.ShapeDtypeStruct((bsz, seq, w), F32) for w in out_w),
        grid=(bsz, nt),
        in_specs=[
            tile(D_MODEL),
            pl.BlockSpec((1, SUBLANES, D_MODEL), lambda b, i: (b, jnp.maximum(i * hb - 1, 0), 0)),
            pl.BlockSpec((1, SUBLANES, D_MODEL), lambda b, i: (b, jnp.minimum((i + 1) * hb, last_hb), 0)),
            pl.BlockSpec((1, SUBLANES, D_MODEL), lambda b, i: (b, 0, 0)),
            const((1, D_MODEL)),
            const((D_MODEL, C_END)),
            const((LANES, 2 * GLA_KW)),
            const((1, 2 * GLA_KW)),
            const((3, CONV_WIDTH)),
            const((1, CONV_WIDTH)),
            const((CONV_WIDTH, CONV_WIDTH)),
        ],
        out_specs=tuple(tile(w) for w in out_w),
        scratch_shapes=[pltpu.VMEM((GLA_HEADS, GLA_DV, LANES), F32)],
        compiler_params=pltpu.CompilerParams(dimension_semantics=("arbitrary", "arbitrary"),
                                             vmem_limit_bytes=VMEM_LIMIT),
        name="mix1",
    )(x, x, x, mod, n1g, win, wdec, bdec, convw, cng, mgrp)


def _mix2_kernel(x_ref, mod_ref, qk_ref, v_ref, lab_ref, of_ref, gy_ref, gng_ref, wout_ref, n2g_ref,
                 x1_ref, h2_ref, h8_ref, hs_ref, st_s):
    i = pl.program_id(1)

    @pl.when(i == 0)
    def _():
        st_s[...] = jnp.zeros_like(st_s)

    gng = gng_ref[...]
    mod = mod_ref[0]
    gt1, sh2, sc2 = mod[2:3], mod[3:4], mod[4:5]
    o = of_ref[0] + _gla_tile(qk_ref[0, :, 0:GLA_KW], qk_ref[0, :, GLA_KW:], v_ref[0], lab_ref[0],
                               st_s, False)
    y = None
    for head in range(GLA_HEADS):
        sl = slice(head * GLA_DV, (head + 1) * GLA_DV)
        oh = o[:, sl]
        g = gy_ref[0, :, sl]
        yh = ((oh * _rms(oh)) * gng) * (g * _sigmoid(g))
        part = _dot(yh.astype(BF16), wout_ref[sl, :])
        y = part if y is None else y + part
    y = y + _dot(gy_ref[0, :, GLA_WIDTH:].astype(BF16), wout_ref[GLA_WIDTH:, :])
    x1 = x_ref[0] + gt1 * y
    x1_ref[0] = x1
    h2 = ((x1 * _rms(x1)) * n2g_ref[...]) * (1.0 + sc2) + sh2
    h2_ref[0] = h2.astype(BF16)
    amax = jnp.maximum(jnp.max(jnp.abs(h2), axis=-1, keepdims=True), F8_TINY)
    h8_ref[0] = (h2 * (F8_TOP / amax)).astype(F8)
    hs_ref[0] = jnp.broadcast_to(amax * (1.0 / F8_TOP), (h2.shape[0], LANES))


def _mix2(x, mod, qk, v, lab, of, gy, gng, wout, n2g, tt):
    bsz, seq, _ = x.shape
    nt = seq // tt
    const = lambda shape: pl.BlockSpec(shape, lambda b, i: tuple(0 for _ in shape))
    tile = lambda w: pl.BlockSpec((1, tt, w), lambda b, i: (b, nt - 1 - i, 0))
    return pl.pallas_call(
        _mix2_kernel,
        out_shape=(jax.ShapeDtypeStruct((bsz, seq, D_MODEL), F32), jax.ShapeDtypeStruct((bsz, seq, D_MODEL), BF16),
                   jax.ShapeDtypeStruct((bsz, seq, D_MODEL), F8), jax.ShapeDtypeStruct((bsz, seq, LANES), F32)),
        grid=(bsz, nt),
        in_specs=[
            tile(D_MODEL),
            pl.BlockSpec((1, SUBLANES, D_MODEL), lambda b, i: (b, 0, 0)),
            tile(2 * GLA_KW), tile(GLA_WIDTH), tile(GLA_KW), tile(GLA_WIDTH), tile(GLA_WIDTH + CONV_WIDTH),
            const((1, GLA_DV)),
            const((D_MODEL, D_MODEL)),
            const((1, D_MODEL)),
        ],
        out_specs=(tile(D_MODEL), tile(D_MODEL), tile(D_MODEL), tile(LANES)),
        scratch_shapes=[pltpu.VMEM((GLA_HEADS, GLA_DV, LANES), F32)],
        compiler_params=pltpu.CompilerParams(dimension_semantics=("arbitrary", "arbitrary"),
                                             vmem_limit_bytes=VMEM_LIMIT),
        name="mix2",
    )(x, mod, qk, v, lab, of, gy, gng, wout, n2g)


NEG_INF = float("-inf")
KEY_ROWS = 2 * PEER_NKEYS * PEER_HEADS
VREG = (SUBLANES, LANES)


def _keyproj_kernel(wq_ref, sk_ref, o_ref):
    w_hi, w_lo = _split2(wq_ref[...])
    k_hi, k_lo = _split2(sk_ref[0])
    o_ref[0] = _dot_nt(k_hi, w_hi) + _dot_nt(k_hi, w_lo) + _dot_nt(k_lo, w_hi)


def _keyproj(wq, subkeys):
    n_hp = 2 * PEER_HEADS
    return pl.pallas_call(
        _keyproj_kernel,
        out_shape=jax.ShapeDtypeStruct((n_hp, PEER_NKEYS, D_MODEL), F32),
        grid=(n_hp,),
        in_specs=[pl.BlockSpec((D_MODEL, PEER_DQH), lambda j: (0, j)),
                  pl.BlockSpec((1, PEER_NKEYS, PEER_DQH), lambda j: (j, 0, 0))],
        out_specs=pl.BlockSpec((1, PEER_NKEYS, D_MODEL), lambda j: (j, 0, 0)),
        compiler_params=pltpu.CompilerParams(dimension_semantics=("arbitrary",), vmem_limit_bytes=VMEM_LIMIT),
        name="keyproj",
    )(wq, subkeys)


def _sort16_network():
    pairs = []

    def merge(lo, hi, r):
        step = r * 2
        if step < hi - lo:
            merge(lo, hi, step)
            merge(lo + r, hi, step)
            pairs.extend((i, i + r) for i in range(lo + r, hi - r, step))
        else:
            pairs.append((lo, lo + r))

    def sort(lo, hi):
        if hi - lo >= 1:
            mid = lo + (hi - lo) // 2
            sort(lo, mid)
            sort(mid + 1, hi)
            merge(lo, hi, 1)

    sort(0, PEER_TOPK - 1)
    return tuple(pairs)


SORT16 = _sort16_network()
CAND_LISTS = ([[(0, k2) for k2 in range(PEER_TOPK)]]
              + [[(k1, k2) for k2 in range(PEER_TOPK // (k1 + 1))] for k1 in range(1, 8)]
              + [[(k1, 0) for k1 in range(8, PEER_TOPK)]])
CANDS = [c for lst in CAND_LISTS for c in lst]


def _tree(op, xs):
    xs = list(xs)
    while len(xs) > 1:
        xs = [op(xs[i], xs[i + 1]) for i in range(0, len(xs) - 1, 2)] + ([xs[-1]] if len(xs) % 2 else [])
    return xs[0]


def _ce(v, x, i, j):
    c = v[j] > v[i]
    v[i], v[j] = jnp.maximum(v[i], v[j]), jnp.minimum(v[i], v[j])
    x[i], x[j] = jnp.where(c, x[j], x[i]), jnp.where(c, x[i], x[j])


def _merge_top16(va, xa, vb, xb, sort_result):
    v, x = list(va), list(xa)
    for j in range(PEER_TOPK - len(vb), PEER_TOPK):
        b = PEER_TOPK - 1 - j
        c = vb[b] > va[j]
        v[j] = jnp.maximum(va[j], vb[b])
        x[j] = jnp.where(c, xb[b], xa[j])
    if sort_result:
        for d in (8, 4, 2, 1):
            for i in range(PEER_TOPK):
                if not i & d:
                    _ce(v, x, i, i + d)
    return v, x


def _any(flag):
    return jnp.max(flag) > 0.0


def _top16_keys(key, work_v, work_x, sv_ref, si_ref):
    for g in range(PEER_NKEYS // PEER_TOPK):
        v = [key(PEER_TOPK * g + i) for i in range(PEER_TOPK)]
        x = [float(PEER_TOPK * g + i) for i in range(PEER_TOPK)]
        for i, j in SORT16:
            _ce(v, x, i, j)
        for i in range(PEER_TOPK):
            work_v[PEER_TOPK * g + i] = v[i]
            work_x[PEER_TOPK * g + i] = x[i]
    for span in (1, 2, 4):
        for g in range(0, PEER_NKEYS // PEER_TOPK, 2 * span):
            a, b = PEER_TOPK * g, PEER_TOPK * (g + span)
            v, x = _merge_top16([work_v[a + i] for i in range(PEER_TOPK)], [work_x[a + i] for i in range(PEER_TOPK)],
                                [work_v[b + i] for i in range(PEER_TOPK)], [work_x[b + i] for i in range(PEER_TOPK)],
                                True)
            for i in range(PEER_TOPK):
                if span == 4:
                    sv_ref[i], si_ref[i] = v[i], x[i]
                else:
                    work_v[a + i], work_x[a + i] = v[i], x[i]
    repeat = _tree(jnp.maximum, [jnp.where(v[r] == v[r + 1], 1.0, 0.0) for r in range(PEER_TOPK - 1)])
    n_ge = _tree(jnp.add, [jnp.where(key(k) >= v[PEER_TOPK - 1], 1.0, 0.0) for k in range(PEER_NKEYS)])

    @pl.when(_any(jnp.maximum(repeat, jnp.where(n_ge > float(PEER_TOPK), 1.0, 0.0))))
    def _():
        for k in range(PEER_NKEYS):
            work_v[k] = key(k)

        def extract(r, carry):
            vals = [work_v[k] for k in range(PEER_NKEYS)]
            m = _tree(jnp.maximum, vals)
            idx = _tree(jnp.minimum, [jnp.where(vals[k] == m, float(k), float(PEER_NKEYS)) for k in range(PEER_NKEYS)])
            for k in range(PEER_NKEYS):
                work_v[k] = jnp.where(idx == float(k), NEG_INF, vals[k])
            sv_ref[r] = m
            si_ref[r] = idx
            return carry

        lax.fori_loop(0, PEER_TOPK, extract, 0)


def _top16_sums(sv_s, si_s, work_v, work_x, rv_s, re_s):
    sv0 = [sv_s[0, r] for r in range(PEER_TOPK)]
    sv1 = [sv_s[1, r] for r in range(PEER_TOPK)]
    e_hi = [si_s[0, r] * float(PEER_NKEYS) for r in range(PEER_TOPK)]
    si1 = [si_s[1, r] for r in range(PEER_TOPK)]
    val = {c: sv0[c[0]] + sv1[c[1]] for c in CANDS}
    eid = {c: e_hi[c[0]] + si1[c[1]] for c in CANDS}
    v, x = [val[c] for c in CAND_LISTS[0]], [eid[c] for c in CAND_LISTS[0]]
    for n, lst in enumerate(CAND_LISTS[1:]):
        v, x = _merge_top16(v, x, [val[c] for c in lst], [eid[c] for c in lst], n < len(CAND_LISTS) - 2)
    for r in range(PEER_TOPK):
        rv_s[r], re_s[r] = v[r], x[r]
    low = _tree(jnp.minimum, v)
    n_ge = _tree(jnp.add, [jnp.where(val[c] >= low, 1.0, 0.0) for c in CANDS])

    @pl.when(_any(jnp.where(n_ge > float(PEER_TOPK), 1.0, 0.0)))
    def _():
        for n, c in enumerate(CANDS):
            work_v[n], work_x[n] = val[c], eid[c]

        def extract(r, carry):
            vals = [work_v[n] for n in range(len(CANDS))]
            m = _tree(jnp.maximum, vals)
            code = [float(c[0] * PEER_TOPK + c[1]) for c in CANDS]
            sel = _tree(jnp.minimum, [jnp.where(vals[n] == m, code[n], 1e9) for n in range(len(CANDS))])
            hit = [sel == code[n] for n in range(len(CANDS))]
            rv_s[r] = m
            re_s[r] = _tree(jnp.add, [jnp.where(hit[n], work_x[n], 0.0) for n in range(len(CANDS))])
            for n in range(len(CANDS)):
                work_v[n] = jnp.where(hit[n], NEG_INF, vals[n])
            return carry

        lax.fori_loop(0, PEER_TOPK, extract, 0)


def _route_kernel(h2_ref, wpt_ref, a_ref, b_ref, g_ref, sc_s, work_v, work_x, sv_s, si_s, rv_s, re_s,
                  pa_s, pb_s, pg_s, *, tt):
    sc_s[...] = _dot_nt(wpt_ref[...], h2_ref[0])

    def lane_tile(c, carry):
        lanes = pl.ds(pl.multiple_of(c * LANES, LANES), LANES)
        for p in range(2):
            def key(k, p=p):
                r0 = (p * PEER_NKEYS + k) * SUBLANES
                return sc_s[r0:r0 + SUBLANES, lanes]
            _top16_keys(key, work_v, work_x, sv_s.at[p], si_s.at[p])
        _top16_sums(sv_s, si_s, work_v, work_x, rv_s, re_s)
        v = [rv_s[r] for r in range(PEER_TOPK)]
        m = _tree(jnp.maximum, v)
        ex = [jnp.exp(vr - m) for vr in v]
        inv = 1.0 / _tree(jnp.add, ex)
        for r in range(PEER_TOPK):
            e = re_s[r]
            i1 = jnp.floor(e * (1.0 / PEER_NKEYS))
            rows = slice(r * SUBLANES, (r + 1) * SUBLANES)
            pa_s[rows, :] = i1
            pb_s[rows, :] = e - i1 * float(PEER_NKEYS)
            pg_s[rows, :] = ex[r] * inv
        toks = pl.ds(pl.multiple_of(c * LANES, LANES), LANES)
        a_ref[0, toks, :] = jnp.transpose(pa_s[...])
        b_ref[0, toks, :] = jnp.transpose(pb_s[...])
        g_ref[0, toks, :] = jnp.transpose(pg_s[...])
        return carry

    lax.fori_loop(0, tt // LANES, lane_tile, 0)


def _route(h2, wpt, tt):
    bsz, seq, _ = h2.shape
    n_pair = PEER_HEADS * PEER_TOPK
    tile = lambda w: pl.BlockSpec((1, tt, w), lambda b, i: (b, i, 0))
    vregs = lambda *lead: pltpu.VMEM(lead + VREG, F32)
    return pl.pallas_call(
        functools.partial(_route_kernel, tt=tt),
        out_shape=tuple(jax.ShapeDtypeStruct((bsz, seq, n_pair), F32) for _ in range(3)),
        grid=(bsz, seq // tt),
        in_specs=[tile(D_MODEL), pl.BlockSpec((KEY_ROWS, D_MODEL), lambda b, i: (0, 0))],
        out_specs=(tile(n_pair), tile(n_pair), tile(n_pair)),
        scratch_shapes=[pltpu.VMEM((KEY_ROWS, tt), F32), vregs(PEER_NKEYS), vregs(PEER_NKEYS),
                        vregs(2, PEER_TOPK), vregs(2, PEER_TOPK), vregs(PEER_TOPK), vregs(PEER_TOPK),
                        pltpu.VMEM((n_pair, LANES), F32), pltpu.VMEM((n_pair, LANES), F32),
                        pltpu.VMEM((n_pair, LANES), F32)],
        compiler_params=pltpu.CompilerParams(dimension_semantics=("arbitrary", "arbitrary"),
                                             vmem_limit_bytes=VMEM_LIMIT),
        name="route",
    )(h2, wpt)


def _peer_kernel(h8_ref, hs_ref, x1_ref, mod_ref, a_ref, b_ref, g_ref, ut_ref, us_ref, v_ref, vs_ref, nfg_ref, y_ref,
                 gate_s, acc_s, *, tt, eb, ne):
    j = pl.program_id(2)
    n_grp = eb // PEER_NKEYS
    steps_per_half = ne // 2

    @pl.when(j == 0)
    def _():
        acc_s[...] = jnp.zeros_like(acc_s)
        sub = lax.broadcasted_iota(jnp.int32, (PEER_NKEYS, PEER_NKEYS), 0).astype(F32).astype(BF16)
        zero = jnp.zeros((PEER_NKEYS, PEER_NKEYS), BF16)
        half = jnp.full((PEER_NKEYS, PEER_NKEYS), 0.5, BF16)

        def onehots(t):
            bc = lambda ref: jnp.broadcast_to(ref[0, pl.ds(t, 1), :], (PEER_NKEYS, PEER_NKEYS)).astype(BF16)
            at = jnp.where(sub == bc(a_ref), half, zero)
            cbt = jnp.where(sub == bc(b_ref), bc(g_ref), zero)
            return at, cbt

        def build(p, carry):
            for u in range(GATE_UNROLL):
                t0 = (p * GATE_UNROLL + u) * 2
                at0, cbt0 = onehots(t0)
                at1, cbt1 = onehots(t0 + 1)
                lhs = jnp.concatenate([at0, at1], axis=1)
                rhs = jnp.concatenate([jnp.concatenate([cbt0.T, zero], axis=1),
                                       jnp.concatenate([zero, cbt1.T], axis=1)], axis=0)
                tiles = _dot(lhs, rhs)
                for k in range(2):
                    lo = tiles[0:GATE_HALF, k * PEER_NKEYS:(k + 1) * PEER_NKEYS]
                    hi = tiles[GATE_HALF:, k * PEER_NKEYS:(k + 1) * PEER_NKEYS]
                    r0 = pl.multiple_of((t0 + k) * GATE_PITCH, SUBLANES)
                    gate_s[pl.ds(r0, GATE_HALF), :] = (_bf16_bits(lo) >> 16) | _bf16_bits(hi)
            return carry

        lax.fori_loop(0, tt // (2 * GATE_UNROLL), build, 0)

    def expert_block(half):
        s = _dot(h8_ref[0], ut_ref[...])
        row_scale = hs_ref[0]
        parts = []
        for gi in range(n_grp):
            lanes = slice(gi * PEER_NKEYS, (gi + 1) * PEER_NKEYS)
            r = (j - half * steps_per_half) * n_grp + gi
            packed = gate_s[pl.ds(r, tt, stride=GATE_PITCH), :]
            word = (packed & jnp.uint32(0xFFFF0000)) if half else (packed << 16)
            half_gate = lax.bitcast_convert_type(word, F32)
            act = s[:, lanes] * (row_scale * us_ref[:, lanes])
            parts.append(_half_gated_gelu(act, half_gate).astype(BF16))
        g = jnp.concatenate(parts, axis=1)
        g_max = jnp.maximum(jnp.max(jnp.abs(g), axis=-1, keepdims=True).astype(F32), F8_TINY)
        g_scale = (F8_TOP / g_max).astype(BF16)
        acc_s[...] += _dot((g * g_scale).astype(F8), v_ref[...]) * (1.0 / g_scale.astype(F32))

    @pl.when(j < steps_per_half)
    def _():
        expert_block(0)

    @pl.when(j >= steps_per_half)
    def _():
        expert_block(1)

    @pl.when(j == ne - 1)
    def _():
        gt2 = mod_ref[0][5:6]
        x2 = x1_ref[0] + gt2 * (acc_s[...] * vs_ref[...])
        y_ref[0] = (x2 * _rms(x2)) * nfg_ref[...]


def _peer(h8, hs, x1, mod, a_idx, b_idx, gates, ut8, us, v8, vs, nfg, tt, eb):
    bsz, seq, _ = h8.shape
    n_exp = v8.shape[0]
    ne = n_exp // eb
    assert ne % 2 == 0 and (GATE_HALF * PEER_NKEYS) % eb == 0 and tt % (2 * GATE_UNROLL) == 0
    n_pair = PEER_HEADS * PEER_TOPK
    tile = lambda w: pl.BlockSpec((1, tt, w), lambda b, i, j: (b, i, 0))
    return pl.pallas_call(
        functools.partial(_peer_kernel, tt=tt, eb=eb, ne=ne),
        out_shape=jax.ShapeDtypeStruct((bsz, seq, D_MODEL), F32),
        grid=(bsz, seq // tt, ne),
        in_specs=[
            tile(D_MODEL), tile(LANES), tile(D_MODEL),
            pl.BlockSpec((1, SUBLANES, D_MODEL), lambda b, i, j: (b, 0, 0)),
            tile(n_pair), tile(n_pair), tile(n_pair),
            pl.BlockSpec((D_MODEL, eb), lambda b, i, j: (0, j)),
            pl.BlockSpec((1, eb), lambda b, i, j: (0, j)),
            pl.BlockSpec((eb, D_MODEL), lambda b, i, j: (j, 0)),
            pl.BlockSpec((1, D_MODEL), lambda b, i, j: (0, 0)),
            pl.BlockSpec((1, D_MODEL), lambda b, i, j: (0, 0)),
        ],
        out_specs=tile(D_MODEL),
        scratch_shapes=[pltpu.VMEM((tt * GATE_PITCH, PEER_NKEYS), jnp.uint32), pltpu.VMEM((tt, D_MODEL), F32)],
        compiler_params=pltpu.CompilerParams(dimension_semantics=("arbitrary", "arbitrary", "arbitrary"),
                                             vmem_limit_bytes=VMEM_LIMIT),
        name="peer",
    )(h8, hs, x1, mod, a_idx, b_idx, gates, ut8, us, v8, vs, nfg)


def _uquant_kernel(ut_ref, u8_ref, us_ref):
    ut = ut_ref[...]
    amax = jnp.maximum(jnp.max(jnp.abs(ut), axis=0, keepdims=True), F8_TINY)
    u8_ref[...] = (ut * (F8_TOP / amax)).astype(F8)
    us_ref[...] = amax * (1.0 / F8_TOP)


def _uquant(ut, eb):
    d, n_exp = ut.shape
    return pl.pallas_call(
        _uquant_kernel,
        out_shape=(jax.ShapeDtypeStruct((d, n_exp), F8), jax.ShapeDtypeStruct((1, n_exp), F32)),
        grid=(n_exp // eb,),
        in_specs=[pl.BlockSpec((d, eb), lambda j: (0, j))],
        out_specs=(pl.BlockSpec((d, eb), lambda j: (0, j)), pl.BlockSpec((1, eb), lambda j: (0, j))),
        compiler_params=pltpu.CompilerParams(dimension_semantics=("arbitrary",), vmem_limit_bytes=VMEM_LIMIT),
        name="uquant",
    )(ut)


def _vmax_kernel(v_ref, m_ref):
    @pl.when(pl.program_id(0) == 0)
    def _():
        m_ref[...] = jnp.zeros_like(m_ref)

    m_ref[...] = jnp.maximum(m_ref[...], jnp.max(jnp.abs(v_ref[...]), axis=0, keepdims=True))


def _vcast_kernel(v_ref, m_ref, v8_ref):
    v8_ref[...] = (v_ref[...] * (F8_TOP / jnp.maximum(m_ref[...], F8_TINY))).astype(F8)


def _vquant(v, eb):
    n_exp, d = v.shape
    params = pltpu.CompilerParams(dimension_semantics=("arbitrary",), vmem_limit_bytes=VMEM_LIMIT)
    blk = pl.BlockSpec((eb, d), lambda j: (j, 0))
    row = pl.BlockSpec((1, d), lambda j: (0, 0))
    vmax = pl.pallas_call(_vmax_kernel, out_shape=jax.ShapeDtypeStruct((1, d), F32), grid=(n_exp // eb,),
                          in_specs=[blk], out_specs=row, compiler_params=params, name="vmax")(v)
    v8 = pl.pallas_call(_vcast_kernel, out_shape=jax.ShapeDtypeStruct((n_exp, d), F8), grid=(n_exp // eb,),
                        in_specs=[blk, row], out_specs=blk, compiler_params=params, name="vcast")(v, vmax)
    return v8, jnp.maximum(vmax, F8_TINY) * (1.0 / F8_TOP)


def _prep_weights(norm1_g, norm2_g, w_in, w_dec_f, b_dec_f, w_dec_b, b_dec_b, gla_norm_g, conv_w, conv_norm_g,
                  w_out, peer_wq, peer_subkeys, peer_u, peer_v, normf_g):
    w = w_in[0]
    q, k, v, g, alf, alb, cb, cc, ch = jnp.split(w, (256, 512, 1024, 1536, 1552, 1568, 2080, 2592), axis=-1)
    pad = jnp.zeros((D_MODEL, C_END - C_AL - 2 * GLA_RANK), F32)
    win = jnp.concatenate([q, k, v, g, cb, cc, ch, alf, alb, pad], axis=-1).astype(BF16)
    wdec = jnp.zeros((LANES, 2 * GLA_KW), F32)
    wdec = wdec.at[0:GLA_RANK, 0:GLA_KW].set(w_dec_f[0]).at[GLA_RANK:2 * GLA_RANK, GLA_KW:].set(w_dec_b[0])
    bdec = jnp.concatenate([b_dec_f[0], b_dec_b[0]])[None, :]
    grp = jnp.arange(CONV_WIDTH) // CONV_GDIM
    mgrp = (grp[:, None] == grp[None, :]).astype(BF16)
    wp = _keyproj(peer_wq[0], peer_subkeys[0].reshape(2 * PEER_HEADS, PEER_NKEYS, PEER_DQH))
    wpt = wp.reshape(PEER_HEADS, 2, PEER_NKEYS, D_MODEL).transpose(1, 2, 0, 3).reshape(KEY_ROWS, D_MODEL)
    ut8, us = _uquant(jnp.transpose(peer_u[0]), EXPERT_BLOCK)
    v8, vs = _vquant(peer_v[0], EXPERT_BLOCK)
    return dict(
        n1g=norm1_g[0][None, :], n2g=norm2_g[0][None, :], win=win, wdec=wdec, bdec=bdec,
        convw=conv_w[0], cng=conv_norm_g[0][None, :], mgrp=mgrp, gng=gla_norm_g[0][None, :],
        wout=w_out[0].astype(BF16), wpt=wpt.astype(BF16),
        ut8=ut8, us=us, v8=v8, vs=vs, nfg=normf_g[None, :],
    )


def _trunk(x, mod, w, tt, rtt, ptt, eb):
    qk, v, lab, of, gy = _mix1(x, mod, w["n1g"], w["win"], w["wdec"], w["bdec"], w["convw"], w["cng"], w["mgrp"], tt)
    x1, h2, h8, hs = _mix2(x, mod, qk, v, lab, of, gy, w["gng"], w["wout"], w["n2g"], tt)
    a_idx, b_idx, gates = _route(h2, w["wpt"], rtt)
    return _peer(h8, hs, x1, mod, a_idx, b_idx, gates, w["ut8"], w["us"], w["v8"], w["vs"], w["nfg"], ptt, eb)


def kernel(x_prompt, x_sample, c_prompt, c_sample, norm1_g, norm2_g, w_ada, b_ada, w_in, w_dec_f, b_dec_f,
           w_dec_b, b_dec_b, gla_norm_g, conv_w, conv_norm_g, w_out, peer_wq, peer_subkeys, peer_u, peer_v,
           normf_g):
    w = _prep_weights(norm1_g, norm2_g, w_in, w_dec_f, b_dec_f, w_dec_b, b_dec_b, gla_norm_g, conv_w,
                      conv_norm_g, w_out, peer_wq, peer_subkeys, peer_u, peer_v, normf_g)
    nb_p, nb_s = c_prompt.shape[0], c_sample.shape[0]
    c_all = jnp.concatenate([c_prompt, c_sample, jnp.zeros((SUBLANES - nb_p - nb_s, D_MODEL), F32)], axis=0)
    ada = _ada(c_all, w_ada[0], b_ada[0][None, :])
    mod = jnp.pad(ada.reshape(SUBLANES, N_ADA, D_MODEL), ((0, 0), (0, SUBLANES - N_ADA), (0, 0)))
    y_prompt = _trunk(x_prompt, mod[:nb_p], w, TOKEN_TILE, ROUTE_TILE, PEER_TOKEN_TILE, EXPERT_BLOCK)
    y_sample = _trunk(x_sample, mod[nb_p:nb_p + nb_s], w, TOKEN_TILE, ROUTE_TILE, PEER_TOKEN_TILE, EXPERT_BLOCK)
    return (y_prompt, y_sample)
```

```python
import functools

import jax
import jax.numpy as jnp
from jax import lax
from jax.experimental import pallas as pl
from jax.experimental.pallas import tpu as pltpu

F32 = jnp.float32
BF16 = jnp.bfloat16
F8 = jnp.float8_e4m3fn
F8_TOP = 224.0
F8_TINY = 1e-30

D_MODEL = 1024
GLA_HEADS = 4
GLA_DV = 128
GLA_DK = 64
GLA_KW = GLA_HEADS * GLA_DK
GLA_WIDTH = GLA_HEADS * GLA_DV
GLA_RANK = 16
GLA_GATE_NORM = 16.0
GLA_CHUNK = 64
CONV_WIDTH = 512
CONV_GDIM = 64
PEER_HEADS = 8
PEER_NKEYS = 128
PEER_DQH = 128
PEER_TOPK = 16
N_ADA = 6
EPS = 1e-6

LANES = 128
SUBLANES = 8
VMEM_LIMIT = 56 * 1024 * 1024

C_QK, C_V, C_G, C_CB, C_CC, C_CH, C_AL, C_END = 0, 512, 1024, 1536, 2048, 2560, 3072, 3200

TOKEN_TILE = 256
ROUTE_TILE = 512
PEER_TOKEN_TILE = 512
EXPERT_BLOCK = 2048
GATE_HALF = PEER_NKEYS // 2
GATE_PITCH = GATE_HALF + SUBLANES
GATE_UNROLL = 64


def _dot(a, b):
    return jnp.dot(a, b, preferred_element_type=F32)


def _dot_nt(a, b):
    return lax.dot_general(a, b, (((1,), (1,)), ((), ())), preferred_element_type=F32)


def _dot_tn(a, b):
    return lax.dot_general(a, b, (((0,), (0,)), ((), ())), preferred_element_type=F32)


def _split2(x):
    hi = x.astype(BF16)
    lo = (x - hi.astype(F32)).astype(BF16)
    return hi, lo


def _split3(x):
    hi = x.astype(BF16)
    r = x - hi.astype(F32)
    mid = r.astype(BF16)
    lo = (r - mid.astype(F32)).astype(BF16)
    return hi, mid, lo


def _dot_f32(a, b):
    a_hi, a_lo = _split2(a)
    b_hi, b_lo = _split2(b)
    return _dot(a_hi, b_hi) + _dot(a_hi, b_lo) + _dot(a_lo, b_hi)


def _bf16_bits(x):
    return lax.bitcast_convert_type(x.astype(BF16).astype(F32), jnp.uint32)


def _rms(x):
    return lax.rsqrt(jnp.mean(x * x, axis=-1, keepdims=True) + EPS)


def _sigmoid(x):
    return 1.0 / (1.0 + jnp.exp(-x))


def _log_sigmoid(x):
    return jnp.minimum(x, 0.0) - jnp.log1p(jnp.exp(-jnp.abs(x)))


GELU_C1 = 0.7978845608028654
GELU_C3 = GELU_C1 * 0.044715


def _half_gated_gelu(x, half_gate):
    x = x.astype(BF16)
    gx = half_gate.astype(BF16) * x
    return gx * jnp.tanh(x * (GELU_C1 + GELU_C3 * (x * x))) + gx


def _ada_kernel(c_ref, w_ref, b_ref, o_ref):
    c = c_ref[...]
    o_ref[...] = _dot_f32(c * _sigmoid(c), w_ref[...]) + b_ref[...]


def _ada(c_pad, w_ada, b_ada):
    n_col = N_ADA * D_MODEL
    blk = 1536
    return pl.pallas_call(
        _ada_kernel,
        out_shape=jax.ShapeDtypeStruct((SUBLANES, n_col), F32),
        grid=(n_col // blk,),
        in_specs=[
            pl.BlockSpec((SUBLANES, D_MODEL), lambda j: (0, 0)),
            pl.BlockSpec((D_MODEL, blk), lambda j: (0, j)),
            pl.BlockSpec((1, blk), lambda j: (0, j)),
        ],
        out_specs=pl.BlockSpec((SUBLANES, blk), lambda j: (0, j)),
        compiler_params=pltpu.CompilerParams(dimension_semantics=("arbitrary",), vmem_limit_bytes=VMEM_LIMIT),
        name="ada",
    )(c_pad, w_ada, b_ada)


def _gla_tile(q, k, v, la, st_ref, forward):
    tt = q.shape[0]
    n_chunk = tt // GLA_CHUNK
    row = lax.broadcasted_iota(jnp.int32, (tt, tt), 0)
    col = lax.broadcasted_iota(jnp.int32, (tt, tt), 1)
    ordered = (row >= col) if forward else (row <= col)
    causal = ordered & (row // GLA_CHUNK == col // GLA_CHUNK)
    tri = jnp.where(causal, 1.0, 0.0).astype(BF16)
    la_hi, la_mid, la_lo = _split3(la)
    b = _dot(tri, la_hi) + _dot(tri, la_mid) + _dot(tri, la_lo)
    end_rows = [ci * GLA_CHUNK + (GLA_CHUNK - 1 if forward else 0) for ci in range(n_chunk)]
    b_end = jnp.concatenate([jnp.broadcast_to(b[r:r + 1, :], (GLA_CHUNK, GLA_KW)) for r in end_rows], axis=0)
    q_dec = q * (jnp.exp(b) * (GLA_DK ** -0.5))
    k_dec = k * jnp.exp(-b)
    k_end = k * jnp.exp(b_end - b)
    decay = [jnp.exp(b[r:r + 1, :]) for r in end_rows]
    lane = lax.broadcasted_iota(jnp.int32, (1, LANES), 1)
    scan = range(n_chunk) if forward else range(n_chunk - 1, -1, -1)
    outs = []
    for head in range(GLA_HEADS):
        pair, half = head // 2, head % 2
        sl = slice(pair * LANES, (pair + 1) * LANES)
        in_head = (lane // GLA_DK) == half
        qd = jnp.where(in_head, q_dec[:, sl], 0.0).astype(BF16)
        kd = k_dec[:, sl].astype(BF16)
        ke = k_end[:, sl].astype(BF16)
        v_h = v[:, head * GLA_DV:(head + 1) * GLA_DV].astype(BF16)
        scores = jnp.where(causal, _dot_nt(qd, kd), 0.0)
        o_intra = _dot(scores.astype(BF16), v_h)
        st = st_ref[head]
        parts = [None] * n_chunk
        for ci in scan:
            rows = slice(ci * GLA_CHUNK, (ci + 1) * GLA_CHUNK)
            parts[ci] = o_intra[rows] + _dot_nt(qd[rows], st.astype(BF16))
            st = st * decay[ci][:, sl] + _dot_tn(v_h[rows], ke[rows])
        st_ref[head] = st
        outs.append(jnp.concatenate(parts, axis=0))
    return jnp.concatenate(outs, axis=-1)


def _mix1_kernel(x_ref, xp_ref, xn_ref, mod_ref, n1g_ref, win_ref, wdec_ref, bdec_ref, convw_ref, cng_ref,
                 mgrp_ref, qk_ref, v_ref, lab_ref, of_ref, gy_ref, st_s, *, tt, nt):
    i = pl.program_id(1)

    @pl.when(i == 0)
    def _():
        st_s[...] = jnp.zeros_like(st_s)

    n1g = n1g_ref[...]
    cw = convw_ref[...]
    row = lax.broadcasted_iota(jnp.int32, (tt, 1), 0)
    mod = mod_ref[0]
    sh1, sc1 = mod[0:1], mod[1:2]

    def norm_mod(x):
        return ((x * _rms(x)) * n1g) * (1.0 + sc1) + sh1

    h = norm_mod(x_ref[0]).astype(BF16)
    qk_ref[0] = _dot(h, win_ref[:, C_QK:C_V])
    v_ref[0] = _dot(h, win_ref[:, C_V:C_G])
    gy_ref[0, :, 0:GLA_WIDTH] = _dot(h, win_ref[:, C_G:C_CB])

    hh = norm_mod(jnp.concatenate([xp_ref[0], xn_ref[0]], axis=0)).astype(BF16)
    h_ext = jnp.concatenate([h, hh], axis=0)
    z_ext = _dot(h_ext, win_ref[:, C_CC:C_CH]) * _dot(h_ext, win_ref[:, C_CH:C_AL])
    z = z_ext[0:tt]
    z_prev = jnp.where(i > 0, z_ext[tt + SUBLANES - 1:tt + SUBLANES, :], 0.0)
    z_next = jnp.where(i < nt - 1, z_ext[tt + SUBLANES:tt + SUBLANES + 1, :], 0.0)
    z_m1 = jnp.where(row == 0, z_prev, pltpu.roll(z, 1, axis=0))
    z_p1 = jnp.where(row == tt - 1, z_next, pltpu.roll(z, tt - 1, axis=0))
    conv = cw[0:1] * z_m1 + cw[1:2] * z + cw[2:3] * z_p1
    yc = _dot(h, win_ref[:, C_CB:C_CC]) * conv
    sq_hi, sq_lo = _split2(yc * yc)
    ss = _dot(sq_hi, mgrp_ref[...]) + _dot(sq_lo, mgrp_ref[...])
    gy_ref[0, :, GLA_WIDTH:] = (yc * lax.rsqrt(ss * (1.0 / CONV_GDIM) + EPS)) * cng_ref[...]

    zd = _dot_f32(_dot(h, win_ref[:, C_AL:C_END]), wdec_ref[...]) + bdec_ref[...]
    la = _log_sigmoid(zd) * (1.0 / GLA_GATE_NORM)
    lab_ref[0] = la[:, GLA_KW:]
    of_ref[0] = _gla_tile(qk_ref[0, :, 0:GLA_KW], qk_ref[0, :, GLA_KW:], v_ref[0], la[:, 0:GLA_KW],
                           st_s, True)


def _mix1(x, mod, n1g, win, wdec, bdec, convw, cng, mgrp, tt):
    bsz, seq, _ = x.shape
    nt = seq // tt
    hb = tt // SUBLANES
    last_hb = seq // SUBLANES - 1
    const = lambda shape: pl.BlockSpec(shape, lambda b, i: tuple(0 for _ in shape))
    tile = lambda w: pl.BlockSpec((1, tt, w), lambda b, i: (b, i, 0))
    out_w = (2 * GLA_KW, GLA_WIDTH, GLA_KW, GLA_WIDTH, GLA_WIDTH + CONV_WIDTH)
    return pl.pallas_call(
        functools.partial(_mix1_kernel, tt=tt, nt=nt),
        out_shape=tuple(jax.ShapeDtypeStruct((bsz, seq, w), F32) for w in out_w),
        grid=(bsz, nt),
        in_specs=[
            tile(D_MODEL),
            pl.BlockSpec((1, SUBLANES, D_MODEL), lambda b, i: (b, jnp.maximum(i * hb - 1, 0), 0)),
            pl.BlockSpec((1, SUBLANES, D_MODEL), lambda b, i: (b, jnp.minimum((i + 1) * hb, last_hb), 0)),
            pl.BlockSpec((1, SUBLANES, D_MODEL), lambda b, i: (b, 0, 0)),
            const((1, D_MODEL)),
            const((D_MODEL, C_END)),
            const((LANES, 2 * GLA_KW)),
            const((1, 2 * GLA_KW)),
            const((3, CONV_WIDTH)),
            const((1, CONV_WIDTH)),
            const((CONV_WIDTH, CONV_WIDTH)),
        ],
        out_specs=tuple(tile(w) for w in out_w),
        scratch_shapes=[pltpu.VMEM((GLA_HEADS, GLA_DV, LANES), F32)],
        compiler_params=pltpu.CompilerParams(dimension_semantics=("arbitrary", "arbitrary"),
                                             vmem_limit_bytes=VMEM_LIMIT),
        name="mix1",
    )(x, x, x, mod, n1g, win, wdec, bdec, convw, cng, mgrp)


def _mix2_kernel(x_ref, mod_ref, qk_ref, v_ref, lab_ref, of_ref, gy_ref, gng_ref, wout_ref, n2g_ref,
                 x1_ref, h2_ref, h8_ref, hs_ref, st_s):
    i = pl.program_id(1)

    @pl.when(i == 0)
    def _():
        st_s[...] = jnp.zeros_like(st_s)

    gng = gng_ref[...]
    mod = mod_ref[0]
    gt1, sh2, sc2 = mod[2:3], mod[3:4], mod[4:5]
    o = of_ref[0] + _gla_tile(qk_ref[0, :, 0:GLA_KW], qk_ref[0, :, GLA_KW:], v_ref[0], lab_ref[0],
                               st_s, False)
    y = None
    for head in range(GLA_HEADS):
        sl = slice(head * GLA_DV, (head + 1) * GLA_DV)
        oh = o[:, sl]
        g = gy_ref[0, :, sl]
        yh = ((oh * _rms(oh)) * gng) * (g * _sigmoid(g))
        part = _dot(yh.astype(BF16), wout_ref[sl, :])
        y = part if y is None else y + part
    y = y + _dot(gy_ref[0, :, GLA_WIDTH:].astype(BF16), wout_ref[GLA_WIDTH:, :])
    x1 = x_ref[0] + gt1 * y
    x1_ref[0] = x1
    h2 = ((x1 * _rms(x1)) * n2g_ref[...]) * (1.0 + sc2) + sh2
    h2_ref[0] = h2.astype(BF16)
    amax = jnp.maximum(jnp.max(jnp.abs(h2), axis=-1, keepdims=True), F8_TINY)
    h8_ref[0] = (h2 * (F8_TOP / amax)).astype(F8)
    hs_ref[0] = jnp.broadcast_to(amax * (1.0 / F8_TOP), (h2.shape[0], LANES))


def _mix2(x, mod, qk, v, lab, of, gy, gng, wout, n2g, tt):
    bsz, seq, _ = x.shape
    nt = seq // tt
    const = lambda shape: pl.BlockSpec(shape, lambda b, i: tuple(0 for _ in shape))
    tile = lambda w: pl.BlockSpec((1, tt, w), lambda b, i: (b, nt - 1 - i, 0))
    return pl.pallas_call(
        _mix2_kernel,
        out_shape=(jax.ShapeDtypeStruct((bsz, seq, D_MODEL), F32), jax.ShapeDtypeStruct((bsz, seq, D_MODEL), BF16),
                   jax.ShapeDtypeStruct((bsz, seq, D_MODEL), F8), jax.ShapeDtypeStruct((bsz, seq, LANES), F32)),
        grid=(bsz, nt),
        in_specs=[
            tile(D_MODEL),
            pl.BlockSpec((1, SUBLANES, D_MODEL), lambda b, i: (b, 0, 0)),
            tile(2 * GLA_KW), tile(GLA_WIDTH), tile(GLA_KW), tile(GLA_WIDTH), tile(GLA_WIDTH + CONV_WIDTH),
            const((1, GLA_DV)),
            const((D_MODEL, D_MODEL)),
            const((1, D_MODEL)),
        ],
        out_specs=(tile(D_MODEL), tile(D_MODEL), tile(D_MODEL), tile(LANES)),
        scratch_shapes=[pltpu.VMEM((GLA_HEADS, GLA_DV, LANES), F32)],
        compiler_params=pltpu.CompilerParams(dimension_semantics=("arbitrary", "arbitrary"),
                                             vmem_limit_bytes=VMEM_LIMIT),
        name="mix2",
    )(x, mod, qk, v, lab, of, gy, gng, wout, n2g)


NEG_INF = float("-inf")
KEY_ROWS = 2 * PEER_NKEYS * PEER_HEADS
VREG = (SUBLANES, LANES)


def _keyproj_kernel(wq_ref, sk_ref, o_ref):
    w_hi, w_lo = _split2(wq_ref[...])
    k_hi, k_lo = _split2(sk_ref[0])
    o_ref[0] = _dot_nt(k_hi, w_hi) + _dot_nt(k_hi, w_lo) + _dot_nt(k_lo, w_hi)


def _keyproj(wq, subkeys):
    n_hp = 2 * PEER_HEADS
    return pl.pallas_call(
        _keyproj_kernel,
        out_shape=jax.ShapeDtypeStruct((n_hp, PEER_NKEYS, D_MODEL), F32),
        grid=(n_hp,),
        in_specs=[pl.BlockSpec((D_MODEL, PEER_DQH), lambda j: (0, j)),
                  pl.BlockSpec((1, PEER_NKEYS, PEER_DQH), lambda j: (j, 0, 0))],
        out_specs=pl.BlockSpec((1, PEER_NKEYS, D_MODEL), lambda j: (j, 0, 0)),
        compiler_params=pltpu.CompilerParams(dimension_semantics=("arbitrary",), vmem_limit_bytes=VMEM_LIMIT),
        name="keyproj",
    )(wq, subkeys)


def _sort16_network():
    pairs = []

    def merge(lo, hi, r):
        step = r * 2
        if step < hi - lo:
            merge(lo, hi, step)
            merge(lo + r, hi, step)
            pairs.extend((i, i + r) for i in range(lo + r, hi - r, step))
        else:
            pairs.append((lo, lo + r))

    def sort(lo, hi):
        if hi - lo >= 1:
            mid = lo + (hi - lo) // 2
            sort(lo, mid)
            sort(mid + 1, hi)
            merge(lo, hi, 1)

    sort(0, PEER_TOPK - 1)
    return tuple(pairs)


SORT16 = _sort16_network()
CAND_LISTS = ([[(0, k2) for k2 in range(PEER_TOPK)]]
              + [[(k1, k2) for k2 in range(PEER_TOPK // (k1 + 1))] for k1 in range(1, 8)]
              + [[(k1, 0) for k1 in range(8, PEER_TOPK)]])
CANDS = [c for lst in CAND_LISTS for c in lst]


def _tree(op, xs):
    xs = list(xs)
    while len(xs) > 1:
        xs = [op(xs[i], xs[i + 1]) for i in range(0, len(xs) - 1, 2)] + ([xs[-1]] if len(xs) % 2 else [])
    return xs[0]


def _ce(v, x, i, j):
    c = v[j] > v[i]
    v[i], v[j] = jnp.maximum(v[i], v[j]), jnp.minimum(v[i], v[j])
    x[i], x[j] = jnp.where(c, x[j], x[i]), jnp.where(c, x[i], x[j])


def _merge_top16(va, xa, vb, xb, sort_result):
    v, x = list(va), list(xa)
    for j in range(PEER_TOPK - len(vb), PEER_TOPK):
        b = PEER_TOPK - 1 - j
        c = vb[b] > va[j]
        v[j] = jnp.maximum(va[j], vb[b])
        x[j] = jnp.where(c, xb[b], xa[j])
    if sort_result:
        for d in (8, 4, 2, 1):
            for i in range(PEER_TOPK):
                if not i & d:
                    _ce(v, x, i, i + d)
    return v, x


def _any(flag):
    return jnp.max(flag) > 0.0


def _top16_keys(key, work_v, work_x, sv_ref, si_ref):
    for g in range(PEER_NKEYS // PEER_TOPK):
        v = [key(PEER_TOPK * g + i) for i in range(PEER_TOPK)]
        x = [float(PEER_TOPK * g + i) for i in range(PEER_TOPK)]
        for i, j in SORT16:
            _ce(v, x, i, j)
        for i in range(PEER_TOPK):
            work_v[PEER_TOPK * g + i] = v[i]
            work_x[PEER_TOPK * g + i] = x[i]
    for span in (1, 2, 4):
        for g in range(0, PEER_NKEYS // PEER_TOPK, 2 * span):
            a, b = PEER_TOPK * g, PEER_TOPK * (g + span)
            v, x = _merge_top16([work_v[a + i] for i in range(PEER_TOPK)], [work_x[a + i] for i in range(PEER_TOPK)],
                                [work_v[b + i] for i in range(PEER_TOPK)], [work_x[b + i] for i in range(PEER_TOPK)],
                                True)
            for i in range(PEER_TOPK):
                if span == 4:
                    sv_ref[i], si_ref[i] = v[i], x[i]
                else:
                    work_v[a + i], work_x[a + i] = v[i], x[i]
    repeat = _tree(jnp.maximum, [jnp.where(v[r] == v[r + 1], 1.0, 0.0) for r in range(PEER_TOPK - 1)])
    n_ge = _tree(jnp.add, [jnp.where(key(k) >= v[PEER_TOPK - 1], 1.0, 0.0) for k in range(PEER_NKEYS)])

    @pl.when(_any(jnp.maximum(repeat, jnp.where(n_ge > float(PEER_TOPK), 1.0, 0.0))))
    def _():
        for k in range(PEER_NKEYS):
            work_v[k] = key(k)

        def extract(r, carry):
            vals = [work_v[k] for k in range(PEER_NKEYS)]
            m = _tree(jnp.maximum, vals)
            idx = _tree(jnp.minimum, [jnp.where(vals[k] == m, float(k), float(PEER_NKEYS)) for k in range(PEER_NKEYS)])
            for k in range(PEER_NKEYS):
                work_v[k] = jnp.where(idx == float(k), NEG_INF, vals[k])
            sv_ref[r] = m
            si_ref[r] = idx
            return carry

        lax.fori_loop(0, PEER_TOPK, extract, 0)


def _top16_sums(sv_s, si_s, work_v, work_x, rv_s, re_s):
    sv0 = [sv_s[0, r] for r in range(PEER_TOPK)]
    sv1 = [sv_s[1, r] for r in range(PEER_TOPK)]
    e_hi = [si_s[0, r] * float(PEER_NKEYS) for r in range(PEER_TOPK)]
    si1 = [si_s[1, r] for r in range(PEER_TOPK)]
    val = {c: sv0[c[0]] + sv1[c[1]] for c in CANDS}
    eid = {c: e_hi[c[0]] + si1[c[1]] for c in CANDS}
    v, x = [val[c] for c in CAND_LISTS[0]], [eid[c] for c in CAND_LISTS[0]]
    for n, lst in enumerate(CAND_LISTS[1:]):
        v, x = _merge_top16(v, x, [val[c] for c in lst], [eid[c] for c in lst], n < len(CAND_LISTS) - 2)
    for r in range(PEER_TOPK):
        rv_s[r], re_s[r] = v[r], x[r]
    low = _tree(jnp.minimum, v)
    n_ge = _tree(jnp.add, [jnp.where(val[c] >= low, 1.0, 0.0) for c in CANDS])

    @pl.when(_any(jnp.where(n_ge > float(PEER_TOPK), 1.0, 0.0)))
    def _():
        for n, c in enumerate(CANDS):
            work_v[n], work_x[n] = val[c], eid[c]

        def extract(r, carry):
            vals = [work_v[n] for n in range(len(CANDS))]
            m = _tree(jnp.maximum, vals)
            code = [float(c[0] * PEER_TOPK + c[1]) for c in CANDS]
            sel = _tree(jnp.minimum, [jnp.where(vals[n] == m, code[n], 1e9) for n in range(len(CANDS))])
            hit = [sel == code[n] for n in range(len(CANDS))]
            rv_s[r] = m
            re_s[r] = _tree(jnp.add, [jnp.where(hit[n], work_x[n], 0.0) for n in range(len(CANDS))])
            for n in range(len(CANDS)):
                work_v[n] = jnp.where(hit[n], NEG_INF, vals[n])
            return carry

        lax.fori_loop(0, PEER_TOPK, extract, 0)


def _route_kernel(h2_ref, wpt_ref, a_ref, b_ref, g_ref, sc_s, work_v, work_x, sv_s, si_s, rv_s, re_s,
                  pa_s, pb_s, pg_s, *, tt):
    sc_s[...] = _dot_nt(wpt_ref[...], h2_ref[0])

    def lane_tile(c, carry):
        lanes = pl.ds(pl.multiple_of(c * LANES, LANES), LANES)
        for p in range(2):
            def key(k, p=p):
                r0 = (p * PEER_NKEYS + k) * SUBLANES
                return sc_s[r0:r0 + SUBLANES, lanes]
            _top16_keys(key, work_v, work_x, sv_s.at[p], si_s.at[p])
        _top16_sums(sv_s, si_s, work_v, work_x, rv_s, re_s)
        v = [rv_s[r] for r in range(PEER_TOPK)]
        m = _tree(jnp.maximum, v)
        ex = [jnp.exp(vr - m) for vr in v]
        inv = 1.0 / _tree(jnp.add, ex)
        for r in range(PEER_TOPK):
            e = re_s[r]
            i1 = jnp.floor(e * (1.0 / PEER_NKEYS))
            rows = slice(r * SUBLANES, (r + 1) * SUBLANES)
            pa_s[rows, :] = i1
            pb_s[rows, :] = e - i1 * float(PEER_NKEYS)
            pg_s[rows, :] = ex[r] * inv
        toks = pl.ds(pl.multiple_of(c * LANES, LANES), LANES)
        a_ref[0, toks, :] = jnp.transpose(pa_s[...])
        b_ref[0, toks, :] = jnp.transpose(pb_s[...])
        g_ref[0, toks, :] = jnp.transpose(pg_s[...])
        return carry

    lax.fori_loop(0, tt // LANES, lane_tile, 0)


def _route(h2, wpt, tt):
    bsz, seq, _ = h2.shape
    n_pair = PEER_HEADS * PEER_TOPK
    tile = lambda w: pl.BlockSpec((1, tt, w), lambda b, i: (b, i, 0))
    vregs = lambda *lead: pltpu.VMEM(lead + VREG, F32)
    return pl.pallas_call(
        functools.partial(_route_kernel, tt=tt),
        out_shape=tuple(jax.ShapeDtypeStruct((bsz, seq, n_pair), F32) for _ in range(3)),
        grid=(bsz, seq // tt),
        in_specs=[tile(D_MODEL), pl.BlockSpec((KEY_ROWS, D_MODEL), lambda b, i: (0, 0))],
        out_specs=(tile(n_pair), tile(n_pair), tile(n_pair)),
        scratch_shapes=[pltpu.VMEM((KEY_ROWS, tt), F32), vregs(PEER_NKEYS), vregs(PEER_NKEYS),
                        vregs(2, PEER_TOPK), vregs(2, PEER_TOPK), vregs(PEER_TOPK), vregs(PEER_TOPK),
                        pltpu.VMEM((n_pair, LANES), F32), pltpu.VMEM((n_pair, LANES), F32),
                        pltpu.VMEM((n_pair, LANES), F32)],
        compiler_params=pltpu.CompilerParams(dimension_semantics=("arbitrary", "arbitrary"),
                                             vmem_limit_bytes=VMEM_LIMIT),
        name="route",
    )(h2, wpt)


def _peer_kernel(h8_ref, hs_ref, x1_ref, mod_ref, a_ref, b_ref, g_ref, ut_ref, us_ref, v_ref, vs_ref, nfg_ref, y_ref,
                 gate_s, acc_s, *, tt, eb, ne):
    j = pl.program_id(2)
    n_grp = eb // PEER_NKEYS
    steps_per_half = ne // 2

    @pl.when(j == 0)
    def _():
        acc_s[...] = jnp.zeros_like(acc_s)
        sub = lax.broadcasted_iota(jnp.int32, (PEER_NKEYS, PEER_NKEYS), 0).astype(F32).astype(BF16)
        zero = jnp.zeros((PEER_NKEYS, PEER_NKEYS), BF16)
        half = jnp.full((PEER_NKEYS, PEER_NKEYS), 0.5, BF16)

        def onehots(t):
            bc = lambda ref: jnp.broadcast_to(ref[0, pl.ds(t, 1), :], (PEER_NKEYS, PEER_NKEYS)).astype(BF16)
            at = jnp.where(sub == bc(a_ref), half, zero)
            cbt = jnp.where(sub == bc(b_ref), bc(g_ref), zero)
            return at, cbt

        def build(p, carry):
            for u in range(GATE_UNROLL):
                t0 = (p * GATE_UNROLL + u) * 2
                at0, cbt0 = onehots(t0)
                at1, cbt1 = onehots(t0 + 1)
                lhs = jnp.concatenate([at0, at1], axis=1)
                rhs = jnp.concatenate([jnp.concatenate([cbt0.T, zero], axis=1),
                                       jnp.concatenate([zero, cbt1.T], axis=1)], axis=0)
                tiles = _dot(lhs, rhs)
                for k in range(2):
                    lo = tiles[0:GATE_HALF, k * PEER_NKEYS:(k + 1) * PEER_NKEYS]
                    hi = tiles[GATE_HALF:, k * PEER_NKEYS:(k + 1) * PEER_NKEYS]
                    r0 = pl.multiple_of((t0 + k) * GATE_PITCH, SUBLANES)
                    gate_s[pl.ds(r0, GATE_HALF), :] = (_bf16_bits(lo) >> 16) | _bf16_bits(hi)
            return carry

        lax.fori_loop(0, tt // (2 * GATE_UNROLL), build, 0)

    def expert_block(half):
        s = _dot(h8_ref[0], ut_ref[...])
        row_scale = hs_ref[0].astype(BF16)
        parts = []
        for gi in range(n_grp):
            lanes = slice(gi * PEER_NKEYS, (gi + 1) * PEER_NKEYS)
            r = (j - half * steps_per_half) * n_grp + gi
            packed = gate_s[pl.ds(r, tt, stride=GATE_PITCH), :]
            word = (packed & jnp.uint32(0xFFFF0000)) if half else (packed << 16)
            half_gate = lax.bitcast_convert_type(word, F32)
            act = s[:, lanes].astype(BF16) * (row_scale * us_ref[:, lanes].astype(BF16))
            parts.append(_half_gated_gelu(act, half_gate).astype(BF16))
        g = jnp.concatenate(parts, axis=1)
        g_max = jnp.maximum(jnp.max(jnp.abs(g), axis=-1, keepdims=True).astype(F32), F8_TINY)
        g_scale = (F8_TOP / g_max).astype(BF16)
        acc_s[...] += _dot((g * g_scale).astype(F8), v_ref[...]) * (1.0 / g_scale.astype(F32))

    @pl.when(j < steps_per_half)
    def _():
        expert_block(0)

    @pl.when(j >= steps_per_half)
    def _():
        expert_block(1)

    @pl.when(j == ne - 1)
    def _():
        gt2 = mod_ref[0][5:6]
        x2 = x1_ref[0] + gt2 * (acc_s[...] * vs_ref[...])
        y_ref[0] = (x2 * _rms(x2)) * nfg_ref[...]


def _peer(h8, hs, x1, mod, a_idx, b_idx, gates, ut8, us, v8, vs, nfg, tt, eb):
    bsz, seq, _ = h8.shape
    n_exp = v8.shape[0]
    ne = n_exp // eb
    assert ne % 2 == 0 and (GATE_HALF * PEER_NKEYS) % eb == 0 and tt % (2 * GATE_UNROLL) == 0
    n_pair = PEER_HEADS * PEER_TOPK
    tile = lambda w: pl.BlockSpec((1, tt, w), lambda b, i, j: (b, i, 0))
    return pl.pallas_call(
        functools.partial(_peer_kernel, tt=tt, eb=eb, ne=ne),
        out_shape=jax.ShapeDtypeStruct((bsz, seq, D_MODEL), F32),
        grid=(bsz, seq // tt, ne),
        in_specs=[
            tile(D_MODEL), tile(LANES), tile(D_MODEL),
            pl.BlockSpec((1, SUBLANES, D_MODEL), lambda b, i, j: (b, 0, 0)),
            tile(n_pair), tile(n_pair), tile(n_pair),
            pl.BlockSpec((D_MODEL, eb), lambda b, i, j: (0, j)),
            pl.BlockSpec((1, eb), lambda b, i, j: (0, j)),
            pl.BlockSpec((eb, D_MODEL), lambda b, i, j: (j, 0)),
            pl.BlockSpec((1, D_MODEL), lambda b, i, j: (0, 0)),
            pl.BlockSpec((1, D_MODEL), lambda b, i, j: (0, 0)),
        ],
        out_specs=tile(D_MODEL),
        scratch_shapes=[pltpu.VMEM((tt * GATE_PITCH, PEER_NKEYS), jnp.uint32), pltpu.VMEM((tt, D_MODEL), F32)],
        compiler_params=pltpu.CompilerParams(dimension_semantics=("arbitrary", "arbitrary", "arbitrary"),
                                             vmem_limit_bytes=VMEM_LIMIT),
        name="peer",
    )(h8, hs, x1, mod, a_idx, b_idx, gates, ut8, us, v8, vs, nfg)


def _uquant_kernel(ut_ref, u8_ref, us_ref):
    ut = ut_ref[...]
    amax = jnp.maximum(jnp.max(jnp.abs(ut), axis=0, keepdims=True), F8_TINY)
    u8_ref[...] = (ut * (F8_TOP / amax)).astype(F8)
    us_ref[...] = amax * (1.0 / F8_TOP)


def _uquant(ut, eb):
    d, n_exp = ut.shape
    return pl.pallas_call(
        _uquant_kernel,
        out_shape=(jax.ShapeDtypeStruct((d, n_exp), F8), jax.ShapeDtypeStruct((1, n_exp), F32)),
        grid=(n_exp // eb,),
        in_specs=[pl.BlockSpec((d, eb), lambda j: (0, j))],
        out_specs=(pl.BlockSpec((d, eb), lambda j: (0, j)), pl.BlockSpec((1, eb), lambda j: (0, j))),
        compiler_params=pltpu.CompilerParams(dimension_semantics=("arbitrary",), vmem_limit_bytes=VMEM_LIMIT),
        name="uquant",
    )(ut)


def _vmax_kernel(v_ref, m_ref):
    @pl.when(pl.program_id(0) == 0)
    def _():
        m_ref[...] = jnp.zeros_like(m_ref)

    m_ref[...] = jnp.maximum(m_ref[...], jnp.max(jnp.abs(v_ref[...]), axis=0, keepdims=True))


def _vcast_kernel(v_ref, m_ref, v8_ref):
    v8_ref[...] = (v_ref[...] * (F8_TOP / jnp.maximum(m_ref[...], F8_TINY))).astype(F8)


def _vquant(v, eb):
    n_exp, d = v.shape
    params = pltpu.CompilerParams(dimension_semantics=("arbitrary",), vmem_limit_bytes=VMEM_LIMIT)
    blk = pl.BlockSpec((eb, d), lambda j: (j, 0))
    row = pl.BlockSpec((1, d), lambda j: (0, 0))
    vmax = pl.pallas_call(_vmax_kernel, out_shape=jax.ShapeDtypeStruct((1, d), F32), grid=(n_exp // eb,),
                          in_specs=[blk], out_specs=row, compiler_params=params, name="vmax")(v)
    v8 = pl.pallas_call(_vcast_kernel, out_shape=jax.ShapeDtypeStruct((n_exp, d), F8), grid=(n_exp // eb,),
                        in_specs=[blk, row], out_specs=blk, compiler_params=params, name="vcast")(v, vmax)
    return v8, jnp.maximum(vmax, F8_TINY) * (1.0 / F8_TOP)


def _prep_weights(norm1_g, norm2_g, w_in, w_dec_f, b_dec_f, w_dec_b, b_dec_b, gla_norm_g, conv_w, conv_norm_g,
                  w_out, peer_wq, peer_subkeys, peer_u, peer_v, normf_g):
    w = w_in[0]
    q, k, v, g, alf, alb, cb, cc, ch = jnp.split(w, (256, 512, 1024, 1536, 1552, 1568, 2080, 2592), axis=-1)
    pad = jnp.zeros((D_MODEL, C_END - C_AL - 2 * GLA_RANK), F32)
    win = jnp.concatenate([q, k, v, g, cb, cc, ch, alf, alb, pad], axis=-1).astype(BF16)
    wdec = jnp.zeros((LANES, 2 * GLA_KW), F32)
    wdec = wdec.at[0:GLA_RANK, 0:GLA_KW].set(w_dec_f[0]).at[GLA_RANK:2 * GLA_RANK, GLA_KW:].set(w_dec_b[0])
    bdec = jnp.concatenate([b_dec_f[0], b_dec_b[0]])[None, :]
    grp = jnp.arange(CONV_WIDTH) // CONV_GDIM
    mgrp = (grp[:, None] == grp[None, :]).astype(BF16)
    wp = _keyproj(peer_wq[0], peer_subkeys[0].reshape(2 * PEER_HEADS, PEER_NKEYS, PEER_DQH))
    wpt = wp.reshape(PEER_HEADS, 2, PEER_NKEYS, D_MODEL).transpose(1, 2, 0, 3).reshape(KEY_ROWS, D_MODEL)
    ut8, us = _uquant(jnp.transpose(peer_u[0]), EXPERT_BLOCK)
    v8, vs = _vquant(peer_v[0], EXPERT_BLOCK)
    return dict(
        n1g=norm1_g[0][None, :], n2g=norm2_g[0][None, :], win=win, wdec=wdec, bdec=bdec,
        convw=conv_w[0], cng=conv_norm_g[0][None, :], mgrp=mgrp, gng=gla_norm_g[0][None, :],
        wout=w_out[0].astype(BF16), wpt=wpt.astype(BF16),
        ut8=ut8, us=us, v8=v8, vs=vs, nfg=normf_g[None, :],
    )


def _trunk(x, mod, w, tt, rtt, ptt, eb):
    qk, v, lab, of, gy = _mix1(x, mod, w["n1g"], w["win"], w["wdec"], w["bdec"], w["convw"], w["cng"], w["mgrp"], tt)
    x1, h2, h8, hs = _mix2(x, mod, qk, v, lab, of, gy, w["gng"], w["wout"], w["n2g"], tt)
    a_idx, b_idx, gates = _route(h2, w["wpt"], rtt)
    return _peer(h8, hs, x1, mod, a_idx, b_idx, gates, w["ut8"], w["us"], w["v8"], w["vs"], w["nfg"], ptt, eb)


def kernel(x_prompt, x_sample, c_prompt, c_sample, norm1_g, norm2_g, w_ada, b_ada, w_in, w_dec_f, b_dec_f,
           w_dec_b, b_dec_b, gla_norm_g, conv_w, conv_norm_g, w_out, peer_wq, peer_subkeys, peer_u, peer_v,
           normf_g):
    w = _prep_weights(norm1_g, norm2_g, w_in, w_dec_f, b_dec_f, w_dec_b, b_dec_b, gla_norm_g, conv_w,
                      conv_norm_g, w_out, peer_wq, peer_subkeys, peer_u, peer_v, normf_g)
    nb_p, nb_s = c_prompt.shape[0], c_sample.shape[0]
    c_all = jnp.concatenate([c_prompt, c_sample, jnp.zeros((SUBLANES - nb_p - nb_s, D_MODEL), F32)], axis=0)
    ada = _ada(c_all, w_ada[0], b_ada[0][None, :])
    mod = jnp.pad(ada.reshape(SUBLANES, N_ADA, D_MODEL), ((0, 0), (0, SUBLANES - N_ADA), (0, 0)))
    y_prompt = _trunk(x_prompt, mod[:nb_p], w, TOKEN_TILE, ROUTE_TILE, PEER_TOKEN_TILE, EXPERT_BLOCK)
    y_sample = _trunk(x_sample, mod[nb_p:nb_p + nb_s], w, TOKEN_TILE, ROUTE_TILE, PEER_TOKEN_TILE, EXPERT_BLOCK)
    return (y_prompt, y_sample)
```

```python
import functools

import jax
import jax.numpy as jnp
from jax import lax
from jax.experimental import pallas as pl
from jax.experimental.pallas import tpu as pltpu

F32 = jnp.float32
BF16 = jnp.bfloat16
F8 = jnp.float8_e4m3fn
F8_TOP = 224.0
F8_TINY = 1e-30

D_MODEL = 1024
GLA_HEADS = 4
GLA_DV = 128
GLA_DK = 64
GLA_KW = GLA_HEADS * GLA_DK
GLA_WIDTH = GLA_HEADS * GLA_DV
GLA_RANK = 16
GLA_GATE_NORM = 16.0
GLA_CHUNK = 64
CONV_WIDTH = 512
CONV_GDIM = 64
PEER_HEADS = 8
PEER_NKEYS = 128
PEER_DQH = 128
PEER_TOPK = 16
N_ADA = 6
EPS = 1e-6

LANES = 128
SUBLANES = 8
VMEM_LIMIT = 56 * 1024 * 1024

C_QK, C_V, C_G, C_CB, C_CC, C_CH, C_AL, C_END = 0, 512, 1024, 1536, 2048, 2560, 3072, 3200

TOKEN_TILE = 256
ROUTE_TILE = 512
PEER_TOKEN_TILE = 512
EXPERT_BLOCK = 4096
GATE_HALF = PEER_NKEYS // 2
GATE_PITCH = GATE_HALF + SUBLANES
GATE_UNROLL = 64


def _dot(a, b):
    return jnp.dot(a, b, preferred_element_type=F32)


def _dot_nt(a, b):
    return lax.dot_general(a, b, (((1,), (1,)), ((), ())), preferred_element_type=F32)


def _dot_tn(a, b):
    return lax.dot_general(a, b, (((0,), (0,)), ((), ())), preferred_element_type=F32)


def _split2(x):
    hi = x.astype(BF16)
    lo = (x - hi.astype(F32)).astype(BF16)
    return hi, lo


def _split3(x):
    hi = x.astype(BF16)
    r = x - hi.astype(F32)
    mid = r.astype(BF16)
    lo = (r - mid.astype(F32)).astype(BF16)
    return hi, mid, lo


def _dot_f32(a, b):
    a_hi, a_lo = _split2(a)
    b_hi, b_lo = _split2(b)
    return _dot(a_hi, b_hi) + _dot(a_hi, b_lo) + _dot(a_lo, b_hi)


def _bf16_bits(x):
    return lax.bitcast_convert_type(x.astype(BF16).astype(F32), jnp.uint32)


def _rms(x):
    return lax.rsqrt(jnp.mean(x * x, axis=-1, keepdims=True) + EPS)


def _sigmoid(x):
    return 1.0 / (1.0 + jnp.exp(-x))


def _log_sigmoid(x):
    return jnp.minimum(x, 0.0) - jnp.log1p(jnp.exp(-jnp.abs(x)))


GELU_C1 = 0.7978845608028654
GELU_C3 = GELU_C1 * 0.044715


def _half_gated_gelu(x, half_gate):
    x = x.astype(BF16)
    gx = half_gate.astype(BF16) * x
    return gx * jnp.tanh(x * (GELU_C1 + GELU_C3 * (x * x))) + gx


def _ada_kernel(c_ref, w_ref, b_ref, o_ref):
    c = c_ref[...]
    o_ref[...] = _dot_f32(c * _sigmoid(c), w_ref[...]) + b_ref[...]


def _ada(c_pad, w_ada, b_ada):
    n_col = N_ADA * D_MODEL
    blk = 1536
    return pl.pallas_call(
        _ada_kernel,
        out_shape=jax.ShapeDtypeStruct((SUBLANES, n_col), F32),
        grid=(n_col // blk,),
        in_specs=[
            pl.BlockSpec((SUBLANES, D_MODEL), lambda j: (0, 0)),
            pl.BlockSpec((D_MODEL, blk), lambda j: (0, j)),
            pl.BlockSpec((1, blk), lambda j: (0, j)),
        ],
        out_specs=pl.BlockSpec((SUBLANES, blk), lambda j: (0, j)),
        compiler_params=pltpu.CompilerParams(dimension_semantics=("arbitrary",), vmem_limit_bytes=VMEM_LIMIT),
        name="ada",
    )(c_pad, w_ada, b_ada)


def _gla_tile(q, k, v, la, st_ref, forward):
    tt = q.shape[0]
    n_chunk = tt // GLA_CHUNK
    row = lax.broadcasted_iota(jnp.int32, (tt, tt), 0)
    col = lax.broadcasted_iota(jnp.int32, (tt, tt), 1)
    ordered = (row >= col) if forward else (row <= col)
    causal = ordered & (row // GLA_CHUNK == col // GLA_CHUNK)
    tri = jnp.where(causal, 1.0, 0.0).astype(BF16)
    la_hi, la_mid, la_lo = _split3(la)
    b = _dot(tri, la_hi) + _dot(tri, la_mid) + _dot(tri, la_lo)
    end_rows = [ci * GLA_CHUNK + (GLA_CHUNK - 1 if forward else 0) for ci in range(n_chunk)]
    b_end = jnp.concatenate([jnp.broadcast_to(b[r:r + 1, :], (GLA_CHUNK, GLA_KW)) for r in end_rows], axis=0)
    q_dec = q * (jnp.exp(b) * (GLA_DK ** -0.5))
    k_dec = k * jnp.exp(-b)
    k_end = k * jnp.exp(b_end - b)
    decay = [jnp.exp(b[r:r + 1, :]) for r in end_rows]
    lane = lax.broadcasted_iota(jnp.int32, (1, LANES), 1)
    scan = range(n_chunk) if forward else range(n_chunk - 1, -1, -1)
    outs = []
    for head in range(GLA_HEADS):
        pair, half = head // 2, head % 2
        sl = slice(pair * LANES, (pair + 1) * LANES)
        in_head = (lane // GLA_DK) == half
        qd = jnp.where(in_head, q_dec[:, sl], 0.0).astype(BF16)
        kd = k_dec[:, sl].astype(BF16)
        ke = k_end[:, sl].astype(BF16)
        v_h = v[:, head * GLA_DV:(head + 1) * GLA_DV].astype(BF16)
        scores = jnp.where(causal, _dot_nt(qd, kd), 0.0)
        o_intra = _dot(scores.astype(BF16), v_h)
        st = st_ref[head]
        parts = [None] * n_chunk
        for ci in scan:
            rows = slice(ci * GLA_CHUNK, (ci + 1) * GLA_CHUNK)
            parts[ci] = o_intra[rows] + _dot_nt(qd[rows], st.astype(BF16))
            st = st * decay[ci][:, sl] + _dot_tn(v_h[rows], ke[rows])
        st_ref[head] = st
        outs.append(jnp.concatenate(parts, axis=0))
    return jnp.concatenate(outs, axis=-1)


def _mix1_kernel(x_ref, xp_ref, xn_ref, mod_ref, n1g_ref, win_ref, wdec_ref, bdec_ref, convw_ref, cng_ref,
                 mgrp_ref, qk_ref, v_ref, lab_ref, of_ref, gy_ref, st_s, *, tt, nt):
    i = pl.program_id(1)

    @pl.when(i == 0)
    def _():
        st_s[...] = jnp.zeros_like(st_s)

    n1g = n1g_ref[...]
    cw = convw_ref[...]
    row = lax.broadcasted_iota(jnp.int32, (tt, 1), 0)
    mod = mod_ref[0]
    sh1, sc1 = mod[0:1], mod[1:2]

    def norm_mod(x):
        return ((x * _rms(x)) * n1g) * (1.0 + sc1) + sh1

    h = norm_mod(x_ref[0]).astype(BF16)
    qk_ref[0] = _dot(h, win_ref[:, C_QK:C_V])
    v_ref[0] = _dot(h, win_ref[:, C_V:C_G])
    gy_ref[0, :, 0:GLA_WIDTH] = _dot(h, win_ref[:, C_G:C_CB])

    hh = norm_mod(jnp.concatenate([xp_ref[0], xn_ref[0]], axis=0)).astype(BF16)
    h_ext = jnp.concatenate([h, hh], axis=0)
    z_ext = _dot(h_ext, win_ref[:, C_CC:C_CH]) * _dot(h_ext, win_ref[:, C_CH:C_AL])
    z = z_ext[0:tt]
    z_prev = jnp.where(i > 0, z_ext[tt + SUBLANES - 1:tt + SUBLANES, :], 0.0)
    z_next = jnp.where(i < nt - 1, z_ext[tt + SUBLANES:tt + SUBLANES + 1, :], 0.0)
    z_m1 = jnp.where(row == 0, z_prev, pltpu.roll(z, 1, axis=0))
    z_p1 = jnp.where(row == tt - 1, z_next, pltpu.roll(z, tt - 1, axis=0))
    conv = cw[0:1] * z_m1 + cw[1:2] * z + cw[2:3] * z_p1
    yc = _dot(h, win_ref[:, C_CB:C_CC]) * conv
    sq_hi, sq_lo = _split2(yc * yc)
    ss = _dot(sq_hi, mgrp_ref[...]) + _dot(sq_lo, mgrp_ref[...])
    gy_ref[0, :, GLA_WIDTH:] = (yc * lax.rsqrt(ss * (1.0 / CONV_GDIM) + EPS)) * cng_ref[...]

    zd = _dot_f32(_dot(h, win_ref[:, C_AL:C_END]), wdec_ref[...]) + bdec_ref[...]
    la = _log_sigmoid(zd) * (1.0 / GLA_GATE_NORM)
    lab_ref[0] = la[:, GLA_KW:]
    of_ref[0] = _gla_tile(qk_ref[0, :, 0:GLA_KW], qk_ref[0, :, GLA_KW:], v_ref[0], la[:, 0:GLA_KW],
                           st_s, True)


def _mix1(x, mod, n1g, win, wdec, bdec, convw, cng, mgrp, tt):
    bsz, seq, _ = x.shape
    nt = seq // tt
    hb = tt // SUBLANES
    last_hb = seq // SUBLANES - 1
    const = lambda shape: pl.BlockSpec(shape, lambda b, i: tuple(0 for _ in shape))
    tile = lambda w: pl.BlockSpec((1, tt, w), lambda b, i: (b, i, 0))
    out_w = (2 * GLA_KW, GLA_WIDTH, GLA_KW, GLA_WIDTH, GLA_WIDTH + CONV_WIDTH)
    return pl.pallas_call(
        functools.partial(_mix1_kernel, tt=tt, nt=nt),
        out_shape=tuple(jax.ShapeDtypeStruct((bsz, seq, w), F32) for w in out_w),
        grid=(bsz, nt),
        in_specs=[
            tile(D_MODEL),
            pl.BlockSpec((1, SUBLANES, D_MODEL), lambda b, i: (b, jnp.maximum(i * hb - 1, 0), 0)),
            pl.BlockSpec((1, SUBLANES, D_MODEL), lambda b, i: (b, jnp.minimum((i + 1) * hb, last_hb), 0)),
            pl.BlockSpec((1, SUBLANES, D_MODEL), lambda b, i: (b, 0, 0)),
            const((1, D_MODEL)),
            const((D_MODEL, C_END)),
            const((LANES, 2 * GLA_KW)),
            const((1, 2 * GLA_KW)),
            const((3, CONV_WIDTH)),
            const((1, CONV_WIDTH)),
            const((CONV_WIDTH, CONV_WIDTH)),
        ],
        out_specs=tuple(tile(w) for w in out_w),
        scratch_shapes=[pltpu.VMEM((GLA_HEADS, GLA_DV, LANES), F32)],
        compiler_params=pltpu.CompilerParams(dimension_semantics=("arbitrary", "arbitrary"),
                                             vmem_limit_bytes=VMEM_LIMIT),
        name="mix1",
    )(x, x, x, mod, n1g, win, wdec, bdec, convw, cng, mgrp)


def _mix2_kernel(x_ref, mod_ref, qk_ref, v_ref, lab_ref, of_ref, gy_ref, gng_ref, wout_ref, n2g_ref,
                 x1_ref, h2_ref, h8_ref, hs_ref, st_s):
    i = pl.program_id(1)

    @pl.when(i == 0)
    def _():
        st_s[...] = jnp.zeros_like(st_s)

    gng = gng_ref[...]
    mod = mod_ref[0]
    gt1, sh2, sc2 = mod[2:3], mod[3:4], mod[4:5]
    o = of_ref[0] + _gla_tile(qk_ref[0, :, 0:GLA_KW], qk_ref[0, :, GLA_KW:], v_ref[0], lab_ref[0],
                               st_s, False)
    y = None
    for head in range(GLA_HEADS):
        sl = slice(head * GLA_DV, (head + 1) * GLA_DV)
        oh = o[:, sl]
        g = gy_ref[0, :, sl]
        yh = ((oh * _rms(oh)) * gng) * (g * _sigmoid(g))
        part = _dot(yh.astype(BF16), wout_ref[sl, :])
        y = part if y is None else y + part
    y = y + _dot(gy_ref[0, :, GLA_WIDTH:].astype(BF16), wout_ref[GLA_WIDTH:, :])
    x1 = x_ref[0] + gt1 * y
    x1_ref[0] = x1
    h2 = ((x1 * _rms(x1)) * n2g_ref[...]) * (1.0 + sc2) + sh2
    h2_ref[0] = h2.astype(BF16)
    amax = jnp.maximum(jnp.max(jnp.abs(h2), axis=-1, keepdims=True), F8_TINY)
    h8_ref[0] = (h2 * (F8_TOP / amax)).astype(F8)
    hs_ref[0] = jnp.broadcast_to(amax * (1.0 / F8_TOP), (h2.shape[0], LANES))


def _mix2(x, mod, qk, v, lab, of, gy, gng, wout, n2g, tt):
    bsz, seq, _ = x.shape
    nt = seq // tt
    const = lambda shape: pl.BlockSpec(shape, lambda b, i: tuple(0 for _ in shape))
    tile = lambda w: pl.BlockSpec((1, tt, w), lambda b, i: (b, nt - 1 - i, 0))
    return pl.pallas_call(
        _mix2_kernel,
        out_shape=(jax.ShapeDtypeStruct((bsz, seq, D_MODEL), F32), jax.ShapeDtypeStruct((bsz, seq, D_MODEL), BF16),
                   jax.ShapeDtypeStruct((bsz, seq, D_MODEL), F8), jax.ShapeDtypeStruct((bsz, seq, LANES), F32)),
        grid=(bsz, nt),
        in_specs=[
            tile(D_MODEL),
            pl.BlockSpec((1, SUBLANES, D_MODEL), lambda b, i: (b, 0, 0)),
            tile(2 * GLA_KW), tile(GLA_WIDTH), tile(GLA_KW), tile(GLA_WIDTH), tile(GLA_WIDTH + CONV_WIDTH),
            const((1, GLA_DV)),
            const((D_MODEL, D_MODEL)),
            const((1, D_MODEL)),
        ],
        out_specs=(tile(D_MODEL), tile(D_MODEL), tile(D_MODEL), tile(LANES)),
        scratch_shapes=[pltpu.VMEM((GLA_HEADS, GLA_DV, LANES), F32)],
        compiler_params=pltpu.CompilerParams(dimension_semantics=("arbitrary", "arbitrary"),
                                             vmem_limit_bytes=VMEM_LIMIT),
        name="mix2",
    )(x, mod, qk, v, lab, of, gy, gng, wout, n2g)


NEG_INF = float("-inf")
KEY_ROWS = 2 * PEER_NKEYS * PEER_HEADS
VREG = (SUBLANES, LANES)


def _keyproj_kernel(wq_ref, sk_ref, o_ref):
    w_hi, w_lo = _split2(wq_ref[...])
    k_hi, k_lo = _split2(sk_ref[0])
    o_ref[0] = _dot_nt(k_hi, w_hi) + _dot_nt(k_hi, w_lo) + _dot_nt(k_lo, w_hi)


def _keyproj(wq, subkeys):
    n_hp = 2 * PEER_HEADS
    return pl.pallas_call(
        _keyproj_kernel,
        out_shape=jax.ShapeDtypeStruct((n_hp, PEER_NKEYS, D_MODEL), F32),
        grid=(n_hp,),
        in_specs=[pl.BlockSpec((D_MODEL, PEER_DQH), lambda j: (0, j)),
                  pl.BlockSpec((1, PEER_NKEYS, PEER_DQH), lambda j: (j, 0, 0))],
        out_specs=pl.BlockSpec((1, PEER_NKEYS, D_MODEL), lambda j: (j, 0, 0)),
        compiler_params=pltpu.CompilerParams(dimension_semantics=("arbitrary",), vmem_limit_bytes=VMEM_LIMIT),
        name="keyproj",
    )(wq, subkeys)


def _sort16_network():
    pairs = []

    def merge(lo, hi, r):
        step = r * 2
        if step < hi - lo:
            merge(lo, hi, step)
            merge(lo + r, hi, step)
            pairs.extend((i, i + r) for i in range(lo + r, hi - r, step))
        else:
            pairs.append((lo, lo + r))

    def sort(lo, hi):
        if hi - lo >= 1:
            mid = lo + (hi - lo) // 2
            sort(lo, mid)
            sort(mid + 1, hi)
            merge(lo, hi, 1)

    sort(0, PEER_TOPK - 1)
    return tuple(pairs)


SORT16 = _sort16_network()
CAND_LISTS = ([[(0, k2) for k2 in range(PEER_TOPK)]]
              + [[(k1, k2) for k2 in range(PEER_TOPK // (k1 + 1))] for k1 in range(1, 8)]
              + [[(k1, 0) for k1 in range(8, PEER_TOPK)]])
CANDS = [c for lst in CAND_LISTS for c in lst]


def _tree(op, xs):
    xs = list(xs)
    while len(xs) > 1:
        xs = [op(xs[i], xs[i + 1]) for i in range(0, len(xs) - 1, 2)] + ([xs[-1]] if len(xs) % 2 else [])
    return xs[0]


def _ce(v, x, i, j):
    c = v[j] > v[i]
    v[i], v[j] = jnp.maximum(v[i], v[j]), jnp.minimum(v[i], v[j])
    x[i], x[j] = jnp.where(c, x[j], x[i]), jnp.where(c, x[i], x[j])


def _merge_top16(va, xa, vb, xb, sort_result):
    v, x = list(va), list(xa)
    for j in range(PEER_TOPK - len(vb), PEER_TOPK):
        b = PEER_TOPK - 1 - j
        c = vb[b] > va[j]
        v[j] = jnp.maximum(va[j], vb[b])
        x[j] = jnp.where(c, xb[b], xa[j])
    if sort_result:
        for d in (8, 4, 2, 1):
            for i in range(PEER_TOPK):
                if not i & d:
                    _ce(v, x, i, i + d)
    return v, x


def _any(flag):
    return jnp.max(flag) > 0.0


def _top16_keys(key, work_v, work_x, sv_ref, si_ref):
    for g in range(PEER_NKEYS // PEER_TOPK):
        v = [key(PEER_TOPK * g + i) for i in range(PEER_TOPK)]
        x = [float(PEER_TOPK * g + i) for i in range(PEER_TOPK)]
        for i, j in SORT16:
            _ce(v, x, i, j)
        for i in range(PEER_TOPK):
            work_v[PEER_TOPK * g + i] = v[i]
            work_x[PEER_TOPK * g + i] = x[i]
    for span in (1, 2, 4):
        for g in range(0, PEER_NKEYS // PEER_TOPK, 2 * span):
            a, b = PEER_TOPK * g, PEER_TOPK * (g + span)
            v, x = _merge_top16([work_v[a + i] for i in range(PEER_TOPK)], [work_x[a + i] for i in range(PEER_TOPK)],
                                [work_v[b + i] for i in range(PEER_TOPK)], [work_x[b + i] for i in range(PEER_TOPK)],
                                True)
            for i in range(PEER_TOPK):
                if span == 4:
                    sv_ref[i], si_ref[i] = v[i], x[i]
                else:
                    work_v[a + i], work_x[a + i] = v[i], x[i]
    repeat = _tree(jnp.maximum, [jnp.where(v[r] == v[r + 1], 1.0, 0.0) for r in range(PEER_TOPK - 1)])
    n_ge = _tree(jnp.add, [jnp.where(key(k) >= v[PEER_TOPK - 1], 1.0, 0.0) for k in range(PEER_NKEYS)])

    @pl.when(_any(jnp.maximum(repeat, jnp.where(n_ge > float(PEER_TOPK), 1.0, 0.0))))
    def _():
        for k in range(PEER_NKEYS):
            work_v[k] = key(k)

        def extract(r, carry):
            vals = [work_v[k] for k in range(PEER_NKEYS)]
            m = _tree(jnp.maximum, vals)
            idx = _tree(jnp.minimum, [jnp.where(vals[k] == m, float(k), float(PEER_NKEYS)) for k in range(PEER_NKEYS)])
            for k in range(PEER_NKEYS):
                work_v[k] = jnp.where(idx == float(k), NEG_INF, vals[k])
            sv_ref[r] = m
            si_ref[r] = idx
            return carry

        lax.fori_loop(0, PEER_TOPK, extract, 0)


def _top16_sums(sv_s, si_s, work_v, work_x, rv_s, re_s):
    sv0 = [sv_s[0, r] for r in range(PEER_TOPK)]
    sv1 = [sv_s[1, r] for r in range(PEER_TOPK)]
    e_hi = [si_s[0, r] * float(PEER_NKEYS) for r in range(PEER_TOPK)]
    si1 = [si_s[1, r] for r in range(PEER_TOPK)]
    val = {c: sv0[c[0]] + sv1[c[1]] for c in CANDS}
    eid = {c: e_hi[c[0]] + si1[c[1]] for c in CANDS}
    v, x = [val[c] for c in CAND_LISTS[0]], [eid[c] for c in CAND_LISTS[0]]
    for n, lst in enumerate(CAND_LISTS[1:]):
        v, x = _merge_top16(v, x, [val[c] for c in lst], [eid[c] for c in lst], n < len(CAND_LISTS) - 2)
    for r in range(PEER_TOPK):
        rv_s[r], re_s[r] = v[r], x[r]
    low = _tree(jnp.minimum, v)
    n_ge = _tree(jnp.add, [jnp.where(val[c] >= low, 1.0, 0.0) for c in CANDS])

    @pl.when(_any(jnp.where(n_ge > float(PEER_TOPK), 1.0, 0.0)))
    def _():
        for n, c in enumerate(CANDS):
            work_v[n], work_x[n] = val[c], eid[c]

        def extract(r, carry):
            vals = [work_v[n] for n in range(len(CANDS))]
            m = _tree(jnp.maximum, vals)
            code = [float(c[0] * PEER_TOPK + c[1]) for c in CANDS]
            sel = _tree(jnp.minimum, [jnp.where(vals[n] == m, code[n], 1e9) for n in range(len(CANDS))])
            hit = [sel == code[n] for n in range(len(CANDS))]
            rv_s[r] = m
            re_s[r] = _tree(jnp.add, [jnp.where(hit[n], work_x[n], 0.0) for n in range(len(CANDS))])
            for n in range(len(CANDS)):
                work_v[n] = jnp.where(hit[n], NEG_INF, vals[n])
            return carry

        lax.fori_loop(0, PEER_TOPK, extract, 0)


def _route_kernel(h2_ref, wpt_ref, a_ref, b_ref, g_ref, sc_s, work_v, work_x, sv_s, si_s, rv_s, re_s,
                  pa_s, pb_s, pg_s, *, tt):
    sc_s[...] = _dot_nt(wpt_ref[...], h2_ref[0])

    def lane_tile(c, carry):
        lanes = pl.ds(pl.multiple_of(c * LANES, LANES), LANES)
        for p in range(2):
            def key(k, p=p):
                r0 = (p * PEER_NKEYS + k) * SUBLANES
                return sc_s[r0:r0 + SUBLANES, lanes]
            _top16_keys(key, work_v, work_x, sv_s.at[p], si_s.at[p])
        _top16_sums(sv_s, si_s, work_v, work_x, rv_s, re_s)
        v = [rv_s[r] for r in range(PEER_TOPK)]
        m = _tree(jnp.maximum, v)
        ex = [jnp.exp(vr - m) for vr in v]
        inv = 1.0 / _tree(jnp.add, ex)
        for r in range(PEER_TOPK):
            e = re_s[r]
            i1 = jnp.floor(e * (1.0 / PEER_NKEYS))
            rows = slice(r * SUBLANES, (r + 1) * SUBLANES)
            pa_s[rows, :] = i1
            pb_s[rows, :] = e - i1 * float(PEER_NKEYS)
            pg_s[rows, :] = ex[r] * inv
        toks = pl.ds(pl.multiple_of(c * LANES, LANES), LANES)
        a_ref[0, toks, :] = jnp.transpose(pa_s[...])
        b_ref[0, toks, :] = jnp.transpose(pb_s[...])
        g_ref[0, toks, :] = jnp.transpose(pg_s[...])
        return carry

    lax.fori_loop(0, tt // LANES, lane_tile, 0)


def _route(h2, wpt, tt):
    bsz, seq, _ = h2.shape
    n_pair = PEER_HEADS * PEER_TOPK
    tile = lambda w: pl.BlockSpec((1, tt, w), lambda b, i: (b, i, 0))
    vregs = lambda *lead: pltpu.VMEM(lead + VREG, F32)
    return pl.pallas_call(
        functools.partial(_route_kernel, tt=tt),
        out_shape=tuple(jax.ShapeDtypeStruct((bsz, seq, n_pair), F32) for _ in range(3)),
        grid=(bsz, seq // tt),
        in_specs=[tile(D_MODEL), pl.BlockSpec((KEY_ROWS, D_MODEL), lambda b, i: (0, 0))],
        out_specs=(tile(n_pair), tile(n_pair), tile(n_pair)),
        scratch_shapes=[pltpu.VMEM((KEY_ROWS, tt), F32), vregs(PEER_NKEYS), vregs(PEER_NKEYS),
                        vregs(2, PEER_TOPK), vregs(2, PEER_TOPK), vregs(PEER_TOPK), vregs(PEER_TOPK),
                        pltpu.VMEM((n_pair, LANES), F32), pltpu.VMEM((n_pair, LANES), F32),
                        pltpu.VMEM((n_pair, LANES), F32)],
        compiler_params=pltpu.CompilerParams(dimension_semantics=("arbitrary", "arbitrary"),
                                             vmem_limit_bytes=VMEM_LIMIT),
        name="route",
    )(h2, wpt)


def _peer_kernel(h8_ref, hs_ref, x1_ref, mod_ref, a_ref, b_ref, g_ref, ut_ref, us_ref, v_ref, vs_ref, nfg_ref, y_ref,
                 gate_s, acc_s, *, tt, eb, ne):
    j = pl.program_id(2)
    n_grp = eb // PEER_NKEYS
    steps_per_half = ne // 2

    @pl.when(j == 0)
    def _():
        acc_s[...] = jnp.zeros_like(acc_s)
        sub = lax.broadcasted_iota(jnp.int32, (PEER_NKEYS, PEER_NKEYS), 0).astype(F32).astype(BF16)
        zero = jnp.zeros((PEER_NKEYS, PEER_NKEYS), BF16)
        half = jnp.full((PEER_NKEYS, PEER_NKEYS), 0.5, BF16)

        def onehots(t):
            bc = lambda ref: jnp.broadcast_to(ref[0, pl.ds(t, 1), :], (PEER_NKEYS, PEER_NKEYS)).astype(BF16)
            at = jnp.where(sub == bc(a_ref), half, zero)
            cbt = jnp.where(sub == bc(b_ref), bc(g_ref), zero)
            return at, cbt

        def build(p, carry):
            for u in range(GATE_UNROLL):
                t0 = (p * GATE_UNROLL + u) * 2
                at0, cbt0 = onehots(t0)
                at1, cbt1 = onehots(t0 + 1)
                lhs = jnp.concatenate([at0, at1], axis=1)
                rhs = jnp.concatenate([jnp.concatenate([cbt0.T, zero], axis=1),
                                       jnp.concatenate([zero, cbt1.T], axis=1)], axis=0)
                tiles = _dot(lhs, rhs)
                for k in range(2):
                    lo = tiles[0:GATE_HALF, k * PEER_NKEYS:(k + 1) * PEER_NKEYS]
                    hi = tiles[GATE_HALF:, k * PEER_NKEYS:(k + 1) * PEER_NKEYS]
                    r0 = pl.multiple_of((t0 + k) * GATE_PITCH, SUBLANES)
                    gate_s[pl.ds(r0, GATE_HALF), :] = (_bf16_bits(lo) >> 16) | _bf16_bits(hi)
            return carry

        lax.fori_loop(0, tt // (2 * GATE_UNROLL), build, 0)

    def expert_block(half):
        s = _dot(h8_ref[0], ut_ref[...])
        row_scale = hs_ref[0].astype(BF16)
        parts = []
        for gi in range(n_grp):
            lanes = slice(gi * PEER_NKEYS, (gi + 1) * PEER_NKEYS)
            r = (j - half * steps_per_half) * n_grp + gi
            packed = gate_s[pl.ds(r, tt, stride=GATE_PITCH), :]
            word = (packed & jnp.uint32(0xFFFF0000)) if half else (packed << 16)
            half_gate = lax.bitcast_convert_type(word, F32)
            act = s[:, lanes].astype(BF16) * (row_scale * us_ref[:, lanes].astype(BF16))
            parts.append(_half_gated_gelu(act, half_gate).astype(BF16))
        g = jnp.concatenate(parts, axis=1)
        g_max = jnp.maximum(jnp.max(jnp.abs(g), axis=-1, keepdims=True).astype(F32), F8_TINY)
        g_scale = (F8_TOP / g_max).astype(BF16)
        acc_s[...] += _dot((g * g_scale).astype(F8), v_ref[...]) * (1.0 / g_scale.astype(F32))

    @pl.when(j < steps_per_half)
    def _():
        expert_block(0)

    @pl.when(j >= steps_per_half)
    def _():
        expert_block(1)

    @pl.when(j == ne - 1)
    def _():
        gt2 = mod_ref[0][5:6]
        x2 = x1_ref[0] + gt2 * (acc_s[...] * vs_ref[...])
        y_ref[0] = (x2 * _rms(x2)) * nfg_ref[...]


def _peer(h8, hs, x1, mod, a_idx, b_idx, gates, ut8, us, v8, vs, nfg, tt, eb):
    bsz, seq, _ = h8.shape
    n_exp = v8.shape[0]
    ne = n_exp // eb
    assert ne % 2 == 0 and (GATE_HALF * PEER_NKEYS) % eb == 0 and tt % (2 * GATE_UNROLL) == 0
    n_pair = PEER_HEADS * PEER_TOPK
    tile = lambda w: pl.BlockSpec((1, tt, w), lambda b, i, j: (b, i, 0))
    return pl.pallas_call(
        functools.partial(_peer_kernel, tt=tt, eb=eb, ne=ne),
        out_shape=jax.ShapeDtypeStruct((bsz, seq, D_MODEL), F32),
        grid=(bsz, seq // tt, ne),
        in_specs=[
            tile(D_MODEL), tile(LANES), tile(D_MODEL),
            pl.BlockSpec((1, SUBLANES, D_MODEL), lambda b, i, j: (b, 0, 0)),
            tile(n_pair), tile(n_pair), tile(n_pair),
            pl.BlockSpec((D_MODEL, eb), lambda b, i, j: (0, j)),
            pl.BlockSpec((1, eb), lambda b, i, j: (0, j)),
            pl.BlockSpec((eb, D_MODEL), lambda b, i, j: (j, 0)),
            pl.BlockSpec((1, D_MODEL), lambda b, i, j: (0, 0)),
            pl.BlockSpec((1, D_MODEL), lambda b, i, j: (0, 0)),
        ],
        out_specs=tile(D_MODEL),
        scratch_shapes=[pltpu.VMEM((tt * GATE_PITCH, PEER_NKEYS), jnp.uint32), pltpu.VMEM((tt, D_MODEL), F32)],
        compiler_params=pltpu.CompilerParams(dimension_semantics=("arbitrary", "arbitrary", "arbitrary"),
                                             vmem_limit_bytes=VMEM_LIMIT),
        name="peer",
    )(h8, hs, x1, mod, a_idx, b_idx, gates, ut8, us, v8, vs, nfg)


def _uquant_kernel(ut_ref, u8_ref, us_ref):
    ut = ut_ref[...]
    amax = jnp.maximum(jnp.max(jnp.abs(ut), axis=0, keepdims=True), F8_TINY)
    u8_ref[...] = (ut * (F8_TOP / amax)).astype(F8)
    us_ref[...] = amax * (1.0 / F8_TOP)


def _uquant(ut, eb):
    d, n_exp = ut.shape
    return pl.pallas_call(
        _uquant_kernel,
        out_shape=(jax.ShapeDtypeStruct((d, n_exp), F8), jax.ShapeDtypeStruct((1, n_exp), F32)),
        grid=(n_exp // eb,),
        in_specs=[pl.BlockSpec((d, eb), lambda j: (0, j))],
        out_specs=(pl.BlockSpec((d, eb), lambda j: (0, j)), pl.BlockSpec((1, eb), lambda j: (0, j))),
        compiler_params=pltpu.CompilerParams(dimension_semantics=("arbitrary",), vmem_limit_bytes=VMEM_LIMIT),
        name="uquant",
    )(ut)


def _vmax_kernel(v_ref, m_ref):
    @pl.when(pl.program_id(0) == 0)
    def _():
        m_ref[...] = jnp.zeros_like(m_ref)

    m_ref[...] = jnp.maximum(m_ref[...], jnp.max(jnp.abs(v_ref[...]), axis=0, keepdims=True))


def _vcast_kernel(v_ref, m_ref, v8_ref):
    v8_ref[...] = (v_ref[...] * (F8_TOP / jnp.maximum(m_ref[...], F8_TINY))).astype(F8)


def _vquant(v, eb):
    n_exp, d = v.shape
    params = pltpu.CompilerParams(dimension_semantics=("arbitrary",), vmem_limit_bytes=VMEM_LIMIT)
    blk = pl.BlockSpec((eb, d), lambda j: (j, 0))
    row = pl.BlockSpec((1, d), lambda j: (0, 0))
    vmax = pl.pallas_call(_vmax_kernel, out_shape=jax.ShapeDtypeStruct((1, d), F32), grid=(n_exp // eb,),
                          in_specs=[blk], out_specs=row, compiler_params=params, name="vmax")(v)
    v8 = pl.pallas_call(_vcast_kernel, out_shape=jax.ShapeDtypeStruct((n_exp, d), F8), grid=(n_exp // eb,),
                        in_specs=[blk, row], out_specs=blk, compiler_params=params, name="vcast")(v, vmax)
    return v8, jnp.maximum(vmax, F8_TINY) * (1.0 / F8_TOP)


def _prep_weights(norm1_g, norm2_g, w_in, w_dec_f, b_dec_f, w_dec_b, b_dec_b, gla_norm_g, conv_w, conv_norm_g,
                  w_out, peer_wq, peer_subkeys, peer_u, peer_v, normf_g):
    w = w_in[0]
    q, k, v, g, alf, alb, cb, cc, ch = jnp.split(w, (256, 512, 1024, 1536, 1552, 1568, 2080, 2592), axis=-1)
    pad = jnp.zeros((D_MODEL, C_END - C_AL - 2 * GLA_RANK), F32)
    win = jnp.concatenate([q, k, v, g, cb, cc, ch, alf, alb, pad], axis=-1).astype(BF16)
    wdec = jnp.zeros((LANES, 2 * GLA_KW), F32)
    wdec = wdec.at[0:GLA_RANK, 0:GLA_KW].set(w_dec_f[0]).at[GLA_RANK:2 * GLA_RANK, GLA_KW:].set(w_dec_b[0])
    bdec = jnp.concatenate([b_dec_f[0], b_dec_b[0]])[None, :]
    grp = jnp.arange(CONV_WIDTH) // CONV_GDIM
    mgrp = (grp[:, None] == grp[None, :]).astype(BF16)
    wp = _keyproj(peer_wq[0], peer_subkeys[0].reshape(2 * PEER_HEADS, PEER_NKEYS, PEER_DQH))
    wpt = wp.reshape(PEER_HEADS, 2, PEER_NKEYS, D_MODEL).transpose(1, 2, 0, 3).reshape(KEY_ROWS, D_MODEL)
    ut8, us = _uquant(jnp.transpose(peer_u[0]), EXPERT_BLOCK)
    v8, vs = _vquant(peer_v[0], EXPERT_BLOCK)
    return dict(
        n1g=norm1_g[0][None, :], n2g=norm2_g[0][None, :], win=win, wdec=wdec, bdec=bdec,
        convw=conv_w[0], cng=conv_norm_g[0][None, :], mgrp=mgrp, gng=gla_norm_g[0][None, :],
        wout=w_out[0].astype(BF16), wpt=wpt.astype(BF16),
        ut8=ut8, us=us, v8=v8, vs=vs, nfg=normf_g[None, :],
    )


def _trunk(x, mod, w, tt, rtt, ptt, eb):
    qk, v, lab, of, gy = _mix1(x, mod, w["n1g"], w["win"], w["wdec"], w["bdec"], w["convw"], w["cng"], w["mgrp"], tt)
    x1, h2, h8, hs = _mix2(x, mod, qk, v, lab, of, gy, w["gng"], w["wout"], w["n2g"], tt)
    a_idx, b_idx, gates = _route(h2, w["wpt"], rtt)
    return _peer(h8, hs, x1, mod, a_idx, b_idx, gates, w["ut8"], w["us"], w["v8"], w["vs"], w["nfg"], ptt, eb)


def kernel(x_prompt, x_sample, c_prompt, c_sample, norm1_g, norm2_g, w_ada, b_ada, w_in, w_dec_f, b_dec_f,
           w_dec_b, b_dec_b, gla_norm_g, conv_w, conv_norm_g, w_out, peer_wq, peer_subkeys, peer_u, peer_v,
           normf_g):
    w = _prep_weights(norm1_g, norm2_g, w_in, w_dec_f, b_dec_f, w_dec_b, b_dec_b, gla_norm_g, conv_w,
                      conv_norm_g, w_out, peer_wq, peer_subkeys, peer_u, peer_v, normf_g)
    nb_p, nb_s = c_prompt.shape[0], c_sample.shape[0]
    c_all = jnp.concatenate([c_prompt, c_sample, jnp.zeros((SUBLANES - nb_p - nb_s, D_MODEL), F32)], axis=0)
    ada = _ada(c_all, w_ada[0], b_ada[0][None, :])
    mod = jnp.pad(ada.reshape(SUBLANES, N_ADA, D_MODEL), ((0, 0), (0, SUBLANES - N_ADA), (0, 0)))
    y_prompt = _trunk(x_prompt, mod[:nb_p], w, TOKEN_TILE, ROUTE_TILE, PEER_TOKEN_TILE, EXPERT_BLOCK)
    y_sample = _trunk(x_sample, mod[nb_p:nb_p + nb_s], w, TOKEN_TILE, ROUTE_TILE, PEER_TOKEN_TILE, EXPERT_BLOCK)
    return (y_prompt, y_sample)
```
